```python
import math
import jax, jax.numpy as jnp
from jax import lax
import numpy as np

D_MODEL = 1024
BATCH = 4
SEQ = 4096
DEPTH = 1
DEC_BATCH = 32
DEC_SEQ = 4
PAST_LEN = 8192
PAGE_SIZE = 128

A_HEADS = 8
A_HEAD_DIM = 64
A_WIDTH = A_HEADS * A_HEAD_DIM
B_HEADS = 4
B_KEY_DIM = 128
B_VAL_DIM = 128
B_KEY_WIDTH = B_HEADS * B_KEY_DIM
B_WIDTH = B_HEADS * B_VAL_DIM
MIX_WIDTH = A_WIDTH + B_WIDTH
PROJ_SPLITS = (A_WIDTH, A_WIDTH, A_WIDTH, B_KEY_WIDTH, B_KEY_WIDTH, B_WIDTH, B_WIDTH)
IN_WIDTH = 3 * A_WIDTH + 2 * B_KEY_WIDTH + 2 * B_WIDTH
MOBA_BLOCK = 256
MOBA_TOPK = 3
Q_CHUNK = 64
HG_CHUNK = 16
N_BUCKETS = 32
REL_MAX_DIST = 4096
D_FF = 2816
NORM_EPS = 1e-6
NEG_INF = -1e30

kernel_name = 'hymba_moba_hgrn2_macaron_step'


def rms_norm(x, gain):
    xf = x.astype(jnp.float32)
    y = xf * lax.rsqrt(jnp.mean(xf * xf, axis=-1, keepdims=True) + NORM_EPS)
    return (y * gain.astype(jnp.float32)).astype(x.dtype)


def swiglu(x, w_gate, w_up, w_down):
    return (jax.nn.silu(x @ w_gate) * (x @ w_up)) @ w_down


def rel_bucket(dist):
    max_exact = N_BUCKETS // 2
    n = jnp.maximum(dist, 0)
    nf = jnp.maximum(n, 1).astype(jnp.float32)
    large = max_exact + (jnp.log(nf / max_exact) / math.log(REL_MAX_DIST / max_exact)
                         * (N_BUCKETS - max_exact)).astype(jnp.int32)
    large = jnp.minimum(large, N_BUCKETS - 1)
    return jnp.where(n < max_exact, n, large)


def project(h, w_in, q_gain, k_gain, lower_bound):
    b, t, _ = h.shape
    z = h @ w_in
    offsets = [int(o) for o in np.cumsum(PROJ_SPLITS)[:-1]]
    qa, ka, va, qb, fb, ib, gb = jnp.split(z, offsets, axis=-1)
    qa = rms_norm(qa.reshape(b, t, A_HEADS, A_HEAD_DIM), q_gain)
    ka = rms_norm(ka.reshape(b, t, A_HEADS, A_HEAD_DIM), k_gain)
    va = va.reshape(b, t, A_HEADS, A_HEAD_DIM)
    lb = lower_bound.astype(jnp.float32)
    forget = lb + (1.0 - lb) * jax.nn.sigmoid(fb.astype(jnp.float32))
    log_f = jnp.log(forget).reshape(b, t, B_HEADS, B_KEY_DIM)
    kb = (1.0 - forget).reshape(b, t, B_HEADS, B_KEY_DIM)
    qb = jax.nn.silu(qb).reshape(b, t, B_HEADS, B_KEY_DIM)
    ib = ib.reshape(b, t, B_HEADS, B_VAL_DIM)
    return qa, ka, va, qb, kb, ib, log_f, gb


def hgrn2_chunked(q, k, v, log_f, s0):
    b, t, h, _ = q.shape
    c = min(HG_CHUNK, t)
    n = -(-t // c)
    pad = n * c - t

    def prep(a):
        a = jnp.pad(a.astype(jnp.float32), ((0, 0), (0, pad), (0, 0), (0, 0)))
        return a.reshape(b, n, c, h, a.shape[-1]).transpose(1, 0, 3, 2, 4)

    qc, kc, vc, gc = prep(q), prep(k), prep(v), prep(log_f)
    causal = jnp.tril(jnp.ones((c, c), dtype=bool))[:, :, None]

    def step(s, inp):
        qq, kk, vv, gg = inp
        cum = jnp.cumsum(gg, axis=2)
        inter = jnp.einsum('bhtk,bhkv->bhtv', qq * jnp.exp(cum), s)
        diff = cum[:, :, :, None, :] - cum[:, :, None, :, :]
        decay = jnp.exp(jnp.where(causal, diff, -jnp.inf))
        scores = jnp.einsum('bhtk,bhtsk,bhsk->bhts', qq, decay, kk)
        intra = jnp.einsum('bhts,bhsv->bhtv', scores, vv)
        last = cum[:, :, -1:, :]
        s_new = jnp.exp(last[:, :, 0, :])[..., None] * s + jnp.einsum(
            'bhsk,bhsv->bhkv', kk * jnp.exp(last - cum), vv)
        return s_new, inter + intra

    s_fin, out = lax.scan(step, s0.astype(jnp.float32), (qc, kc, vc, gc))
    out = out.transpose(1, 0, 3, 2, 4).reshape(b, n * c, h, -1)[:, :t]
    return out, s_fin


def moba_attend(q, q_pos, k_sel, v_sel, kpos_sel, sel_valid, k_own, v_own, kpos_own, rel_table):
    scale = A_HEAD_DIM ** -0.5
    table_t = rel_table.astype(jnp.float32).T
    own_dist = q_pos[:, None] - kpos_own[None, :]
    own_logits = jnp.einsum('bhtd,bhnd->bhtn', q, k_own, preferred_element_type=jnp.float32) * scale
    own_logits = own_logits + jnp.take(table_t, rel_bucket(own_dist), axis=1)[None]
    own_logits = jnp.where((own_dist >= 0)[None, None], own_logits, NEG_INF)
    if k_sel is None:
        p = jax.nn.softmax(own_logits, axis=-1)
        return jnp.einsum('bhtn,bhnd->bhtd', p.astype(v_own.dtype), v_own,
                          preferred_element_type=jnp.float32)
    heads = jnp.arange(q.shape[1])[None, :, None, None, None]
    sel_dist = q_pos[None, None, :, None, None] - kpos_sel
    sel_logits = jnp.einsum('bhtd,bhtjld->bhtjl', q, k_sel, preferred_element_type=jnp.float32) * scale
    sel_logits = sel_logits + table_t[heads, rel_bucket(sel_dist)]
    sel_logits = jnp.where(sel_valid, sel_logits, NEG_INF)
    b, h, t, j, L = sel_logits.shape
    logits = jnp.concatenate([sel_logits.reshape(b, h, t, j * L), own_logits], axis=-1)
    p = jax.nn.softmax(logits, axis=-1)
    p_sel = p[..., :j * L].reshape(b, h, t, j, L)
    p_own = p[..., j * L:]
    return (jnp.einsum('bhtjl,bhtjld->bhtd', p_sel.astype(v_sel.dtype), v_sel,
                       preferred_element_type=jnp.float32)
            + jnp.einsum('bhtn,bhnd->bhtd', p_own.astype(v_own.dtype), v_own,
                         preferred_element_type=jnp.float32))


def moba_prompt(q, k, v, rel_table):
    b, s, h, d = q.shape
    nb = -(-s // MOBA_BLOCK)
    pad = nb * MOBA_BLOCK - s
    qh = q.transpose(0, 2, 1, 3)
    kh = jnp.pad(k.transpose(0, 2, 1, 3), ((0, 0), (0, 0), (0, pad), (0, 0)))
    vh = jnp.pad(v.transpose(0, 2, 1, 3), ((0, 0), (0, 0), (0, pad), (0, 0)))
    k_blocks = kh.reshape(b, h, nb, MOBA_BLOCK, d)
    v_blocks = vh.reshape(b, h, nb, MOBA_BLOCK, d)
    n_full = jnp.arange(s) // MOBA_BLOCK
    n_sel = min(MOBA_TOPK, nb - 1)
    offs = jnp.arange(MOBA_BLOCK)
    bi = jnp.arange(b)[:, None, None, None]
    hi = jnp.arange(h)[None, :, None, None]
    if n_sel > 0:
        k_mean = jnp.mean(k_blocks.astype(jnp.float32), axis=3)
        gate = jnp.einsum('bhsd,bhnd->bhsn', qh.astype(jnp.float32), k_mean)
        gate = jnp.where(jnp.arange(nb)[None, :] < n_full[:, None], gate, -jnp.inf)
        _, sel_idx = lax.top_k(gate, n_sel)
        sel_valid = jnp.arange(n_sel)[None, :] < n_full[:, None]

    def chunk(ci):
        c0 = ci * Q_CHUNK
        qc = lax.dynamic_slice_in_dim(qh, c0, Q_CHUNK, axis=2)
        q_pos = c0 + jnp.arange(Q_CHUNK)
        own_start = (c0 // MOBA_BLOCK) * MOBA_BLOCK
        k_own = lax.dynamic_slice_in_dim(kh, own_start, MOBA_BLOCK, axis=2)
        v_own = lax.dynamic_slice_in_dim(vh, own_start, MOBA_BLOCK, axis=2)
        kpos_own = own_start + offs
        if n_sel == 0:
            return moba_attend(qc, q_pos, None, None, None, None, k_own, v_own, kpos_own, rel_table)
        idx = lax.dynamic_slice_in_dim(sel_idx, c0, Q_CHUNK, axis=2)
        valid = lax.dynamic_slice_in_dim(sel_valid, c0, Q_CHUNK, axis=0)[None, None, :, :, None]
        k_sel = k_blocks[bi, hi, idx]
        v_sel = v_blocks[bi, hi, idx]
        kpos_sel = idx[..., None] * MOBA_BLOCK + offs
        return moba_attend(qc, q_pos, k_sel, v_sel, kpos_sel, valid, k_own, v_own, kpos_own, rel_table)

    out = lax.map(chunk, jnp.arange(s // Q_CHUNK))
    return out.transpose(1, 0, 3, 2, 4).reshape(b, s, h, d)


def moba_sample(q, k_new, v_new, cache_k, cache_v, page_table, rel_table):
    db, t, h, d = q.shape
    k_past = cache_k[page_table].reshape(db, -1, h, d)
    v_past = cache_v[page_table].reshape(db, -1, h, d)
    past = k_past.shape[1]
    n_full = past // MOBA_BLOCK
    own_start = n_full * MOBA_BLOCK
    q_pos = past + jnp.arange(t)
    qh = q.transpose(0, 2, 1, 3)
    k_own = jnp.concatenate([k_past[:, own_start:], k_new], axis=1).transpose(0, 2, 1, 3)
    v_own = jnp.concatenate([v_past[:, own_start:], v_new], axis=1).transpose(0, 2, 1, 3)
    kpos_own = jnp.concatenate([jnp.arange(own_start, past), q_pos])
    n_sel = min(MOBA_TOPK, n_full)
    if n_sel == 0:
        out = moba_attend(qh, q_pos, None, None, None, None, k_own, v_own, kpos_own, rel_table)
        return out.transpose(0, 2, 1, 3)
    k_blocks = k_past[:, :own_start].reshape(db, n_full, MOBA_BLOCK, h, d)
    v_blocks = v_past[:, :own_start].reshape(db, n_full, MOBA_BLOCK, h, d)
    k_mean = jnp.mean(k_blocks.astype(jnp.float32), axis=2)
    gate = jnp.einsum('bhtd,bnhd->bhtn', qh.astype(jnp.float32), k_mean)
    _, idx = lax.top_k(gate, n_sel)
    bi = jnp.arange(db)[:, None, None, None]
    hi = jnp.arange(h)[None, :, None, None]
    k_sel = k_blocks[bi, idx, :, hi]
    v_sel = v_blocks[bi, idx, :, hi]
    kpos_sel = idx[..., None] * MOBA_BLOCK + jnp.arange(MOBA_BLOCK)
    valid = jnp.ones((1, 1, 1, 1, 1), dtype=bool)
    out = moba_attend(qh, q_pos, k_sel, v_sel, kpos_sel, valid, k_own, v_own, kpos_own, rel_table)
    return out.transpose(0, 2, 1, 3)


def merge_groups(attn, rec, gate, out_gain, w_out):
    b, t = gate.shape[:2]
    rec = rms_norm(rec, out_gain.reshape(B_HEADS, B_VAL_DIM)).reshape(b, t, B_WIDTH).astype(gate.dtype)
    rec = rec * jax.nn.silu(gate)
    mixed = jnp.concatenate([attn.reshape(b, t, A_WIDTH).astype(gate.dtype), rec], axis=-1)
    return mixed @ w_out


def setup_inputs(seed: int = 0) -> dict:
    key = jax.random.key(seed)
    ks = jax.random.split(key, 24)
    n_pages = PAST_LEN // PAGE_SIZE
    n_used = DEC_BATCH * n_pages
    n_phys = n_used + n_used // 4

    def nrm(k, shape, scale):
        return scale * jax.random.normal(k, shape, jnp.float32)

    def gain(k, shape):
        return 1.0 + 0.02 * jax.random.normal(k, shape, jnp.float32)

    page_table = jax.random.permutation(ks[5], n_phys)[:n_used].reshape(DEC_BATCH, n_pages).astype(jnp.int32)
    return {
        'x_prompt': nrm(ks[0], (BATCH, SEQ, D_MODEL), 1.0),
        'x_sample': nrm(ks[1], (DEC_BATCH, DEC_SEQ, D_MODEL), 1.0),
        'cache_k': nrm(ks[2], (DEPTH, n_phys, PAGE_SIZE, A_HEADS, A_HEAD_DIM), 1.0),
        'cache_v': nrm(ks[3], (DEPTH, n_phys, PAGE_SIZE, A_HEADS, A_HEAD_DIM), 1.0),
        'state_hgrn': nrm(ks[4], (DEPTH, DEC_BATCH, B_HEADS, B_KEY_DIM, B_VAL_DIM), 0.5),
        'page_table': page_table,
        'rel_bias_table': nrm(ks[6], (N_BUCKETS, A_HEADS), 0.5),
        'lb_logits': nrm(ks[7], (DEPTH + 1, B_KEY_WIDTH), 1.0),
        'ffn1_norm': gain(ks[8], (DEPTH, D_MODEL)),
        'ffn1_gate': nrm(ks[9], (DEPTH, D_MODEL, D_FF), D_MODEL ** -0.5),
        'ffn1_up': nrm(ks[10], (DEPTH, D_MODEL, D_FF), D_MODEL ** -0.5),
        'ffn1_down': nrm(ks[11], (DEPTH, D_FF, D_MODEL), D_FF ** -0.5),
        'mix_norm': gain(ks[12], (DEPTH, D_MODEL)),
        'w_in': nrm(ks[13], (DEPTH, D_MODEL, IN_WIDTH), D_MODEL ** -0.5),
        'q_norm': gain(ks[14], (DEPTH, A_HEAD_DIM)),
        'k_norm': gain(ks[15], (DEPTH, A_HEAD_DIM)),
        'hgrn_out_norm': gain(ks[16], (DEPTH, B_WIDTH)),
        'w_out': nrm(ks[17], (DEPTH, MIX_WIDTH, D_MODEL), MIX_WIDTH ** -0.5),
        'ffn2_norm': gain(ks[18], (DEPTH, D_MODEL)),
        'ffn2_gate': nrm(ks[19], (DEPTH, D_MODEL, D_FF), D_MODEL ** -0.5),
        'ffn2_up': nrm(ks[20], (DEPTH, D_MODEL, D_FF), D_MODEL ** -0.5),
        'ffn2_down': nrm(ks[21], (DEPTH, D_FF, D_MODEL), D_FF ** -0.5),
    }


def reference(x_prompt, x_sample, cache_k, cache_v, state_hgrn, page_table, rel_bias_table,
              lb_logits, ffn1_norm, ffn1_gate, ffn1_up, ffn1_down, mix_norm, w_in, q_norm,
              k_norm, hgrn_out_norm, w_out, ffn2_norm, ffn2_gate, ffn2_up, ffn2_down):
    lower_bounds = jnp.cumsum(jax.nn.softmax(lb_logits.astype(jnp.float32), axis=0), axis=0)
    yp, ys = x_prompt, x_sample
    kp_rows, vp_rows, sp_rows, ks_rows, vs_rows, ss_rows = [], [], [], [], [], []
    for l in range(DEPTH):
        yp = yp + 0.5 * swiglu(rms_norm(yp, ffn1_norm[l]), ffn1_gate[l], ffn1_up[l], ffn1_down[l])
        ys = ys + 0.5 * swiglu(rms_norm(ys, ffn1_norm[l]), ffn1_gate[l], ffn1_up[l], ffn1_down[l])

        qa, ka, va, qb, kb, ib, lf, gb = project(rms_norm(yp, mix_norm[l]), w_in[l], q_norm[l],
                                                 k_norm[l], lower_bounds[l])
        attn = moba_prompt(qa, ka, va, rel_bias_table)
        s0 = jnp.zeros((yp.shape[0], B_HEADS, B_KEY_DIM, B_VAL_DIM), jnp.float32)
        rec, s_fin = hgrn2_chunked(qb, kb, ib, lf, s0)
        yp = yp + merge_groups(attn, rec, gb, hgrn_out_norm[l], w_out[l])
        kp_rows.append(ka)
        vp_rows.append(va)
        sp_rows.append(s_fin.astype(state_hgrn.dtype))

        qa, ka, va, qb, kb, ib, lf, gb = project(rms_norm(ys, mix_norm[l]), w_in[l], q_norm[l],
                                                 k_norm[l], lower_bounds[l])
        attn = moba_sample(qa, ka, va, cache_k[l], cache_v[l], page_table, rel_bias_table)
        rec, s_new = hgrn2_chunked(qb, kb, ib, lf, state_hgrn[l])
        ys = ys + merge_groups(attn, rec, gb, hgrn_out_norm[l], w_out[l])
        ks_rows.append(ka)
        vs_rows.append(va)
        ss_rows.append(s_new.astype(state_hgrn.dtype))

        yp = yp + 0.5 * swiglu(rms_norm(yp, ffn2_norm[l]), ffn2_gate[l], ffn2_up[l], ffn2_down[l])
        ys = ys + 0.5 * swiglu(rms_norm(ys, ffn2_norm[l]), ffn2_gate[l], ffn2_up[l], ffn2_down[l])

    k_prompt = jnp.stack(kp_rows)
    v_prompt = jnp.stack(vp_rows)
    state_prompt = jnp.stack(sp_rows)
    k_sample = jnp.stack(ks_rows)
    v_sample = jnp.stack(vs_rows)
    state_sample = jnp.stack(ss_rows)
    return (yp, ys, k_prompt, v_prompt, state_prompt, k_sample, v_sample, state_sample)
```

```python
import functools
import math

import jax
import jax.numpy as jnp
from jax import lax
from jax.experimental import pallas as pl
from jax.experimental.pallas import tpu as pltpu

F32 = jnp.float32
BF16 = jnp.bfloat16

D_MODEL = 1024
A_HEADS = 8
A_HEAD_DIM = 64
A_WIDTH = A_HEADS * A_HEAD_DIM
B_HEADS = 4
B_DIM = 128
B_WIDTH = B_HEADS * B_DIM
GROUP_W = 512
N_GROUPS = 7
D_FF = 2816
MOBA_BLOCK = 256
MOBA_TOPK = 3
PAGE_SIZE = 128
N_BUCKETS = 32
MAX_EXACT = N_BUCKETS // 2
REL_MAX_DIST = 4096
HG_CHUNK = 16
HG_BLOCK = 128
NORM_EPS = 1e-6
NEG_INF = -1e30
ATTN_SCALE = A_HEAD_DIM ** -0.5

FF_CHUNK = 256
VMEM_LIMIT = 48 * 1024 * 1024


def _resident(shape):
    nd = len(shape)
    return pl.BlockSpec(shape, lambda *_: (0,) * nd, pipeline_mode=pl.Buffered(1))


def _params(semantics):
    return pltpu.CompilerParams(dimension_semantics=semantics, vmem_limit_bytes=VMEM_LIMIT)


def _dot(a, b):
    return jnp.dot(a, b, preferred_element_type=F32)


def _dot_nt(a, b):
    return lax.dot_general(a, b, (((1,), (1,)), ((), ())), preferred_element_type=F32)


def _split3(a):
    a1 = a.astype(BF16)
    r1 = a - a1.astype(F32)
    a2 = r1.astype(BF16)
    a3 = (r1 - a2.astype(F32)).astype(BF16)
    return a1, a2, a3


def _div_pow2(x, d):
    return lax.shift_right_logical(x, int(math.log2(d)))


def _rms_rows(x, gain):
    ms = jnp.mean(x * x, axis=-1, keepdims=True)
    return x * lax.rsqrt(ms + NORM_EPS) * gain


def _silu(x):
    return x * jax.nn.sigmoid(x)


def _ffn_kernel(*refs, with_mix):
    if with_mix:
        x_ref, attn_ref, rec_ref, wo_ref, gain_ref, wg_ref, wu_ref, wd_ref, o_ref = refs
        y = (x_ref[...] + _dot(attn_ref[...], wo_ref[:A_WIDTH, :])
             + _dot(rec_ref[...], wo_ref[A_WIDTH:, :]))
    else:
        x_ref, gain_ref, wg_ref, wu_ref, wd_ref, o_ref = refs
        y = x_ref[...]
    h = _rms_rows(y, gain_ref[...]).astype(BF16)
    acc = jnp.zeros(y.shape, F32)
    for j in range(D_FF // FF_CHUNK):
        cols = slice(j * FF_CHUNK, (j + 1) * FF_CHUNK)
        g = _dot(h, wg_ref[:, cols])
        u = _dot(h, wu_ref[:, cols])
        acc = acc + _dot((_silu(g) * u).astype(BF16), wd_ref[cols, :])
    o_ref[...] = y + 0.5 * acc


def _ffn(x, gain, wg, wu, wd, tm, mix=None):
    n = x.shape[0]
    row = lambda w: pl.BlockSpec((tm, w), lambda i: (i, 0))
    args, specs = [x], [row(D_MODEL)]
    if mix is not None:
        attn, rec, wo = mix
        args += [attn, rec, wo]
        specs += [row(A_WIDTH), row(B_WIDTH), _resident(wo.shape)]
    args += [gain, wg, wu, wd]
    specs += [_resident(gain.shape), _resident(wg.shape), _resident(wu.shape), _resident(wd.shape)]
    return pl.pallas_call(
        functools.partial(_ffn_kernel, with_mix=mix is not None),
        out_shape=jax.ShapeDtypeStruct((n, D_MODEL), F32),
        grid=(n // tm,),
        in_specs=specs,
        out_specs=row(D_MODEL),
        compiler_params=_params(("parallel",)),
        name="ffn_mix" if mix is not None else "ffn",
    )(*args)


def _proj_kernel(y_ref, gain_ref, wat_ref, wb_ref, qg_ref, kg_ref, lbl_ref,
                 q_ref, k_ref, v_ref, qb_ref, kb_ref, ib_ref, lf_ref, gb_ref):
    h = _rms_rows(y_ref[...], gain_ref[...]).astype(BF16)
    tm = h.shape[0]

    def group_t(i):
        return _dot_nt(wat_ref[i * GROUP_W:(i + 1) * GROUP_W, :], h)

    def group(i):
        return _dot(h, wb_ref[:, i * GROUP_W:(i + 1) * GROUP_W])

    def head_norm(z, g):
        out = []
        for hd in range(A_HEADS):
            zh = z[hd * A_HEAD_DIM:(hd + 1) * A_HEAD_DIM, :]
            out.append(zh * lax.rsqrt(jnp.mean(zh * zh, axis=0, keepdims=True) + NORM_EPS))
        return jnp.concatenate(out, axis=0) * jnp.concatenate([g] * pl.cdiv(tm, 128), axis=1)[:, :tm]

    q_ref[...] = head_norm(group_t(0), qg_ref[...]) * ATTN_SCALE
    k_ref[...] = head_norm(group_t(1), kg_ref[...])
    v_ref[...] = group_t(2)
    qb_ref[...] = _silu(group(0))
    lbl = lbl_ref[...]
    e = jnp.exp(lbl - jnp.max(lbl, axis=0, keepdims=True))
    lb = e[0:1, :] / jnp.sum(e, axis=0, keepdims=True)
    forget = lb + (1.0 - lb) * jax.nn.sigmoid(group(1))
    lf_ref[...] = jnp.log(forget)
    kb_ref[...] = 1.0 - forget
    ib_ref[...] = group(2)
    gb_ref[...] = group(3)


def _proj(y, gain, wat, wb, qg, kg, lbl, seq, tm):
    n = y.shape[0]
    nt = seq // tm
    row = lambda w: pl.BlockSpec((tm, w), lambda i: (i, 0))
    col = pl.BlockSpec((GROUP_W, tm), lambda i: (i // nt, i % nt))
    out_t = jax.ShapeDtypeStruct((n // seq * GROUP_W, seq), F32)
    out = jax.ShapeDtypeStruct((n, GROUP_W), F32)
    return pl.pallas_call(
        _proj_kernel,
        out_shape=(out_t,) * 3 + (out,) * 5,
        grid=(n // tm,),
        in_specs=[row(D_MODEL)] + [_resident(a.shape) for a in (gain, wat, wb, qg, kg, lbl)],
        out_specs=(col,) * 3 + (row(GROUP_W),) * 5,
        compiler_params=_params(("parallel",)),
        name="proj",
    )(y, gain, wat, wb, qg, kg, lbl)


def _rel_bucket(n):
    nf = jnp.maximum(n, 1).astype(F32)
    large = MAX_EXACT + (jnp.log(nf / MAX_EXACT) / math.log(REL_MAX_DIST / MAX_EXACT)
                         * (N_BUCKETS - MAX_EXACT)).astype(jnp.int32)
    large = jnp.minimum(large, N_BUCKETS - 1)
    return jnp.where(n < MAX_EXACT, n, large)


def _bias_kernel(tab_ref, w_ref, sbt_ref, ob_ref, *, past, dec_seq):
    tab = tab_ref[...]
    tab_rows = jnp.concatenate([tab] * dec_seq, axis=0)

    def lookup(bucket, table):
        out = jnp.zeros(bucket.shape, F32)
        for b in range(N_BUCKETS):
            out = jnp.where(bucket == b, table[:, b:b + 1], out)
        return out

    c = lax.broadcasted_iota(jnp.int32, w_ref.shape, 1)
    w_ref[...] = lookup(_rel_bucket(jnp.maximum(c - MOBA_BLOCK, 0)), tab)
    t = _div_pow2(lax.broadcasted_iota(jnp.int32, sbt_ref.shape, 0), A_HEADS)
    kpos = lax.broadcasted_iota(jnp.int32, sbt_ref.shape, 1)
    sbt_ref[...] = lookup(_rel_bucket(jnp.maximum(past + t - kpos, 0)), tab_rows)
    t = _div_pow2(lax.broadcasted_iota(jnp.int32, ob_ref.shape, 0), A_HEADS)
    s = lax.broadcasted_iota(jnp.int32, ob_ref.shape, 1)
    ob_ref[...] = lookup(_rel_bucket(jnp.maximum(t - s, 0)), tab_rows)


def _bias_tables(tab_t, seq, past, dec_seq):
    rows = dec_seq * A_HEADS
    return pl.pallas_call(
        functools.partial(_bias_kernel, past=past, dec_seq=dec_seq),
        out_shape=(jax.ShapeDtypeStruct((A_HEADS, seq + MOBA_BLOCK), F32),
                   jax.ShapeDtypeStruct((rows, past), F32),
                   jax.ShapeDtypeStruct((rows, 128), F32)),
        name="rel_bias",
    )(tab_t)


def _top_k_mask(gate, k, axis):
    index = lax.broadcasted_iota(jnp.int32, gate.shape, axis).astype(F32)
    sel = jnp.zeros(gate.shape, F32)
    for _ in range(k):
        mx = jnp.max(gate, axis=axis, keepdims=True)
        first = jnp.min(jnp.where(gate == mx, index, float(gate.shape[axis])), axis=axis, keepdims=True)
        chosen = index == first
        sel = jnp.where(chosen & (mx > -jnp.inf), 1.0, sel)
        gate = jnp.where(chosen, -jnp.inf, gate)
    return sel


def _moba_prompt_kernel(q_ref, k_ref, v_ref, w_ref, o_ref, kb_ref, vt_ref, km_ref, tt_ref, *, nb):
    b = pl.program_id(1)
    i = pl.program_id(2)
    blk = MOBA_BLOCK

    @pl.when((b == 0) & (i == 0))
    def _build_bias_tiles():
        for hh in range(2):
            for d in range(nb):
                win = jnp.broadcast_to(w_ref[0, hh:hh + 1, d * blk:(d + 2) * blk], (blk, 2 * blk))
                tt_ref[hh, d] = pltpu.roll(win, 0, 1, stride=1, stride_axis=0)[:, blk:]

    @pl.when(i == 0)
    def _prepare_sequence():
        for n in range(nb):
            cols = slice(n * blk, (n + 1) * blk)
            kblk = k_ref[:, cols].T
            kb_ref[n] = kblk.astype(BF16)
            km_ref[n:n + 1, :] = jnp.mean(kblk, axis=0, keepdims=True)
            vt_ref[n] = v_ref[:, cols].astype(BF16)

    q_t = q_ref[...]
    feat = lax.broadcasted_iota(jnp.int32, q_t.shape, 0)
    km = km_ref[...]
    km_lane = lax.broadcasted_iota(jnp.int32, km.shape, 1)
    blk_id = lax.broadcasted_iota(jnp.int32, (nb, blk), 0)
    key_j = lax.broadcasted_iota(jnp.int32, (blk, blk), 0)
    qry_c = lax.broadcasted_iota(jnp.int32, (blk, blk), 1)

    qp, sel, carry = [], [], []
    for hh in range(2):
        lo = hh * A_HEAD_DIM
        q_h = jnp.where((feat >= lo) & (feat < lo + A_HEAD_DIM), q_t, 0.0)
        km_h = jnp.where((km_lane >= lo) & (km_lane < lo + A_HEAD_DIM), km, 0.0)
        k1, k2, _ = _split3(km_h)
        q1, q2, _ = _split3(q_h)
        gate = _dot(k1, q1) + _dot(k1, q2) + _dot(k2, q1)
        gate = jnp.where(blk_id < i, gate, -jnp.inf)
        sel.append(_top_k_mask(gate, MOBA_TOPK, 0))
        qp.append(q1)
        s = _dot(kb_ref[i], q1) + tt_ref[hh, 0]
        s = jnp.where(qry_c >= key_j, s, NEG_INF)
        m = jnp.max(s, axis=0, keepdims=True)
        p = jnp.exp(s - m)
        carry += [m, jnp.sum(p, axis=0, keepdims=True),
                  _dot(vt_ref[i, lo:lo + A_HEAD_DIM, :], p.astype(BF16))]

    def past_block(n, carry):
        out = []
        for hh in range(2):
            lo = hh * A_HEAD_DIM
            m, l, acc = carry[3 * hh:3 * hh + 3]
            picked = jnp.sum(jnp.where(blk_id == n, sel[hh], 0.0), axis=0, keepdims=True)
            s = _dot(kb_ref[n], qp[hh]) + tt_ref[hh, i - n]
            s = jnp.where(picked > 0.0, s, NEG_INF)
            m_new = jnp.maximum(m, jnp.max(s, axis=0, keepdims=True))
            alpha = jnp.exp(m - m_new)
            p = jnp.exp(s - m_new)
            out += [m_new, alpha * l + jnp.sum(p, axis=0, keepdims=True),
                    alpha * acc + _dot(vt_ref[n, lo:lo + A_HEAD_DIM, :], p.astype(BF16))]
        return tuple(out)

    carry = lax.fori_loop(0, i, past_block, tuple(carry))
    out_t = jnp.concatenate([carry[2] / carry[1], carry[5] / carry[4]], axis=0)
    o_ref[...] = out_t.T.astype(o_ref.dtype)


def _moba_prompt(q_t, k_t, v_t, w_pairs, batch, seq):
    nb = seq // MOBA_BLOCK
    pairs = A_HEADS // 2
    q_tile = pl.BlockSpec((128, MOBA_BLOCK), lambda p, b, i: (b * pairs + p, i))
    whole = pl.BlockSpec((128, seq), lambda p, b, i: (b * pairs + p, 0))
    return pl.pallas_call(
        functools.partial(_moba_prompt_kernel, nb=nb),
        out_shape=jax.ShapeDtypeStruct((batch * seq, A_WIDTH), BF16),
        grid=(pairs, batch, nb),
        in_specs=[q_tile, whole, whole,
                  pl.BlockSpec((1, 2, seq + MOBA_BLOCK), lambda p, b, i: (p, 0, 0))],
        out_specs=pl.BlockSpec((MOBA_BLOCK, 128), lambda p, b, i: (b * nb + i, p)),
        scratch_shapes=[pltpu.VMEM((nb, MOBA_BLOCK, 128), BF16),
                        pltpu.VMEM((nb, 128, MOBA_BLOCK), BF16),
                        pltpu.VMEM((nb, 128), F32),
                        pltpu.VMEM((2, nb, MOBA_BLOCK, MOBA_BLOCK), F32)],
        compiler_params=_params(("arbitrary", "arbitrary", "arbitrary")),
        name="moba_prompt",
    )(q_t, k_t, v_t, w_pairs)


def _moba_sample_kernel(pt_ref, q_ref, kn_ref, vn_ref, sbt_ref, ob_ref, *refs, nbk, dec_seq, bps):
    del pt_ref
    n_pages = 2 * bps
    k_refs, v_refs = refs[:n_pages], refs[n_pages:2 * n_pages]
    o_ref, m_sc, l_sc, acc_sc, g_sc, q16_sc = refs[2 * n_pages:]
    j = pl.program_id(1)
    rows = dec_seq * A_HEADS
    head_of_row = lax.broadcasted_iota(jnp.int32, (A_HEADS, A_WIDTH), 0)
    head_of_lane = _div_pow2(lax.broadcasted_iota(jnp.int32, (A_HEADS, A_WIDTH), 1), A_HEAD_DIM)
    head_mask = head_of_row == head_of_lane
    wide = lambda a: jnp.broadcast_to(a, (rows, 128))

    @pl.when(j == 0)
    def _expand_queries():
        for t in range(dec_seq):
            q_t = jnp.where(head_mask, jnp.broadcast_to(q_ref[0, t:t + 1, :], head_mask.shape), 0.0)
            q16_sc[t * A_HEADS:(t + 1) * A_HEADS, :] = q_t.astype(BF16)

    q16 = q16_sc[...]
    for r in range(bps):
        n = j * bps + r
        raw = jnp.concatenate([_dot(q16, k_refs[2 * r + pg][0].astype(BF16)) for pg in range(2)], axis=1)
        s = raw + sbt_ref[:, r * MOBA_BLOCK:(r + 1) * MOBA_BLOCK]
        m = jnp.max(s, axis=1, keepdims=True)
        p = jnp.exp(s - m)
        m_sc[n] = wide(m)
        l_sc[n] = wide(jnp.sum(p, axis=1, keepdims=True))
        p = p.astype(BF16)
        acc_sc[n] = (_dot_nt(p[:, :PAGE_SIZE], v_refs[2 * r][0].astype(BF16))
                     + _dot_nt(p[:, PAGE_SIZE:], v_refs[2 * r + 1][0].astype(BF16)))
        g_sc[n] = wide(jnp.sum(raw, axis=1, keepdims=True) * (1.0 / MOBA_BLOCK))

    @pl.when(j == pl.num_programs(1) - 1)
    def _merge():
        gates = [g_sc[n] for n in range(nbk)]
        picked = [jnp.zeros((rows, 128), jnp.bool_)] * nbk
        for _ in range(min(MOBA_TOPK, nbk)):
            mx = functools.reduce(jnp.maximum, gates)
            first = functools.reduce(
                jnp.minimum, [jnp.where(g == mx, float(n), float(nbk)) for n, g in enumerate(gates)])
            for n in range(nbk):
                chosen = first == float(n)
                picked[n] = picked[n] | (chosen & (mx > -jnp.inf))
                gates[n] = jnp.where(chosen, -jnp.inf, gates[n])
        zeros = jnp.zeros((128 - kn_ref.shape[1], A_WIDTH), F32)
        k_own = jnp.concatenate([kn_ref[0], zeros], axis=0).astype(BF16)
        v_own = jnp.concatenate([vn_ref[0], zeros], axis=0).astype(BF16)
        s = _dot_nt(q16, k_own) + ob_ref[...]
        t_of_row = _div_pow2(lax.broadcasted_iota(jnp.int32, s.shape, 0), A_HEADS)
        s_idx = lax.broadcasted_iota(jnp.int32, s.shape, 1)
        s = jnp.where(s_idx <= t_of_row, s, NEG_INF)
        m_own = wide(jnp.max(s, axis=1, keepdims=True))
        p = jnp.exp(s - m_own)
        l_own = wide(jnp.sum(p, axis=1, keepdims=True))
        acc_own = _dot(p.astype(BF16), v_own)
        m_all = m_own
        for n in range(nbk):
            m_all = jnp.maximum(m_all, jnp.where(picked[n], m_sc[n], -jnp.inf))
        lanes4 = lambda a: jnp.concatenate([a] * (A_WIDTH // 128), axis=1)
        w = jnp.exp(m_own - m_all)
        l_all = w * l_own
        acc = lanes4(w) * acc_own
        for n in range(nbk):
            w = jnp.where(picked[n], jnp.exp(m_sc[n] - m_all), 0.0)
            l_all = l_all + w * l_sc[n]
            acc = acc + lanes4(w) * acc_sc[n]
        res = acc / lanes4(l_all)
        out = [jnp.sum(jnp.where(head_mask, res[t * A_HEADS:(t + 1) * A_HEADS, :], 0.0),
                       axis=0, keepdims=True) for t in range(dec_seq)]
        o_ref[0] = jnp.concatenate(out, axis=0).astype(o_ref.dtype)


def _moba_sample(q, k_new, v_new, cache_k, cache_v, page_table, sbt, ob, bps=2):
    db, dec_seq, _ = q.shape
    n_pages = page_table.shape[1]
    nbk = n_pages * PAGE_SIZE // MOBA_BLOCK
    rows = dec_seq * A_HEADS
    ppb = MOBA_BLOCK // PAGE_SIZE
    assert ppb == 2 and nbk % bps == 0
    per_seq = lambda b, j, pt: (b, 0, 0)
    page = lambda r: pl.BlockSpec(
        (1, A_WIDTH, PAGE_SIZE), lambda b, j, pt: (pt[b * n_pages + j * (ppb * bps) + r], 0, 0))
    pages = [page(r) for r in range(ppb * bps)]
    grid_spec = pltpu.PrefetchScalarGridSpec(
        num_scalar_prefetch=1,
        grid=(db, nbk // bps),
        in_specs=[pl.BlockSpec((1, dec_seq, A_WIDTH), per_seq),
                  pl.BlockSpec((1, k_new.shape[1], A_WIDTH), per_seq),
                  pl.BlockSpec((1, v_new.shape[1], A_WIDTH), per_seq),
                  pl.BlockSpec((rows, bps * MOBA_BLOCK), lambda b, j, pt: (0, j)),
                  pl.BlockSpec((rows, 128), lambda b, j, pt: (0, 0))] + pages + pages,
        out_specs=pl.BlockSpec((1, dec_seq, A_WIDTH), per_seq),
        scratch_shapes=[pltpu.VMEM((nbk, rows, 128), F32),
                        pltpu.VMEM((nbk, rows, 128), F32),
                        pltpu.VMEM((nbk, rows, A_WIDTH), F32),
                        pltpu.VMEM((nbk, rows, 128), F32),
                        pltpu.VMEM((rows, A_WIDTH), BF16)])
    return pl.pallas_call(
        functools.partial(_moba_sample_kernel, nbk=nbk, dec_seq=dec_seq, bps=bps),
        out_shape=jax.ShapeDtypeStruct((db, dec_seq, A_WIDTH), BF16),
        grid_spec=grid_spec,
        compiler_params=_params(("arbitrary", "arbitrary")),
        name="moba_sample",
    )(page_table.reshape(-1), q, k_new, v_new, sbt, ob,
      *([cache_k] * (ppb * bps)), *([cache_v] * (ppb * bps)))


def _hgrn_kernel(q_ref, k_ref, v_ref, g_ref, gate_ref, s0_ref, og_ref, rec_ref, sfin_ref,
                 st_ref, cum_ref, o_ref, *, n_chunks):
    tb = pl.program_id(2)
    n_blocks = q_ref.shape[0] // HG_BLOCK
    row = lax.broadcasted_iota(jnp.int32, (HG_BLOCK, HG_BLOCK), 0)
    col = lax.broadcasted_iota(jnp.int32, (HG_BLOCK, HG_BLOCK), 1)
    tril = jnp.where(row >= col, 1.0, 0.0).astype(BF16)
    ones = jnp.ones((HG_BLOCK, HG_BLOCK), BF16)
    chunk_row = lax.broadcasted_iota(jnp.int32, (HG_CHUNK, B_DIM), 0)

    @pl.when(tb == 0)
    def _load_state():
        st_ref[...] = s0_ref[0, 0].T

    def block(bi, st):
        r0 = pl.multiple_of(bi * HG_BLOCK, HG_BLOCK)
        rows = pl.ds(r0, HG_BLOCK)
        q, k, v = q_ref[rows, :], k_ref[rows, :], v_ref[rows, :]
        g1, g2, g3 = _split3(g_ref[rows, :])
        cum = _dot(tril, g1) + _dot(tril, g2) + _dot(tril, g3)
        cum_ref[...] = cum
        last = cum[HG_BLOCK - 1:HG_BLOCK, :]
        v16 = v.astype(BF16)
        o = _dot_nt((q * jnp.exp(cum)).astype(BF16), st.astype(BF16))
        st_new = st * jnp.exp(last) + _dot(v.T.astype(BF16), (k * jnp.exp(last - cum)).astype(BF16))
        parts = [jnp.zeros((HG_CHUNK, B_DIM), F32)]
        for c in range(1, n_chunks):
            lo = c * HG_CHUNK
            r = cum[lo - 1:lo, :]
            k_c = (k[:lo] * jnp.exp(r - cum[:lo])).astype(BF16)
            k_c = jnp.concatenate([k_c, jnp.zeros((HG_BLOCK - lo, B_DIM), BF16)], axis=0)
            q_c = (q[lo:lo + HG_CHUNK] * jnp.exp(cum[lo:lo + HG_CHUNK] - r)).astype(BF16)
            parts.append(_dot(_dot_nt(q_c, k_c).astype(BF16), v16))
        used = n_chunks * HG_CHUNK
        if used < HG_BLOCK:
            parts.append(jnp.zeros((HG_BLOCK - used, B_DIM), F32))
        o_ref[rows, :] = o + jnp.concatenate(parts, axis=0)

        def chunk(c, _):
            crow = pl.ds(pl.multiple_of(r0 + c * HG_CHUNK, HG_CHUNK), HG_CHUNK)
            cum_c = cum_ref[pl.ds(pl.multiple_of(c * HG_CHUNK, HG_CHUNK), HG_CHUNK), :]
            q_c, k_c, v_c = q_ref[crow, :], k_ref[crow, :], v_ref[crow, :]
            terms = []
            for s in range(HG_CHUNK):
                decay = jnp.exp(jnp.where(chunk_row >= s, cum_c - cum_c[s:s + 1, :], -jnp.inf))
                terms.append(q_c * decay * k_c[s:s + 1, :])
            score = _dot(jnp.concatenate(terms, axis=0).astype(BF16), ones)
            intra = jnp.zeros((HG_CHUNK, B_DIM), F32)
            for s in range(HG_CHUNK):
                intra = intra + score[s * HG_CHUNK:(s + 1) * HG_CHUNK, :] * v_c[s:s + 1, :]
            o_ref[crow, :] = o_ref[crow, :] + intra
            return 0

        lax.fori_loop(0, n_chunks, chunk, 0)
        return st_new

    st = lax.fori_loop(0, n_blocks, block, st_ref[...])
    st_ref[...] = st

    o = o_ref[...]
    ms = _dot((o * o).astype(BF16), ones) * (1.0 / B_DIM)
    gate = gate_ref[...]
    rec_ref[...] = ((o * lax.rsqrt(ms + NORM_EPS) * og_ref[...]) * _silu(gate)).astype(rec_ref.dtype)

    @pl.when(tb == pl.num_programs(2) - 1)
    def _store_state():
        sfin_ref[0, 0] = st.T


def _hgrn(qb, kb, ib, lf, gb, s0, out_gain, batch, seq, tile, n_chunks):
    nt = seq // tile
    tok = pl.BlockSpec((tile, B_DIM), lambda b, h, t: (b * nt + t, h))
    state = pl.BlockSpec((1, 1, B_DIM, B_DIM), lambda b, h, t: (b, h, 0, 0))
    return pl.pallas_call(
        functools.partial(_hgrn_kernel, n_chunks=n_chunks),
        out_shape=(jax.ShapeDtypeStruct((batch * seq, B_WIDTH), BF16),
                   jax.ShapeDtypeStruct(s0.shape, F32)),
        grid=(batch, B_HEADS, nt),
        in_specs=[tok, tok, tok, tok, tok, state, pl.BlockSpec((1, B_DIM), lambda b, h, t: (0, h))],
        out_specs=(tok, state),
        scratch_shapes=[pltpu.VMEM((B_DIM, B_DIM), F32),
                        pltpu.VMEM((HG_BLOCK, B_DIM), F32),
                        pltpu.VMEM((tile, B_DIM), F32)],
        compiler_params=_params(("arbitrary", "arbitrary", "arbitrary")),
        name="hgrn",
    )(qb, kb, ib, lf, gb, s0, out_gain)


def kernel(x_prompt, x_sample, cache_k, cache_v, state_hgrn, page_table, rel_bias_table, lb_logits,
           ffn1_norm, ffn1_gate, ffn1_up, ffn1_down, mix_norm, w_in, q_norm, k_norm, hgrn_out_norm,
           w_out, ffn2_norm, ffn2_gate, ffn2_up, ffn2_down):
    batch, seq, _ = x_prompt.shape
    db, dec_seq, _ = x_sample.shape
    depth, n_phys = cache_k.shape[:2]
    assert depth == 1 and lb_logits.shape[0] == 2
    past = page_table.shape[1] * PAGE_SIZE
    assert seq % MOBA_BLOCK == 0 and past % MOBA_BLOCK == 0 and dec_seq <= HG_CHUNK
    n_p, n_s = batch * seq, db * dec_seq

    bf = lambda w: w[0].astype(BF16)
    gain = lambda g: g[0].reshape(1, -1)
    ffn1 = (gain(ffn1_norm), bf(ffn1_gate), bf(ffn1_up), bf(ffn1_down))
    ffn2 = (gain(ffn2_norm), bf(ffn2_gate), bf(ffn2_up), bf(ffn2_down))
    wo = bf(w_out)
    w_in16 = bf(w_in)
    per_lane = lambda g: jnp.broadcast_to(jnp.tile(g[0], A_HEADS)[:, None], (A_WIDTH, 128))
    proj_w = (gain(mix_norm), w_in16[:, :3 * GROUP_W].T, w_in16[:, 3 * GROUP_W:],
              per_lane(q_norm), per_lane(k_norm), lb_logits)
    out_gain = hgrn_out_norm[0].reshape(1, -1)
    w_bias, sbt, ob = _bias_tables(rel_bias_table.T, seq, past, dec_seq)

    tm = 512
    y = _ffn(x_prompt.reshape(n_p, D_MODEL), *ffn1, tm)
    q_t, kp_t, vp_t, qb, kb, ib, lf, gb = _proj(y, *proj_w, seq, tm)
    attn = _moba_prompt(q_t, kp_t, vp_t, w_bias.reshape(A_HEADS // 2, 2, -1), batch, seq)
    s0 = jnp.zeros((batch, B_HEADS, B_DIM, B_DIM), F32)
    rec, s_p = _hgrn(qb, kb, ib, lf, gb, s0, out_gain, batch, seq, 512, HG_BLOCK // HG_CHUNK)
    y_p = _ffn(y, *ffn2, tm, mix=(attn, rec, wo))
    heads_p = lambda a: a.reshape(batch, A_HEADS, A_HEAD_DIM, seq).transpose(0, 3, 1, 2)[None]

    ys = _ffn(x_sample.reshape(n_s, D_MODEL), *ffn1, n_s)
    q_t, k_t, v_t, qb, kb, ib, lf, gb = _proj(ys, *proj_w, n_s, n_s)
    q_s, k_s, v_s = (a.T.reshape(db, dec_seq, A_WIDTH) for a in (q_t, k_t, v_t))
    pad_rows = lambda a, r: jnp.pad(a.reshape(db, dec_seq, -1), ((0, 0), (0, r - dec_seq), (0, 0)))
    pages = lambda c: c[0].transpose(0, 2, 3, 1).reshape(n_phys, A_WIDTH, PAGE_SIZE)
    attn = _moba_sample(q_s, pad_rows(k_s, 8), pad_rows(v_s, 8), pages(cache_k), pages(cache_v),
                        page_table, sbt, ob)
    blocked = lambda a: pad_rows(a, HG_BLOCK).reshape(db * HG_BLOCK, -1)
    rec, s_s = _hgrn(blocked(qb), blocked(kb), blocked(ib), blocked(lf), blocked(gb), state_hgrn[0],
                     out_gain, db, HG_BLOCK, HG_BLOCK, 1)
    rec = rec.reshape(db, HG_BLOCK, B_WIDTH)[:, :dec_seq].reshape(n_s, B_WIDTH)
    y_s = _ffn(ys, *ffn2, n_s, mix=(attn.reshape(n_s, A_WIDTH), rec, wo))
    heads_s = lambda a: a.reshape(1, db, dec_seq, A_HEADS, A_HEAD_DIM)

    return (y_p.reshape(batch, seq, D_MODEL), y_s.reshape(db, dec_seq, D_MODEL),
            heads_p(kp_t), heads_p(vp_t), s_p[None],
            heads_s(k_s), heads_s(v_s), s_s[None])
```

```python
import functools
import math

import jax
import jax.numpy as jnp
from jax import lax
from jax.experimental import pallas as pl
from jax.experimental.pallas import tpu as pltpu

F32 = jnp.float32
BF16 = jnp.bfloat16

D_MODEL = 1024
A_HEADS = 8
A_HEAD_DIM = 64
A_WIDTH = A_HEADS * A_HEAD_DIM
B_HEADS = 4
B_DIM = 128
B_WIDTH = B_HEADS * B_DIM
GROUP_W = 512
N_GROUPS = 7
D_FF = 2816
MOBA_BLOCK = 256
MOBA_TOPK = 3
PAGE_SIZE = 128
N_BUCKETS = 32
MAX_EXACT = N_BUCKETS // 2
REL_MAX_DIST = 4096
HG_CHUNK = 16
HG_BLOCK = 128
NORM_EPS = 1e-6
NEG_INF = -1e30
ATTN_SCALE = A_HEAD_DIM ** -0.5
LOG2_E = math.log2(math.e)

FF_CHUNK = 256
VMEM_LIMIT = 48 * 1024 * 1024


def _resident(shape):
    nd = len(shape)
    return pl.BlockSpec(shape, lambda *_: (0,) * nd, pipeline_mode=pl.Buffered(1))


def _params(semantics):
    return pltpu.CompilerParams(dimension_semantics=semantics, vmem_limit_bytes=VMEM_LIMIT)


def _dot(a, b):
    return jnp.dot(a, b, preferred_element_type=F32)


def _dot_nt(a, b):
    return lax.dot_general(a, b, (((1,), (1,)), ((), ())), preferred_element_type=F32)


def _split3(a):
    a1 = a.astype(BF16)
    r1 = a - a1.astype(F32)
    a2 = r1.astype(BF16)
    a3 = (r1 - a2.astype(F32)).astype(BF16)
    return a1, a2, a3


def _div_pow2(x, d):
    return lax.shift_right_logical(x, int(math.log2(d)))


def _rms_rows(x, gain):
    ms = jnp.mean(x * x, axis=-1, keepdims=True)
    return x * lax.rsqrt(ms + NORM_EPS) * gain


def _silu(x):
    return x * jax.nn.sigmoid(x)


def _ffn_kernel(*refs, with_mix):
    if with_mix:
        x_ref, attn_ref, rec_ref, wo_ref, gain_ref, wg_ref, wu_ref, wd_ref, o_ref = refs
        y = (x_ref[...] + _dot(attn_ref[...], wo_ref[:A_WIDTH, :])
             + _dot(rec_ref[...], wo_ref[A_WIDTH:, :]))
    else:
        x_ref, gain_ref, wg_ref, wu_ref, wd_ref, o_ref = refs
        y = x_ref[...]
    h = _rms_rows(y, gain_ref[...]).astype(BF16)
    acc = jnp.zeros(y.shape, F32)
    for j in range(D_FF // FF_CHUNK):
        cols = slice(j * FF_CHUNK, (j + 1) * FF_CHUNK)
        g = _dot(h, wg_ref[:, cols])
        u = _dot(h, wu_ref[:, cols])
        acc = acc + _dot((_silu(g) * u).astype(BF16), wd_ref[cols, :])
    o_ref[...] = y + 0.5 * acc


def _ffn(x, gain, wg, wu, wd, tm, mix=None):
    n = x.shape[0]
    row = lambda w: pl.BlockSpec((tm, w), lambda i: (i, 0))
    args, specs = [x], [row(D_MODEL)]
    if mix is not None:
        attn, rec, wo = mix
        args += [attn, rec, wo]
        specs += [row(A_WIDTH), row(B_WIDTH), _resident(wo.shape)]
    args += [gain, wg, wu, wd]
    specs += [_resident(gain.shape), _resident(wg.shape), _resident(wu.shape), _resident(wd.shape)]
    return pl.pallas_call(
        functools.partial(_ffn_kernel, with_mix=mix is not None),
        out_shape=jax.ShapeDtypeStruct((n, D_MODEL), F32),
        grid=(n // tm,),
        in_specs=specs,
        out_specs=row(D_MODEL),
        compiler_params=_params(("parallel",)),
        name="ffn_mix" if mix is not None else "ffn",
    )(*args)


def _proj_kernel(y_ref, gain_ref, wat_ref, wb_ref, qg_ref, kg_ref, lbl_ref,
                 q_ref, k_ref, v_ref, qb_ref, kb_ref, ib_ref, lf_ref, gb_ref):
    h = _rms_rows(y_ref[...], gain_ref[...]).astype(BF16)
    tm = h.shape[0]

    def group_t(i):
        return _dot_nt(wat_ref[i * GROUP_W:(i + 1) * GROUP_W, :], h)

    def group(i):
        return _dot(h, wb_ref[:, i * GROUP_W:(i + 1) * GROUP_W])

    def head_norm(z, g):
        out = []
        for hd in range(A_HEADS):
            zh = z[hd * A_HEAD_DIM:(hd + 1) * A_HEAD_DIM, :]
            out.append(zh * lax.rsqrt(jnp.mean(zh * zh, axis=0, keepdims=True) + NORM_EPS))
        return jnp.concatenate(out, axis=0) * jnp.concatenate([g] * pl.cdiv(tm, 128), axis=1)[:, :tm]

    q_ref[...] = head_norm(group_t(0), qg_ref[...]) * ATTN_SCALE
    k_ref[...] = head_norm(group_t(1), kg_ref[...])
    v_ref[...] = group_t(2)
    qb_ref[...] = _silu(group(0))
    lbl = lbl_ref[...]
    e = jnp.exp(lbl - jnp.max(lbl, axis=0, keepdims=True))
    lb = e[0:1, :] / jnp.sum(e, axis=0, keepdims=True)
    forget = lb + (1.0 - lb) * jax.nn.sigmoid(group(1))
    lf_ref[...] = jnp.log(forget)
    kb_ref[...] = 1.0 - forget
    ib_ref[...] = group(2)
    gb_ref[...] = group(3)


def _proj(y, gain, wat, wb, qg, kg, lbl, seq, tm):
    n = y.shape[0]
    nt = seq // tm
    row = lambda w: pl.BlockSpec((tm, w), lambda i: (i, 0))
    col = pl.BlockSpec((GROUP_W, tm), lambda i: (i // nt, i % nt))
    out_t = jax.ShapeDtypeStruct((n // seq * GROUP_W, seq), F32)
    out = jax.ShapeDtypeStruct((n, GROUP_W), F32)
    return pl.pallas_call(
        _proj_kernel,
        out_shape=(out_t,) * 3 + (out,) * 5,
        grid=(n // tm,),
        in_specs=[row(D_MODEL)] + [_resident(a.shape) for a in (gain, wat, wb, qg, kg, lbl)],
        out_specs=(col,) * 3 + (row(GROUP_W),) * 5,
        compiler_params=_params(("parallel",)),
        name="proj",
    )(y, gain, wat, wb, qg, kg, lbl)


def _rel_bucket(n):
    nf = jnp.maximum(n, 1).astype(F32)
    large = MAX_EXACT + (jnp.log(nf / MAX_EXACT) / math.log(REL_MAX_DIST / MAX_EXACT)
                         * (N_BUCKETS - MAX_EXACT)).astype(jnp.int32)
    large = jnp.minimum(large, N_BUCKETS - 1)
    return jnp.where(n < MAX_EXACT, n, large)


def _bias_kernel(tab_ref, w_ref, sbt_ref, ob_ref, *, past, dec_seq):
    tab = tab_ref[...]
    tab_rows = jnp.concatenate([tab] * dec_seq, axis=0)

    def lookup(bucket, table):
        out = jnp.zeros(bucket.shape, F32)
        for b in range(N_BUCKETS):
            out = jnp.where(bucket == b, table[:, b:b + 1], out)
        return out

    c = lax.broadcasted_iota(jnp.int32, w_ref.shape, 1)
    w_ref[...] = lookup(_rel_bucket(jnp.maximum(c - MOBA_BLOCK, 0)), tab)
    t = _div_pow2(lax.broadcasted_iota(jnp.int32, sbt_ref.shape, 0), A_HEADS)
    kpos = lax.broadcasted_iota(jnp.int32, sbt_ref.shape, 1)
    sbt_ref[...] = lookup(_rel_bucket(jnp.maximum(past + t - kpos, 0)), tab_rows)
    t = _div_pow2(lax.broadcasted_iota(jnp.int32, ob_ref.shape, 0), A_HEADS)
    s = lax.broadcasted_iota(jnp.int32, ob_ref.shape, 1)
    ob_ref[...] = lookup(_rel_bucket(jnp.maximum(t - s, 0)), tab_rows)


def _bias_tables(tab_t, seq, past, dec_seq):
    rows = dec_seq * A_HEADS
    return pl.pallas_call(
        functools.partial(_bias_kernel, past=past, dec_seq=dec_seq),
        out_shape=(jax.ShapeDtypeStruct((A_HEADS, seq + MOBA_BLOCK), F32),
                   jax.ShapeDtypeStruct((rows, past), F32),
                   jax.ShapeDtypeStruct((rows, 128), F32)),
        name="rel_bias",
    )(tab_t)


def _top_k_mask(gate, k, axis):
    index = lax.broadcasted_iota(jnp.int32, gate.shape, axis).astype(F32)
    sel = jnp.zeros(gate.shape, F32)
    for _ in range(k):
        mx = jnp.max(gate, axis=axis, keepdims=True)
        first = jnp.min(jnp.where(gate == mx, index, float(gate.shape[axis])), axis=axis, keepdims=True)
        chosen = index == first
        sel = jnp.where(chosen & (mx > -jnp.inf), 1.0, sel)
        gate = jnp.where(chosen, -jnp.inf, gate)
    return sel


def _moba_prompt_kernel(q_ref, k_ref, v_ref, w_ref, o_ref, kb_ref, vt_ref, km_ref, tt_ref, s_ref, *, nb):
    b = pl.program_id(1)
    i = pl.program_id(2)
    blk = MOBA_BLOCK

    @pl.when((b == 0) & (i == 0))
    def _build_bias_tiles():
        for hh in range(2):
            for d in range(nb):
                win = jnp.broadcast_to(w_ref[0, hh:hh + 1, d * blk:(d + 2) * blk], (blk, 2 * blk))
                tt_ref[hh, d] = pltpu.roll(win, 0, 1, stride=1, stride_axis=0)[:, blk:]

    @pl.when(i == 0)
    def _prepare_sequence():
        for n in range(nb):
            cols = slice(n * blk, (n + 1) * blk)
            kblk = k_ref[:, cols].T
            kb_ref[n] = kblk.astype(BF16)
            km_ref[n:n + 1, :] = jnp.mean(kblk, axis=0, keepdims=True)
            vt_ref[n] = v_ref[:, cols].astype(BF16)

    q_t = q_ref[...]
    feat = lax.broadcasted_iota(jnp.int32, q_t.shape, 0)
    km = km_ref[...]
    km_lane = lax.broadcasted_iota(jnp.int32, km.shape, 1)
    blk_id = lax.broadcasted_iota(jnp.int32, (nb, blk), 0)
    key_j = lax.broadcasted_iota(jnp.int32, (blk, blk), 0)
    qry_c = lax.broadcasted_iota(jnp.int32, (blk, blk), 1)

    fold = lambda a, op: op(a.reshape(blk // 8, 8, blk), axis=0)

    qp, sel, m8 = [], [], []
    for hh in range(2):
        lo = hh * A_HEAD_DIM
        q_h = jnp.where((feat >= lo) & (feat < lo + A_HEAD_DIM), q_t, 0.0)
        km_h = jnp.where((km_lane >= lo) & (km_lane < lo + A_HEAD_DIM), km, 0.0)
        k1, k2, _ = _split3(km_h)
        q1, q2, _ = _split3(q_h)
        gate = _dot(k1, q1) + _dot(k1, q2) + _dot(k2, q1)
        gate = jnp.where(blk_id < i, gate, -jnp.inf)
        sel.append(_top_k_mask(gate, MOBA_TOPK, 0))
        qp.append(q1)
        s = _dot(kb_ref[i], q1) + tt_ref[hh, 0]
        s = jnp.where(qry_c >= key_j, s, NEG_INF)
        s_ref[hh, 0] = s
        m8.append(fold(s, jnp.max))

    def logits(n, m8):
        out = []
        for hh in range(2):
            picked = jnp.sum(jnp.where(blk_id == n, sel[hh], 0.0), axis=0, keepdims=True)
            s = _dot(kb_ref[n], qp[hh]) + tt_ref[hh, i - n]
            s = jnp.where(picked > 0.0, s, NEG_INF)
            s_ref[hh, n + 1] = s
            out.append(jnp.maximum(m8[hh], fold(s, jnp.max)))
        return tuple(out)

    m8 = lax.fori_loop(0, i, logits, tuple(m8))
    m = [jnp.max(a, axis=0, keepdims=True) for a in m8]

    def weigh(slot, n, hh, l8, acc):
        p = jnp.exp(s_ref[hh, slot] - m[hh])
        lo = hh * A_HEAD_DIM
        return l8 + fold(p, jnp.sum), acc + _dot(vt_ref[n, lo:lo + A_HEAD_DIM, :], p.astype(BF16))

    zero = (jnp.zeros((8, blk), F32), jnp.zeros((A_HEAD_DIM, blk), F32))
    carry = weigh(0, i, 0, *zero) + weigh(0, i, 1, *zero)

    def values(n, carry):
        return weigh(n + 1, n, 0, *carry[:2]) + weigh(n + 1, n, 1, *carry[2:])

    l8_a, acc_a, l8_b, acc_b = lax.fori_loop(0, i, values, carry)
    out_t = jnp.concatenate([acc_a / jnp.sum(l8_a, axis=0, keepdims=True),
                             acc_b / jnp.sum(l8_b, axis=0, keepdims=True)], axis=0)
    o_ref[...] = out_t.T.astype(o_ref.dtype)


def _moba_prompt(q_t, k_t, v_t, w_pairs, batch, seq):
    nb = seq // MOBA_BLOCK
    pairs = A_HEADS // 2
    q_tile = pl.BlockSpec((128, MOBA_BLOCK), lambda p, b, i: (b * pairs + p, i))
    whole = pl.BlockSpec((128, seq), lambda p, b, i: (b * pairs + p, 0))
    return pl.pallas_call(
        functools.partial(_moba_prompt_kernel, nb=nb),
        out_shape=jax.ShapeDtypeStruct((batch * seq, A_WIDTH), BF16),
        grid=(pairs, batch, nb),
        in_specs=[q_tile, whole, whole,
                  pl.BlockSpec((1, 2, seq + MOBA_BLOCK), lambda p, b, i: (p, 0, 0))],
        out_specs=pl.BlockSpec((MOBA_BLOCK, 128), lambda p, b, i: (b * nb + i, p)),
        scratch_shapes=[pltpu.VMEM((nb, MOBA_BLOCK, 128), BF16),
                        pltpu.VMEM((nb, 128, MOBA_BLOCK), BF16),
                        pltpu.VMEM((nb, 128), F32),
                        pltpu.VMEM((2, nb, MOBA_BLOCK, MOBA_BLOCK), F32),
                        pltpu.VMEM((2, nb, MOBA_BLOCK, MOBA_BLOCK), F32)],
        compiler_params=_params(("arbitrary", "arbitrary", "arbitrary")),
        name="moba_prompt",
    )(q_t, k_t, v_t, w_pairs)


def _moba_sample_kernel(pt_ref, q_ref, kn_ref, vn_ref, sbt_ref, ob_ref, *refs, nbk, dec_seq, bps):
    del pt_ref
    n_pages = 2 * bps
    k_refs, v_refs = refs[:n_pages], refs[n_pages:2 * n_pages]
    o_ref, m_sc, l_sc, acc_sc, g_sc, q16_sc = refs[2 * n_pages:]
    j = pl.program_id(1)
    rows = dec_seq * A_HEADS
    head_of_row = lax.broadcasted_iota(jnp.int32, (A_HEADS, A_WIDTH), 0)
    head_of_lane = _div_pow2(lax.broadcasted_iota(jnp.int32, (A_HEADS, A_WIDTH), 1), A_HEAD_DIM)
    head_mask = head_of_row == head_of_lane
    wide = lambda a: jnp.broadcast_to(a, (rows, 128))

    @pl.when(j == 0)
    def _expand_queries():
        for t in range(dec_seq):
            q_t = jnp.where(head_mask, jnp.broadcast_to(q_ref[0, t:t + 1, :], head_mask.shape), 0.0)
            q16_sc[t * A_HEADS:(t + 1) * A_HEADS, :] = q_t.astype(BF16)

    q16 = q16_sc[...]
    for r in range(bps):
        n = j * bps + r
        raw = jnp.concatenate([_dot(q16, k_refs[2 * r + pg][0].astype(BF16)) for pg in range(2)], axis=1)
        s = raw + sbt_ref[:, r * MOBA_BLOCK:(r + 1) * MOBA_BLOCK]
        m = jnp.max(s, axis=1, keepdims=True)
        p = jnp.exp(s - m)
        m_sc[n] = wide(m)
        l_sc[n] = wide(jnp.sum(p, axis=1, keepdims=True))
        p = p.astype(BF16)
        acc_sc[n] = (_dot_nt(p[:, :PAGE_SIZE], v_refs[2 * r][0].astype(BF16))
                     + _dot_nt(p[:, PAGE_SIZE:], v_refs[2 * r + 1][0].astype(BF16)))
        g_sc[n] = wide(jnp.sum(raw, axis=1, keepdims=True) * (1.0 / MOBA_BLOCK))

    @pl.when(j == pl.num_programs(1) - 1)
    def _merge():
        gates = [g_sc[n] for n in range(nbk)]
        picked = [jnp.zeros((rows, 128), jnp.bool_)] * nbk
        for _ in range(min(MOBA_TOPK, nbk)):
            mx = functools.reduce(jnp.maximum, gates)
            first = functools.reduce(
                jnp.minimum, [jnp.where(g == mx, float(n), float(nbk)) for n, g in enumerate(gates)])
            for n in range(nbk):
                chosen = first == float(n)
                picked[n] = picked[n] | (chosen & (mx > -jnp.inf))
                gates[n] = jnp.where(chosen, -jnp.inf, gates[n])
        zeros = jnp.zeros((128 - kn_ref.shape[1], A_WIDTH), F32)
        k_own = jnp.concatenate([kn_ref[0], zeros], axis=0).astype(BF16)
        v_own = jnp.concatenate([vn_ref[0], zeros], axis=0).astype(BF16)
        s = _dot_nt(q16, k_own) + ob_ref[...]
        t_of_row = _div_pow2(lax.broadcasted_iota(jnp.int32, s.shape, 0), A_HEADS)
        s_idx = lax.broadcasted_iota(jnp.int32, s.shape, 1)
        s = jnp.where(s_idx <= t_of_row, s, NEG_INF)
        m_own = wide(jnp.max(s, axis=1, keepdims=True))
        p = jnp.exp(s - m_own)
        l_own = wide(jnp.sum(p, axis=1, keepdims=True))
        acc_own = _dot(p.astype(BF16), v_own)
        m_all = m_own
        for n in range(nbk):
            m_all = jnp.maximum(m_all, jnp.where(picked[n], m_sc[n], -jnp.inf))
        lanes4 = lambda a: jnp.concatenate([a] * (A_WIDTH // 128), axis=1)
        w = jnp.exp(m_own - m_all)
        l_all = w * l_own
        acc = lanes4(w) * acc_own
        for n in range(nbk):
            w = jnp.where(picked[n], jnp.exp(m_sc[n] - m_all), 0.0)
            l_all = l_all + w * l_sc[n]
            acc = acc + lanes4(w) * acc_sc[n]
        res = acc / lanes4(l_all)
        out = [jnp.sum(jnp.where(head_mask, res[t * A_HEADS:(t + 1) * A_HEADS, :], 0.0),
                       axis=0, keepdims=True) for t in range(dec_seq)]
        o_ref[0] = jnp.concatenate(out, axis=0).astype(o_ref.dtype)


def _moba_sample(q, k_new, v_new, cache_k, cache_v, page_table, sbt, ob, bps=2):
    db, dec_seq, _ = q.shape
    n_pages = page_table.shape[1]
    nbk = n_pages * PAGE_SIZE // MOBA_BLOCK
    rows = dec_seq * A_HEADS
    ppb = MOBA_BLOCK // PAGE_SIZE
    assert ppb == 2 and nbk % bps == 0
    per_seq = lambda b, j, pt: (b, 0, 0)
    page = lambda r: pl.BlockSpec(
        (1, A_WIDTH, PAGE_SIZE), lambda b, j, pt: (pt[b * n_pages + j * (ppb * bps) + r], 0, 0))
    pages = [page(r) for r in range(ppb * bps)]
    grid_spec = pltpu.PrefetchScalarGridSpec(
        num_scalar_prefetch=1,
        grid=(db, nbk // bps),
        in_specs=[pl.BlockSpec((1, dec_seq, A_WIDTH), per_seq),
                  pl.BlockSpec((1, k_new.shape[1], A_WIDTH), per_seq),
                  pl.BlockSpec((1, v_new.shape[1], A_WIDTH), per_seq),
                  pl.BlockSpec((rows, bps * MOBA_BLOCK), lambda b, j, pt: (0, j)),
                  pl.BlockSpec((rows, 128), lambda b, j, pt: (0, 0))] + pages + pages,
        out_specs=pl.BlockSpec((1, dec_seq, A_WIDTH), per_seq),
        scratch_shapes=[pltpu.VMEM((nbk, rows, 128), F32),
                        pltpu.VMEM((nbk, rows, 128), F32),
                        pltpu.VMEM((nbk, rows, A_WIDTH), F32),
                        pltpu.VMEM((nbk, rows, 128), F32),
                        pltpu.VMEM((rows, A_WIDTH), BF16)])
    return pl.pallas_call(
        functools.partial(_moba_sample_kernel, nbk=nbk, dec_seq=dec_seq, bps=bps),
        out_shape=jax.ShapeDtypeStruct((db, dec_seq, A_WIDTH), BF16),
        grid_spec=grid_spec,
        compiler_params=_params(("arbitrary", "arbitrary")),
        name="moba_sample",
    )(page_table.reshape(-1), q, k_new, v_new, sbt, ob,
      *([cache_k] * (ppb * bps)), *([cache_v] * (ppb * bps)))


def _hgrn_kernel(q_ref, k_ref, v_ref, g_ref, gate_ref, s0_ref, og_ref, rec_ref, sfin_ref,
                 st_ref, *, n_chunks):
    tb = pl.program_id(2)
    n_blocks = q_ref.shape[0] // HG_BLOCK
    used = n_chunks * HG_CHUNK
    row = lax.broadcasted_iota(jnp.int32, (HG_BLOCK, HG_BLOCK), 0)
    col = lax.broadcasted_iota(jnp.int32, (HG_BLOCK, HG_BLOCK), 1)
    tril = jnp.where(row >= col, 1.0, 0.0).astype(BF16)
    ones = jnp.ones((HG_BLOCK, HG_BLOCK), BF16)
    chunk_row = lax.broadcasted_iota(jnp.int32, (n_chunks, HG_CHUNK, B_DIM), 1)
    by_chunk = lambda a: a[:used].reshape(n_chunks, HG_CHUNK, B_DIM)
    pick_s = _div_pow2(lax.broadcasted_iota(jnp.int32, (HG_CHUNK * B_DIM, HG_BLOCK), 0), B_DIM)
    pick_j = lax.broadcasted_iota(jnp.int32, (HG_CHUNK * B_DIM, HG_BLOCK), 1) & (HG_CHUNK - 1)
    pick = jnp.where(pick_s == pick_j, 1.0, 0.0).astype(BF16)
    same_chunk = (_div_pow2(row, HG_CHUNK) == _div_pow2(col, HG_CHUNK))[:used]

    @pl.when(tb == 0)
    def _load_state():
        st_ref[...] = s0_ref[0, 0].T

    st = st_ref[...]
    for bi in range(n_blocks):
        rows = slice(bi * HG_BLOCK, (bi + 1) * HG_BLOCK)
        q, k, v = q_ref[rows, :], k_ref[rows, :], v_ref[rows, :]
        g1, g2, g3 = _split3(g_ref[rows, :] * LOG2_E)
        cum = _dot(tril, g1) + _dot(tril, g2) + _dot(tril, g3)
        last = cum[HG_BLOCK - 1:HG_BLOCK, :]
        v16 = v.astype(BF16)
        o = _dot_nt((q * jnp.exp2(cum)).astype(BF16), st.astype(BF16))
        st = st * jnp.exp2(last) + _dot(v.T.astype(BF16), (k * jnp.exp2(last - cum)).astype(BF16))
        parts = [jnp.zeros((HG_CHUNK, B_DIM), F32)]
        for c in range(1, n_chunks):
            lo = c * HG_CHUNK
            r = cum[lo - 1:lo, :]
            k_c = (k[:lo] * jnp.exp2(r - cum[:lo])).astype(BF16)
            k_c = jnp.concatenate([k_c, jnp.zeros((HG_BLOCK - lo, B_DIM), BF16)], axis=0)
            q_c = (q[lo:lo + HG_CHUNK] * jnp.exp2(cum[lo:lo + HG_CHUNK] - r)).astype(BF16)
            parts.append(_dot_nt(q_c, k_c))
        cum3, q3, k3 = by_chunk(cum), by_chunk(q), by_chunk(k)
        terms = []
        for s in range(HG_CHUNK):
            decay = jnp.exp2(jnp.where(chunk_row >= s, cum3 - cum3[:, s:s + 1, :], -jnp.inf))
            terms.append((q3 * decay * k3[:, s:s + 1, :]).reshape(used, B_DIM).astype(BF16))
        inside = _dot(jnp.concatenate(terms, axis=1), pick)
        score = jnp.concatenate(parts, axis=0) + jnp.where(same_chunk, inside, 0.0)
        intra = _dot(score.astype(BF16), v16)
        if used < HG_BLOCK:
            intra = jnp.concatenate([intra, jnp.zeros((HG_BLOCK - used, B_DIM), F32)], axis=0)
        o = o + intra
        ms = _dot((o * o).astype(BF16), ones) * (1.0 / B_DIM)
        gate = gate_ref[rows, :]
        rec_ref[rows, :] = ((o * lax.rsqrt(ms + NORM_EPS) * og_ref[...]) * _silu(gate)).astype(rec_ref.dtype)
    st_ref[...] = st

    @pl.when(tb == pl.num_programs(2) - 1)
    def _store_state():
        sfin_ref[0, 0] = st.T


def _hgrn(qb, kb, ib, lf, gb, s0, out_gain, batch, seq, tile, n_chunks):
    nt = seq // tile
    tok = pl.BlockSpec((tile, B_DIM), lambda b, h, t: (b * nt + t, h))
    state = pl.BlockSpec((1, 1, B_DIM, B_DIM), lambda b, h, t: (b, h, 0, 0))
    return pl.pallas_call(
        functools.partial(_hgrn_kernel, n_chunks=n_chunks),
        out_shape=(jax.ShapeDtypeStruct((batch * seq, B_WIDTH), BF16),
                   jax.ShapeDtypeStruct(s0.shape, F32)),
        grid=(batch, B_HEADS, nt),
        in_specs=[tok, tok, tok, tok, tok, state, pl.BlockSpec((1, B_DIM), lambda b, h, t: (0, h))],
        out_specs=(tok, state),
        scratch_shapes=[pltpu.VMEM((B_DIM, B_DIM), F32)],
        compiler_params=_params(("arbitrary", "arbitrary", "arbitrary")),
        name="hgrn",
    )(qb, kb, ib, lf, gb, s0, out_gain)


def kernel(x_prompt, x_sample, cache_k, cache_v, state_hgrn, page_table, rel_bias_table, lb_logits,
           ffn1_norm, ffn1_gate, ffn1_up, ffn1_down, mix_norm, w_in, q_norm, k_norm, hgrn_out_norm,
           w_out, ffn2_norm, ffn2_gate, ffn2_up, ffn2_down):
    batch, seq, _ = x_prompt.shape
    db, dec_seq, _ = x_sample.shape
    depth, n_phys = cache_k.shape[:2]
    assert depth == 1 and lb_logits.shape[0] == 2
    past = page_table.shape[1] * PAGE_SIZE
    assert seq % MOBA_BLOCK == 0 and past % MOBA_BLOCK == 0 and dec_seq <= HG_CHUNK
    n_p, n_s = batch * seq, db * dec_seq

    bf = lambda w: w[0].astype(BF16)
    gain = lambda g: g[0].reshape(1, -1)
    ffn1 = (gain(ffn1_norm), bf(ffn1_gate), bf(ffn1_up), bf(ffn1_down))
    ffn2 = (gain(ffn2_norm), bf(ffn2_gate), bf(ffn2_up), bf(ffn2_down))
    wo = bf(w_out)
    w_in16 = bf(w_in)
    per_lane = lambda g: jnp.broadcast_to(jnp.tile(g[0], A_HEADS)[:, None], (A_WIDTH, 128))
    proj_w = (gain(mix_norm), w_in16[:, :3 * GROUP_W].T, w_in16[:, 3 * GROUP_W:],
              per_lane(q_norm), per_lane(k_norm), lb_logits)
    out_gain = hgrn_out_norm[0].reshape(1, -1)
    w_bias, sbt, ob = _bias_tables(rel_bias_table.T, seq, past, dec_seq)

    tm = 512
    y = _ffn(x_prompt.reshape(n_p, D_MODEL), *ffn1, tm)
    q_t, kp_t, vp_t, qb, kb, ib, lf, gb = _proj(y, *proj_w, seq, tm)
    attn = _moba_prompt(q_t, kp_t, vp_t, w_bias.reshape(A_HEADS // 2, 2, -1), batch, seq)
    s0 = jnp.zeros((batch, B_HEADS, B_DIM, B_DIM), F32)
    rec, s_p = _hgrn(qb, kb, ib, lf, gb, s0, out_gain, batch, seq, 512, HG_BLOCK // HG_CHUNK)
    y_p = _ffn(y, *ffn2, tm, mix=(attn, rec, wo))
    heads_p = lambda a: a.reshape(batch, A_HEADS, A_HEAD_DIM, seq).transpose(0, 3, 1, 2)[None]

    ys = _ffn(x_sample.reshape(n_s, D_MODEL), *ffn1, n_s)
    q_t, k_t, v_t, qb, kb, ib, lf, gb = _proj(ys, *proj_w, n_s, n_s)
    q_s, k_s, v_s = (a.T.reshape(db, dec_seq, A_WIDTH) for a in (q_t, k_t, v_t))
    pad_rows = lambda a, r: jnp.pad(a.reshape(db, dec_seq, -1), ((0, 0), (0, r - dec_seq), (0, 0)))
    pages = lambda c: c[0].transpose(0, 2, 3, 1).reshape(n_phys, A_WIDTH, PAGE_SIZE)
    attn = _moba_sample(q_s, pad_rows(k_s, 8), pad_rows(v_s, 8), pages(cache_k), pages(cache_v),
                        page_table, sbt, ob)
    blocked = lambda a: pad_rows(a, HG_BLOCK).reshape(db * HG_BLOCK, -1)
    rec, s_s = _hgrn(blocked(qb), blocked(kb), blocked(ib), blocked(lf), blocked(gb), state_hgrn[0],
                     out_gain, db, HG_BLOCK, HG_BLOCK, 1)
    rec = rec.reshape(db, HG_BLOCK, B_WIDTH)[:, :dec_seq].reshape(n_s, B_WIDTH)
    y_s = _ffn(ys, *ffn2, n_s, mix=(attn.reshape(n_s, A_WIDTH), rec, wo))
    heads_s = lambda a: a.reshape(1, db, dec_seq, A_HEADS, A_HEAD_DIM)

    return (y_p.reshape(batch, seq, D_MODEL), y_s.reshape(db, dec_seq, D_MODEL),
            heads_p(kp_t), heads_p(vp_t), s_p[None],
            heads_s(k_s), heads_s(v_s), s_s[None])
```

```python
import functools
import math

import jax
import jax.numpy as jnp
from jax import lax
from jax.experimental import pallas as pl
from jax.experimental.pallas import tpu as pltpu

F32 = jnp.float32
BF16 = jnp.bfloat16

D_MODEL = 1024
A_HEADS = 8
A_HEAD_DIM = 64
A_WIDTH = A_HEADS * A_HEAD_DIM
B_HEADS = 4
B_DIM = 128
B_WIDTH = B_HEADS * B_DIM
GROUP_W = 512
N_GROUPS = 7
D_FF = 2816
MOBA_BLOCK = 256
MOBA_TOPK = 3
PAGE_SIZE = 128
N_BUCKETS = 32
MAX_EXACT = N_BUCKETS // 2
REL_MAX_DIST = 4096
HG_CHUNK = 16
HG_BLOCK = 128
NORM_EPS = 1e-6
NEG_INF = -1e30
ATTN_SCALE = A_HEAD_DIM ** -0.5
LOG2_E = math.log2(math.e)

FF_CHUNK = 256
SAMPLE_BLOCKS_PER_STEP = 8
VMEM_LIMIT = 48 * 1024 * 1024


def _resident(shape):
    nd = len(shape)
    return pl.BlockSpec(shape, lambda *_: (0,) * nd, pipeline_mode=pl.Buffered(1))


def _params(semantics):
    return pltpu.CompilerParams(dimension_semantics=semantics, vmem_limit_bytes=VMEM_LIMIT)


def _dot(a, b):
    return jnp.dot(a, b, preferred_element_type=F32)


def _dot_nt(a, b):
    return lax.dot_general(a, b, (((1,), (1,)), ((), ())), preferred_element_type=F32)


def _split3(a):
    a1 = a.astype(BF16)
    r1 = a - a1.astype(F32)
    a2 = r1.astype(BF16)
    a3 = (r1 - a2.astype(F32)).astype(BF16)
    return a1, a2, a3


def _div_pow2(x, d):
    return lax.shift_right_logical(x, int(math.log2(d)))


def _rms_rows(x, gain):
    ms = jnp.mean(x * x, axis=-1, keepdims=True)
    return x * lax.rsqrt(ms + NORM_EPS) * gain


def _silu(x):
    return x * jax.nn.sigmoid(x)


def _ffn_kernel(*refs, with_mix):
    if with_mix:
        x_ref, attn_ref, rec_ref, wo_ref, gain_ref, wg_ref, wu_ref, wd_ref, o_ref = refs
        y = (x_ref[...] + _dot(attn_ref[...], wo_ref[:A_WIDTH, :])
             + _dot(rec_ref[...], wo_ref[A_WIDTH:, :]))
    else:
        x_ref, gain_ref, wg_ref, wu_ref, wd_ref, o_ref = refs
        y = x_ref[...]
    h = _rms_rows(y, gain_ref[...]).astype(BF16)
    acc = jnp.zeros(y.shape, F32)
    for j in range(D_FF // FF_CHUNK):
        cols = slice(j * FF_CHUNK, (j + 1) * FF_CHUNK)
        g = _dot(h, wg_ref[:, cols])
        u = _dot(h, wu_ref[:, cols])
        acc = acc + _dot((_silu(g) * u).astype(BF16), wd_ref[cols, :])
    o_ref[...] = y + 0.5 * acc


def _ffn(x, gain, wg, wu, wd, tm, mix=None):
    n = x.shape[0]
    row = lambda w: pl.BlockSpec((tm, w), lambda i: (i, 0))
    args, specs = [x], [row(D_MODEL)]
    if mix is not None:
        attn, rec, wo = mix
        args += [attn, rec, wo]
        specs += [row(A_WIDTH), row(B_WIDTH), _resident(wo.shape)]
    args += [gain, wg, wu, wd]
    specs += [_resident(gain.shape), _resident(wg.shape), _resident(wu.shape), _resident(wd.shape)]
    return pl.pallas_call(
        functools.partial(_ffn_kernel, with_mix=mix is not None),
        out_shape=jax.ShapeDtypeStruct((n, D_MODEL), F32),
        grid=(n // tm,),
        in_specs=specs,
        out_specs=row(D_MODEL),
        compiler_params=_params(("parallel",)),
        name="ffn_mix" if mix is not None else "ffn",
    )(*args)


def _proj_kernel(y_ref, gain_ref, wat_ref, wb_ref, qg_ref, kg_ref, lbl_ref,
                 q_ref, k_ref, v_ref, qb_ref, kb_ref, ib_ref, lf_ref, gb_ref):
    h = _rms_rows(y_ref[...], gain_ref[...]).astype(BF16)
    tm = h.shape[0]

    def group_t(i):
        return _dot_nt(wat_ref[i * GROUP_W:(i + 1) * GROUP_W, :], h)

    def group(i):
        return _dot(h, wb_ref[:, i * GROUP_W:(i + 1) * GROUP_W])

    def head_norm(z, g):
        out = []
        for hd in range(A_HEADS):
            zh = z[hd * A_HEAD_DIM:(hd + 1) * A_HEAD_DIM, :]
            out.append(zh * lax.rsqrt(jnp.mean(zh * zh, axis=0, keepdims=True) + NORM_EPS))
        return jnp.concatenate(out, axis=0) * jnp.concatenate([g] * pl.cdiv(tm, 128), axis=1)[:, :tm]

    q_ref[...] = head_norm(group_t(0), qg_ref[...]) * ATTN_SCALE
    k_ref[...] = head_norm(group_t(1), kg_ref[...])
    v_ref[...] = group_t(2)
    qb_ref[...] = _silu(group(0))
    lbl = lbl_ref[...]
    e = jnp.exp(lbl - jnp.max(lbl, axis=0, keepdims=True))
    lb = e[0:1, :] / jnp.sum(e, axis=0, keepdims=True)
    forget = lb + (1.0 - lb) * jax.nn.sigmoid(group(1))
    lf_ref[...] = jnp.log(forget)
    kb_ref[...] = 1.0 - forget
    ib_ref[...] = group(2)
    gb_ref[...] = group(3)


def _proj(y, gain, wat, wb, qg, kg, lbl, seq, tm):
    n = y.shape[0]
    nt = seq // tm
    row = lambda w: pl.BlockSpec((tm, w), lambda i: (i, 0))
    col = pl.BlockSpec((GROUP_W, tm), lambda i: (i // nt, i % nt))
    out_t = jax.ShapeDtypeStruct((n // seq * GROUP_W, seq), F32)
    out = jax.ShapeDtypeStruct((n, GROUP_W), F32)
    return pl.pallas_call(
        _proj_kernel,
        out_shape=(out_t,) * 3 + (out,) * 5,
        grid=(n // tm,),
        in_specs=[row(D_MODEL)] + [_resident(a.shape) for a in (gain, wat, wb, qg, kg, lbl)],
        out_specs=(col,) * 3 + (row(GROUP_W),) * 5,
        compiler_params=_params(("parallel",)),
        name="proj",
    )(y, gain, wat, wb, qg, kg, lbl)


def _rel_bucket(n):
    nf = jnp.maximum(n, 1).astype(F32)
    large = MAX_EXACT + (jnp.log(nf / MAX_EXACT) / math.log(REL_MAX_DIST / MAX_EXACT)
                         * (N_BUCKETS - MAX_EXACT)).astype(jnp.int32)
    large = jnp.minimum(large, N_BUCKETS - 1)
    return jnp.where(n < MAX_EXACT, n, large)


def _bias_kernel(tab_ref, w_ref, sbt_ref, ob_ref, *, past, dec_seq):
    tab = tab_ref[...]
    tab_rows = jnp.concatenate([tab] * dec_seq, axis=0)

    def lookup(bucket, table):
        out = jnp.zeros(bucket.shape, F32)
        for b in range(N_BUCKETS):
            out = jnp.where(bucket == b, table[:, b:b + 1], out)
        return out

    c = lax.broadcasted_iota(jnp.int32, w_ref.shape, 1)
    w_ref[...] = lookup(_rel_bucket(jnp.maximum(c - MOBA_BLOCK, 0)), tab)
    t = _div_pow2(lax.broadcasted_iota(jnp.int32, sbt_ref.shape, 0), A_HEADS)
    kpos = lax.broadcasted_iota(jnp.int32, sbt_ref.shape, 1)
    sbt_ref[...] = lookup(_rel_bucket(jnp.maximum(past + t - kpos, 0)), tab_rows)
    t = _div_pow2(lax.broadcasted_iota(jnp.int32, ob_ref.shape, 0), A_HEADS)
    s = lax.broadcasted_iota(jnp.int32, ob_ref.shape, 1)
    ob_ref[...] = lookup(_rel_bucket(jnp.maximum(t - s, 0)), tab_rows)


def _bias_tables(tab_t, seq, past, dec_seq):
    rows = dec_seq * A_HEADS
    return pl.pallas_call(
        functools.partial(_bias_kernel, past=past, dec_seq=dec_seq),
        out_shape=(jax.ShapeDtypeStruct((A_HEADS, seq + MOBA_BLOCK), F32),
                   jax.ShapeDtypeStruct((rows, past), F32),
                   jax.ShapeDtypeStruct((rows, 128), F32)),
        name="rel_bias",
    )(tab_t)


def _top_k_mask(gate, k, axis):
    index = lax.broadcasted_iota(jnp.int32, gate.shape, axis).astype(F32)
    sel = jnp.zeros(gate.shape, F32)
    for _ in range(k):
        mx = jnp.max(gate, axis=axis, keepdims=True)
        first = jnp.min(jnp.where(gate == mx, index, float(gate.shape[axis])), axis=axis, keepdims=True)
        chosen = index == first
        sel = jnp.where(chosen & (mx > -jnp.inf), 1.0, sel)
        gate = jnp.where(chosen, -jnp.inf, gate)
    return sel


def _moba_prompt_kernel(q_ref, k_ref, v_ref, w_ref, o_ref, kb_ref, vt_ref, km_ref, tt_ref, s_ref, *, nb):
    b = pl.program_id(1)
    i = pl.program_id(2)
    blk = MOBA_BLOCK

    key_j = lax.broadcasted_iota(jnp.int32, (blk, blk), 0)
    qry_c = lax.broadcasted_iota(jnp.int32, (blk, blk), 1)

    @pl.when((b == 0) & (i == 0))
    def _build_bias_tiles():
        for hh in range(2):
            for d in range(nb):
                win = jnp.broadcast_to(w_ref[0, hh:hh + 1, d * blk:(d + 2) * blk], (blk, 2 * blk))
                tile = pltpu.roll(win, 0, 1, stride=1, stride_axis=0)[:, blk:] * LOG2_E
                tt_ref[hh, d] = jnp.where(qry_c >= key_j, tile, NEG_INF) if d == 0 else tile

    @pl.when(i == 0)
    def _prepare_sequence():
        for n in range(nb):
            cols = slice(n * blk, (n + 1) * blk)
            kblk = k_ref[:, cols].T
            kb_ref[n] = kblk.astype(BF16)
            km_ref[n:n + 1, :] = jnp.mean(kblk, axis=0, keepdims=True)
            vt_ref[n] = v_ref[:, cols].astype(BF16)

    q_t = q_ref[...] * LOG2_E
    feat = lax.broadcasted_iota(jnp.int32, q_t.shape, 0)
    km = km_ref[...]
    km_lane = lax.broadcasted_iota(jnp.int32, km.shape, 1)
    blk_id = lax.broadcasted_iota(jnp.int32, (nb, blk), 0)

    fold = lambda a, op: op(a.reshape(blk // 8, 8, blk), axis=0)

    qp, sel = [], []
    for hh in range(2):
        lo = hh * A_HEAD_DIM
        q_h = jnp.where((feat >= lo) & (feat < lo + A_HEAD_DIM), q_t, 0.0)
        km_h = jnp.where((km_lane >= lo) & (km_lane < lo + A_HEAD_DIM), km, 0.0)
        k1, k2, _ = _split3(km_h)
        q1, q2, _ = _split3(q_h)
        gate = _dot(k1, q1) + _dot(k1, q2) + _dot(k2, q1)
        gate = jnp.where(blk_id < i, gate, -jnp.inf)
        sel.append(jnp.where(blk_id == i, 1.0, _top_k_mask(gate, MOBA_TOPK, 0)))
        qp.append(q1)

    trips = lax.shift_right_logical(i + 2, 1)

    def logits(j, m8):
        out = list(m8)
        for n in (2 * j, 2 * j + 1):
            for hh in range(2):
                picked = jnp.sum(jnp.where(blk_id == n, sel[hh], 0.0), axis=0, keepdims=True)
                s = _dot(kb_ref[n], qp[hh]) + tt_ref[hh, jnp.maximum(i - n, 0)]
                s = jnp.where(picked > 0.0, s, NEG_INF)
                s_ref[hh, n] = s
                out[hh] = jnp.maximum(out[hh], fold(s, jnp.max))
        return tuple(out)

    m8 = lax.fori_loop(0, trips, logits, (jnp.full((8, blk), -jnp.inf, F32),) * 2)
    m = [jnp.max(a, axis=0, keepdims=True) for a in m8]

    def values(j, carry):
        out = list(carry)
        for n in (2 * j, 2 * j + 1):
            for hh in range(2):
                lo = hh * A_HEAD_DIM
                p = jnp.exp2(s_ref[hh, n] - m[hh])
                out[2 * hh] = out[2 * hh] + fold(p, jnp.sum)
                out[2 * hh + 1] = out[2 * hh + 1] + _dot(vt_ref[n, lo:lo + A_HEAD_DIM, :], p.astype(BF16))
        return tuple(out)

    zero = (jnp.zeros((8, blk), F32), jnp.zeros((A_HEAD_DIM, blk), F32))
    l8_a, acc_a, l8_b, acc_b = lax.fori_loop(0, trips, values, zero + zero)
    out_t = jnp.concatenate([acc_a / jnp.sum(l8_a, axis=0, keepdims=True),
                             acc_b / jnp.sum(l8_b, axis=0, keepdims=True)], axis=0)
    o_ref[...] = out_t.T.astype(o_ref.dtype)


def _moba_prompt(q_t, k_t, v_t, w_pairs, batch, seq):
    nb = seq // MOBA_BLOCK
    pairs = A_HEADS // 2
    q_tile = pl.BlockSpec((128, MOBA_BLOCK), lambda p, b, i: (b * pairs + p, i))
    whole = pl.BlockSpec((128, seq), lambda p, b, i: (b * pairs + p, 0))
    return pl.pallas_call(
        functools.partial(_moba_prompt_kernel, nb=nb),
        out_shape=jax.ShapeDtypeStruct((batch * seq, A_WIDTH), BF16),
        grid=(pairs, batch, nb),
        in_specs=[q_tile, whole, whole,
                  pl.BlockSpec((1, 2, seq + MOBA_BLOCK), lambda p, b, i: (p, 0, 0))],
        out_specs=pl.BlockSpec((MOBA_BLOCK, 128), lambda p, b, i: (b * nb + i, p)),
        scratch_shapes=[pltpu.VMEM((nb, MOBA_BLOCK, 128), BF16),
                        pltpu.VMEM((nb, 128, MOBA_BLOCK), BF16),
                        pltpu.VMEM((nb, 128), F32),
                        pltpu.VMEM((2, nb, MOBA_BLOCK, MOBA_BLOCK), F32),
                        pltpu.VMEM((2, nb, MOBA_BLOCK, MOBA_BLOCK), F32)],
        compiler_params=_params(("arbitrary", "arbitrary", "arbitrary")),
        name="moba_prompt",
    )(q_t, k_t, v_t, w_pairs)


def _moba_sample_kernel(pt_ref, q_ref, kn_ref, vn_ref, sbt_ref, ob_ref, *refs, nbk, dec_seq, bps):
    del pt_ref
    n_pages = 2 * bps
    k_refs, v_refs = refs[:n_pages], refs[n_pages:2 * n_pages]
    o_ref, m_sc, l_sc, acc_sc, g_sc, q16_sc = refs[2 * n_pages:]
    j = pl.program_id(1)
    rows = dec_seq * A_HEADS
    head_of_row = lax.broadcasted_iota(jnp.int32, (A_HEADS, A_WIDTH), 0)
    head_of_lane = _div_pow2(lax.broadcasted_iota(jnp.int32, (A_HEADS, A_WIDTH), 1), A_HEAD_DIM)
    head_mask = head_of_row == head_of_lane
    wide = lambda a: jnp.broadcast_to(a, (rows, 128))

    @pl.when(j == 0)
    def _expand_queries():
        for t in range(dec_seq):
            q_t = jnp.where(head_mask, jnp.broadcast_to(q_ref[0, t:t + 1, :], head_mask.shape), 0.0)
            q16_sc[t * A_HEADS:(t + 1) * A_HEADS, :] = q_t.astype(BF16)

    q16 = q16_sc[...]
    for r in range(bps):
        n = j * bps + r
        kblk = jnp.concatenate([k_refs[2 * r][0], k_refs[2 * r + 1][0]], axis=1).astype(BF16)
        vblk = jnp.concatenate([v_refs[2 * r][0], v_refs[2 * r + 1][0]], axis=1).astype(BF16)
        raw = _dot(q16, kblk)
        s = raw + sbt_ref[:, r * MOBA_BLOCK:(r + 1) * MOBA_BLOCK]
        m = jnp.max(s, axis=1, keepdims=True)
        p = jnp.exp(s - m)
        m_sc[n] = wide(m)
        l_sc[n] = wide(jnp.sum(p, axis=1, keepdims=True))
        acc_sc[n] = _dot_nt(p.astype(BF16), vblk)
        g_sc[n] = wide(jnp.sum(raw, axis=1, keepdims=True) * (1.0 / MOBA_BLOCK))

    @pl.when(j == pl.num_programs(1) - 1)
    def _merge():
        gates = [g_sc[n] for n in range(nbk)]
        picked = [jnp.zeros((rows, 128), jnp.bool_)] * nbk
        for _ in range(min(MOBA_TOPK, nbk)):
            mx = functools.reduce(jnp.maximum, gates)
            first = functools.reduce(
                jnp.minimum, [jnp.where(g == mx, float(n), float(nbk)) for n, g in enumerate(gates)])
            for n in range(nbk):
                chosen = first == float(n)
                picked[n] = picked[n] | (chosen & (mx > -jnp.inf))
                gates[n] = jnp.where(chosen, -jnp.inf, gates[n])
        zeros = jnp.zeros((128 - kn_ref.shape[1], A_WIDTH), F32)
        k_own = jnp.concatenate([kn_ref[0], zeros], axis=0).astype(BF16)
        v_own = jnp.concatenate([vn_ref[0], zeros], axis=0).astype(BF16)
        s = _dot_nt(q16, k_own) + ob_ref[...]
        t_of_row = _div_pow2(lax.broadcasted_iota(jnp.int32, s.shape, 0), A_HEADS)
        s_idx = lax.broadcasted_iota(jnp.int32, s.shape, 1)
        s = jnp.where(s_idx <= t_of_row, s, NEG_INF)
        m_own = wide(jnp.max(s, axis=1, keepdims=True))
        p = jnp.exp(s - m_own)
        l_own = wide(jnp.sum(p, axis=1, keepdims=True))
        acc_own = _dot(p.astype(BF16), v_own)
        m_all = m_own
        for n in range(nbk):
            m_all = jnp.maximum(m_all, jnp.where(picked[n], m_sc[n], -jnp.inf))
        lanes4 = lambda a: jnp.concatenate([a] * (A_WIDTH // 128), axis=1)
        w = jnp.exp(m_own - m_all)
        l_all = w * l_own
        acc = lanes4(w) * acc_own
        for n in range(nbk):
            w = jnp.where(picked[n], jnp.exp(m_sc[n] - m_all), 0.0)
            l_all = l_all + w * l_sc[n]
            acc = acc + lanes4(w) * acc_sc[n]
        res = acc / lanes4(l_all)
        out = [jnp.sum(jnp.where(head_mask, res[t * A_HEADS:(t + 1) * A_HEADS, :], 0.0),
                       axis=0, keepdims=True) for t in range(dec_seq)]
        o_ref[0] = jnp.concatenate(out, axis=0).astype(o_ref.dtype)


def _moba_sample(q, k_new, v_new, cache_k, cache_v, page_table, sbt, ob, bps):
    db, dec_seq, _ = q.shape
    n_pages = page_table.shape[1]
    nbk = n_pages * PAGE_SIZE // MOBA_BLOCK
    rows = dec_seq * A_HEADS
    ppb = MOBA_BLOCK // PAGE_SIZE
    assert ppb == 2 and nbk % bps == 0
    per_seq = lambda b, j, pt: (b, 0, 0)
    page = lambda r: pl.BlockSpec(
        (1, A_WIDTH, PAGE_SIZE), lambda b, j, pt: (pt[b * n_pages + j * (ppb * bps) + r], 0, 0))
    pages = [page(r) for r in range(ppb * bps)]
    grid_spec = pltpu.PrefetchScalarGridSpec(
        num_scalar_prefetch=1,
        grid=(db, nbk // bps),
        in_specs=[pl.BlockSpec((1, dec_seq, A_WIDTH), per_seq),
                  pl.BlockSpec((1, k_new.shape[1], A_WIDTH), per_seq),
                  pl.BlockSpec((1, v_new.shape[1], A_WIDTH), per_seq),
                  pl.BlockSpec((rows, bps * MOBA_BLOCK), lambda b, j, pt: (0, j)),
                  pl.BlockSpec((rows, 128), lambda b, j, pt: (0, 0))] + pages + pages,
        out_specs=pl.BlockSpec((1, dec_seq, A_WIDTH), per_seq),
        scratch_shapes=[pltpu.VMEM((nbk, rows, 128), F32),
                        pltpu.VMEM((nbk, rows, 128), F32),
                        pltpu.VMEM((nbk, rows, A_WIDTH), F32),
                        pltpu.VMEM((nbk, rows, 128), F32),
                        pltpu.VMEM((rows, A_WIDTH), BF16)])
    return pl.pallas_call(
        functools.partial(_moba_sample_kernel, nbk=nbk, dec_seq=dec_seq, bps=bps),
        out_shape=jax.ShapeDtypeStruct((db, dec_seq, A_WIDTH), BF16),
        grid_spec=grid_spec,
        compiler_params=_params(("arbitrary", "arbitrary")),
        name="moba_sample",
    )(page_table.reshape(-1), q, k_new, v_new, sbt, ob,
      *([cache_k] * (ppb * bps)), *([cache_v] * (ppb * bps)))


def _hgrn_kernel(q_ref, k_ref, v_ref, g_ref, gate_ref, s0_ref, og_ref, tril_ref, pick_ref, rec_ref, sfin_ref,
                 st_ref, *, n_chunks):
    tb = pl.program_id(2)
    n_blocks = q_ref.shape[0] // HG_BLOCK
    used = n_chunks * HG_CHUNK
    blocks = [slice(bi * HG_BLOCK, (bi + 1) * HG_BLOCK) for bi in range(n_blocks)]
    row = lax.broadcasted_iota(jnp.int32, (HG_BLOCK, HG_BLOCK), 0)
    col = lax.broadcasted_iota(jnp.int32, (HG_BLOCK, HG_BLOCK), 1)
    same_chunk = (_div_pow2(row, HG_CHUNK) == _div_pow2(col, HG_CHUNK))[:used]
    ones = jnp.ones((HG_BLOCK, HG_BLOCK), BF16)
    chunk_row = lax.broadcasted_iota(jnp.int32, (n_chunks, HG_CHUNK, B_DIM), 1)
    by_chunk = lambda a: a[:used].reshape(n_chunks, HG_CHUNK, B_DIM)

    @pl.when(tb == 0)
    def _load_state():
        st_ref[...] = s0_ref[0, 0].T

    tril = tril_ref[...]
    g_all = jnp.concatenate([g_ref[rows, :] for rows in blocks], axis=1) * LOG2_E
    g1, g2, g3 = _split3(g_all)
    cum_all = _dot(tril, g1) + _dot(tril, g2) + _dot(tril, g3)

    st = st_ref[...]
    o_blocks, off_blocks, term_blocks, v16_blocks = [], [], [], []
    for bi, rows in enumerate(blocks):
        q, k, v = q_ref[rows, :], k_ref[rows, :], v_ref[rows, :]
        cum = cum_all[:, bi * B_DIM:(bi + 1) * B_DIM]
        last = cum[HG_BLOCK - 1:HG_BLOCK, :]
        v16_blocks.append(v.astype(BF16))
        o_blocks.append(_dot_nt((q * jnp.exp2(cum)).astype(BF16), st.astype(BF16)))
        st = st * jnp.exp2(last) + _dot(v.T.astype(BF16), (k * jnp.exp2(last - cum)).astype(BF16))
        parts = [jnp.zeros((HG_CHUNK, B_DIM), F32)]
        for c in range(1, n_chunks):
            lo = c * HG_CHUNK
            r = cum[lo - 1:lo, :]
            k_c = (k[:lo] * jnp.exp2(r - cum[:lo])).astype(BF16)
            k_c = jnp.concatenate([k_c, jnp.zeros((HG_BLOCK - lo, B_DIM), BF16)], axis=0)
            q_c = (q[lo:lo + HG_CHUNK] * jnp.exp2(cum[lo:lo + HG_CHUNK] - r)).astype(BF16)
            parts.append(_dot_nt(q_c, k_c))
        off_blocks.append(jnp.concatenate(parts, axis=0))
        cum3, q3, k3 = by_chunk(cum), by_chunk(q), by_chunk(k)
        terms = []
        for s in range(HG_CHUNK):
            decay = jnp.exp2(jnp.where(chunk_row >= s, cum3 - cum3[:, s:s + 1, :], -jnp.inf))
            terms.append((q3 * decay * k3[:, s:s + 1, :]).reshape(used, B_DIM).astype(BF16))
        term_blocks.append(jnp.concatenate(terms, axis=1))
    st_ref[...] = st

    inside_all = _dot(jnp.concatenate(term_blocks, axis=0), pick_ref[...])
    for bi in range(n_blocks):
        inside = inside_all[bi * used:(bi + 1) * used]
        score = off_blocks[bi] + jnp.where(same_chunk, inside, 0.0)
        intra = _dot(score.astype(BF16), v16_blocks[bi])
        if used < HG_BLOCK:
            intra = jnp.concatenate([intra, jnp.zeros((HG_BLOCK - used, B_DIM), F32)], axis=0)
        o_blocks[bi] = o_blocks[bi] + intra
    o = jnp.concatenate(o_blocks, axis=0)
    ms = _dot((o * o).astype(BF16), ones) * (1.0 / B_DIM)
    gate = gate_ref[...]
    rec_ref[...] = ((o * lax.rsqrt(ms + NORM_EPS) * og_ref[...]) * _silu(gate)).astype(rec_ref.dtype)

    @pl.when(tb == pl.num_programs(2) - 1)
    def _store_state():
        sfin_ref[0, 0] = st.T


def _hgrn(qb, kb, ib, lf, gb, s0, out_gain, batch, seq, tile, n_chunks):
    nt = seq // tile
    tok = pl.BlockSpec((tile, B_DIM), lambda b, h, t: (b * nt + t, h))
    state = pl.BlockSpec((1, 1, B_DIM, B_DIM), lambda b, h, t: (b, h, 0, 0))
    idx = jnp.arange(HG_BLOCK)
    tril = (idx[:, None] >= idx[None, :]).astype(BF16)
    pick = (jnp.repeat(jnp.arange(HG_CHUNK), B_DIM)[:, None] == (idx % HG_CHUNK)[None, :]).astype(BF16)
    return pl.pallas_call(
        functools.partial(_hgrn_kernel, n_chunks=n_chunks),
        out_shape=(jax.ShapeDtypeStruct((batch * seq, B_WIDTH), BF16),
                   jax.ShapeDtypeStruct(s0.shape, F32)),
        grid=(batch, B_HEADS, nt),
        in_specs=[tok, tok, tok, tok, tok, state, pl.BlockSpec((1, B_DIM), lambda b, h, t: (0, h)),
                  _resident(tril.shape), _resident(pick.shape)],
        out_specs=(tok, state),
        scratch_shapes=[pltpu.VMEM((B_DIM, B_DIM), F32)],
        compiler_params=_params(("arbitrary", "arbitrary", "arbitrary")),
        name="hgrn",
    )(qb, kb, ib, lf, gb, s0, out_gain, tril, pick)


def kernel(x_prompt, x_sample, cache_k, cache_v, state_hgrn, page_table, rel_bias_table, lb_logits,
           ffn1_norm, ffn1_gate, ffn1_up, ffn1_down, mix_norm, w_in, q_norm, k_norm, hgrn_out_norm,
           w_out, ffn2_norm, ffn2_gate, ffn2_up, ffn2_down):
    batch, seq, _ = x_prompt.shape
    db, dec_seq, _ = x_sample.shape
    depth, n_phys = cache_k.shape[:2]
    assert depth == 1 and lb_logits.shape[0] == 2
    past = page_table.shape[1] * PAGE_SIZE
    assert seq % MOBA_BLOCK == 0 and past % MOBA_BLOCK == 0 and dec_seq <= HG_CHUNK
    n_p, n_s = batch * seq, db * dec_seq

    bf = lambda w: w[0].astype(BF16)
    gain = lambda g: g[0].reshape(1, -1)
    ffn1 = (gain(ffn1_norm), bf(ffn1_gate), bf(ffn1_up), bf(ffn1_down))
    ffn2 = (gain(ffn2_norm), bf(ffn2_gate), bf(ffn2_up), bf(ffn2_down))
    wo = bf(w_out)
    w_in16 = bf(w_in)
    per_lane = lambda g: jnp.broadcast_to(jnp.tile(g[0], A_HEADS)[:, None], (A_WIDTH, 128))
    proj_w = (gain(mix_norm), w_in16[:, :3 * GROUP_W].T, w_in16[:, 3 * GROUP_W:],
              per_lane(q_norm), per_lane(k_norm), lb_logits)
    out_gain = hgrn_out_norm[0].reshape(1, -1)
    w_bias, sbt, ob = _bias_tables(rel_bias_table.T, seq, past, dec_seq)

    tm = 512
    y = _ffn(x_prompt.reshape(n_p, D_MODEL), *ffn1, tm)
    q_t, kp_t, vp_t, qb, kb, ib, lf, gb = _proj(y, *proj_w, seq, tm)
    attn = _moba_prompt(q_t, kp_t, vp_t, w_bias.reshape(A_HEADS // 2, 2, -1), batch, seq)
    s0 = jnp.zeros((batch, B_HEADS, B_DIM, B_DIM), F32)
    rec, s_p = _hgrn(qb, kb, ib, lf, gb, s0, out_gain, batch, seq, 512, HG_BLOCK // HG_CHUNK)
    y_p = _ffn(y, *ffn2, tm, mix=(attn, rec, wo))
    heads_p = lambda a: a.reshape(batch, A_HEADS, A_HEAD_DIM, seq).transpose(0, 3, 1, 2)[None]

    ys = _ffn(x_sample.reshape(n_s, D_MODEL), *ffn1, n_s)
    q_t, k_t, v_t, qb, kb, ib, lf, gb = _proj(ys, *proj_w, n_s, n_s)
    q_s, k_s, v_s = (a.T.reshape(db, dec_seq, A_WIDTH) for a in (q_t, k_t, v_t))
    pad_rows = lambda a, r: jnp.pad(a.reshape(db, dec_seq, -1), ((0, 0), (0, r - dec_seq), (0, 0)))
    pages = lambda c: c[0].transpose(0, 2, 3, 1).reshape(n_phys, A_WIDTH, PAGE_SIZE)
    attn = _moba_sample(q_s, pad_rows(k_s, 8), pad_rows(v_s, 8), pages(cache_k), pages(cache_v),
                        page_table, sbt, ob, bps=math.gcd(past // MOBA_BLOCK, SAMPLE_BLOCKS_PER_STEP))
    blocked = lambda a: pad_rows(a, HG_BLOCK).reshape(db * HG_BLOCK, -1)
    rec, s_s = _hgrn(blocked(qb), blocked(kb), blocked(ib), blocked(lf), blocked(gb), state_hgrn[0],
                     out_gain, db, HG_BLOCK, HG_BLOCK, 1)
    rec = rec.reshape(db, HG_BLOCK, B_WIDTH)[:, :dec_seq].reshape(n_s, B_WIDTH)
    y_s = _ffn(ys, *ffn2, n_s, mix=(attn.reshape(n_s, A_WIDTH), rec, wo))
    heads_s = lambda a: a.reshape(1, db, dec_seq, A_HEADS, A_HEAD_DIM)

    return (y_p.reshape(batch, seq, D_MODEL), y_s.reshape(db, dec_seq, D_MODEL),
            heads_p(kp_t), heads_p(vp_t), s_p[None],
            heads_s(k_s), heads_s(v_s), s_s[None])
```

```python
import functools
import math

import jax
import jax.numpy as jnp
from jax import lax
from jax.experimental import pallas as pl
from jax.experimental.pallas import tpu as pltpu

F32 = jnp.float32
BF16 = jnp.bfloat16

D_MODEL = 1024
A_HEADS = 8
A_HEAD_DIM = 64
A_WIDTH = A_HEADS * A_HEAD_DIM
B_HEADS = 4
B_DIM = 128
B_WIDTH = B_HEADS * B_DIM
GROUP_W = 512
N_GROUPS = 7
D_FF = 2816
MOBA_BLOCK = 256
MOBA_TOPK = 3
PAGE_SIZE = 128
N_BUCKETS = 32
MAX_EXACT = N_BUCKETS // 2
REL_MAX_DIST = 4096
HG_CHUNK = 16
HG_BLOCK = 128
NORM_EPS = 1e-6
NEG_INF = -1e30
ATTN_SCALE = A_HEAD_DIM ** -0.5
LOG2_E = math.log2(math.e)

FF_CHUNK = 256
SAMPLE_BLOCKS_PER_STEP = 8
MOBA_BLOCKS_PER_TRIP = 4
MOBA_ONES_ROWS = 16
VMEM_LIMIT = 48 * 1024 * 1024


def _resident(shape):
    nd = len(shape)
    return pl.BlockSpec(shape, lambda *_: (0,) * nd, pipeline_mode=pl.Buffered(1))


def _params(semantics):
    return pltpu.CompilerParams(dimension_semantics=semantics, vmem_limit_bytes=VMEM_LIMIT)


def _dot(a, b):
    return jnp.dot(a, b, preferred_element_type=F32)


def _dot_nt(a, b):
    return lax.dot_general(a, b, (((1,), (1,)), ((), ())), preferred_element_type=F32)


def _split3(a):
    a1 = a.astype(BF16)
    r1 = a - a1.astype(F32)
    a2 = r1.astype(BF16)
    a3 = (r1 - a2.astype(F32)).astype(BF16)
    return a1, a2, a3


def _div_pow2(x, d):
    return lax.shift_right_logical(x, int(math.log2(d)))


def _rms_rows(x, gain):
    ms = jnp.mean(x * x, axis=-1, keepdims=True)
    return x * lax.rsqrt(ms + NORM_EPS) * gain


def _silu(x):
    return x * jax.nn.sigmoid(x)


def _ffn_kernel(*refs, with_mix):
    if with_mix:
        x_ref, attn_ref, rec_ref, wo_ref, gain_ref, wg_ref, wu_ref, wd_ref, o_ref = refs
        y = (x_ref[...] + _dot(attn_ref[...], wo_ref[:A_WIDTH, :])
             + _dot(rec_ref[...], wo_ref[A_WIDTH:, :]))
    else:
        x_ref, gain_ref, wg_ref, wu_ref, wd_ref, o_ref = refs
        y = x_ref[...]
    h = _rms_rows(y, gain_ref[...]).astype(BF16)
    acc = jnp.zeros(y.shape, F32)
    for j in range(D_FF // FF_CHUNK):
        cols = slice(j * FF_CHUNK, (j + 1) * FF_CHUNK)
        g = _dot(h, wg_ref[:, cols])
        u = _dot(h, wu_ref[:, cols])
        acc = acc + _dot((_silu(g) * u).astype(BF16), wd_ref[cols, :])
    o_ref[...] = y + 0.5 * acc


def _ffn(x, gain, wg, wu, wd, tm, mix=None):
    n = x.shape[0]
    row = lambda w: pl.BlockSpec((tm, w), lambda i: (i, 0))
    args, specs = [x], [row(D_MODEL)]
    if mix is not None:
        attn, rec, wo = mix
        args += [attn, rec, wo]
        specs += [row(A_WIDTH), row(B_WIDTH), _resident(wo.shape)]
    args += [gain, wg, wu, wd]
    specs += [_resident(gain.shape), _resident(wg.shape), _resident(wu.shape), _resident(wd.shape)]
    return pl.pallas_call(
        functools.partial(_ffn_kernel, with_mix=mix is not None),
        out_shape=jax.ShapeDtypeStruct((n, D_MODEL), F32),
        grid=(n // tm,),
        in_specs=specs,
        out_specs=row(D_MODEL),
        compiler_params=_params(("parallel",)),
        name="ffn_mix" if mix is not None else "ffn",
    )(*args)


def _proj_kernel(y_ref, gain_ref, wat_ref, wb_ref, qg_ref, kg_ref, lbl_ref,
                 q_ref, k_ref, v_ref, qb_ref, kb_ref, ib_ref, lf_ref, gb_ref):
    h = _rms_rows(y_ref[...], gain_ref[...]).astype(BF16)
    tm = h.shape[0]

    def group_t(i):
        return _dot_nt(wat_ref[i * GROUP_W:(i + 1) * GROUP_W, :], h)

    def group(i):
        return _dot(h, wb_ref[:, i * GROUP_W:(i + 1) * GROUP_W])

    def head_norm(z, g):
        out = []
        for hd in range(A_HEADS):
            zh = z[hd * A_HEAD_DIM:(hd + 1) * A_HEAD_DIM, :]
            out.append(zh * lax.rsqrt(jnp.mean(zh * zh, axis=0, keepdims=True) + NORM_EPS))
        return jnp.concatenate(out, axis=0) * jnp.concatenate([g] * pl.cdiv(tm, 128), axis=1)[:, :tm]

    q_ref[...] = head_norm(group_t(0), qg_ref[...]) * ATTN_SCALE
    k_ref[...] = head_norm(group_t(1), kg_ref[...])
    v_ref[...] = group_t(2)
    qb_ref[...] = _silu(group(0))
    lbl = lbl_ref[...]
    e = jnp.exp(lbl - jnp.max(lbl, axis=0, keepdims=True))
    lb = e[0:1, :] / jnp.sum(e, axis=0, keepdims=True)
    forget = lb + (1.0 - lb) * jax.nn.sigmoid(group(1))
    lf_ref[...] = jnp.log(forget)
    kb_ref[...] = 1.0 - forget
    ib_ref[...] = group(2)
    gb_ref[...] = group(3)


def _proj(y, gain, wat, wb, qg, kg, lbl, seq, tm):
    n = y.shape[0]
    nt = seq // tm
    row = lambda w: pl.BlockSpec((tm, w), lambda i: (i, 0))
    col = pl.BlockSpec((GROUP_W, tm), lambda i: (i // nt, i % nt))
    out_t = jax.ShapeDtypeStruct((n // seq * GROUP_W, seq), F32)
    out = jax.ShapeDtypeStruct((n, GROUP_W), F32)
    return pl.pallas_call(
        _proj_kernel,
        out_shape=(out_t,) * 3 + (out,) * 5,
        grid=(n // tm,),
        in_specs=[row(D_MODEL)] + [_resident(a.shape) for a in (gain, wat, wb, qg, kg, lbl)],
        out_specs=(col,) * 3 + (row(GROUP_W),) * 5,
        compiler_params=_params(("parallel",)),
        name="proj",
    )(y, gain, wat, wb, qg, kg, lbl)


def _rel_bucket(n):
    nf = jnp.maximum(n, 1).astype(F32)
    large = MAX_EXACT + (jnp.log(nf / MAX_EXACT) / math.log(REL_MAX_DIST / MAX_EXACT)
                         * (N_BUCKETS - MAX_EXACT)).astype(jnp.int32)
    large = jnp.minimum(large, N_BUCKETS - 1)
    return jnp.where(n < MAX_EXACT, n, large)


def _bias_kernel(tab_ref, w_ref, sbt_ref, ob_ref, *, past, dec_seq):
    tab = tab_ref[...]
    tab_rows = jnp.concatenate([tab] * dec_seq, axis=0)

    def lookup(bucket, table):
        out = jnp.zeros(bucket.shape, F32)
        for b in range(N_BUCKETS):
            out = jnp.where(bucket == b, table[:, b:b + 1], out)
        return out

    c = lax.broadcasted_iota(jnp.int32, w_ref.shape, 1)
    w_ref[...] = lookup(_rel_bucket(jnp.maximum(c - MOBA_BLOCK, 0)), tab)
    t = _div_pow2(lax.broadcasted_iota(jnp.int32, sbt_ref.shape, 0), A_HEADS)
    kpos = lax.broadcasted_iota(jnp.int32, sbt_ref.shape, 1)
    sbt_ref[...] = lookup(_rel_bucket(jnp.maximum(past + t - kpos, 0)), tab_rows)
    t = _div_pow2(lax.broadcasted_iota(jnp.int32, ob_ref.shape, 0), A_HEADS)
    s = lax.broadcasted_iota(jnp.int32, ob_ref.shape, 1)
    ob_ref[...] = lookup(_rel_bucket(jnp.maximum(t - s, 0)), tab_rows)


def _bias_tables(tab_t, seq, past, dec_seq):
    rows = dec_seq * A_HEADS
    return pl.pallas_call(
        functools.partial(_bias_kernel, past=past, dec_seq=dec_seq),
        out_shape=(jax.ShapeDtypeStruct((A_HEADS, seq + MOBA_BLOCK), F32),
                   jax.ShapeDtypeStruct((rows, past), F32),
                   jax.ShapeDtypeStruct((rows, 128), F32)),
        name="rel_bias",
    )(tab_t)


def _top_k_mask(gate, k, axis):
    index = lax.broadcasted_iota(jnp.int32, gate.shape, axis).astype(F32)
    sel = jnp.zeros(gate.shape, F32)
    for _ in range(k):
        mx = jnp.max(gate, axis=axis, keepdims=True)
        first = jnp.min(jnp.where(gate == mx, index, float(gate.shape[axis])), axis=axis, keepdims=True)
        chosen = index == first
        sel = jnp.where(chosen & (mx > -jnp.inf), 1.0, sel)
        gate = jnp.where(chosen, -jnp.inf, gate)
    return sel


def _moba_prompt_kernel(q_ref, k_ref, v_ref, w_ref, o_ref, kb_ref, vt_ref, km_ref, tt_ref, s_ref, *, nb):
    b = pl.program_id(1)
    i = pl.program_id(2)
    blk = MOBA_BLOCK

    key_j = lax.broadcasted_iota(jnp.int32, (blk, blk), 0)
    qry_c = lax.broadcasted_iota(jnp.int32, (blk, blk), 1)

    @pl.when((b == 0) & (i == 0))
    def _build_bias_tiles():
        for hh in range(2):
            for d in range(nb):
                win = jnp.broadcast_to(w_ref[0, hh:hh + 1, d * blk:(d + 2) * blk], (blk, 2 * blk))
                tile = pltpu.roll(win, 0, 1, stride=1, stride_axis=0)[:, blk:] * LOG2_E
                tt_ref[hh, d] = jnp.where(qry_c >= key_j, tile, NEG_INF) if d == 0 else tile

    @pl.when(i == 0)
    def _prepare_sequence():
        lane = lax.broadcasted_iota(jnp.int32, (blk, 128), 1)
        ones = jnp.ones((MOBA_ONES_ROWS, blk), F32)
        for n in range(nb):
            cols = slice(n * blk, (n + 1) * blk)
            kblk = k_ref[:, cols].T
            kb_ref[n] = jnp.concatenate([kblk, jnp.where(lane == n, 1.0, 0.0)], axis=1).astype(BF16)
            km_ref[n:n + 1, :] = jnp.mean(kblk, axis=0, keepdims=True)
            for hh in range(2):
                v_h = v_ref[hh * A_HEAD_DIM:(hh + 1) * A_HEAD_DIM, cols]
                vt_ref[n, hh] = jnp.concatenate([v_h, ones], axis=0).astype(BF16)

    q_t = q_ref[...] * LOG2_E
    feat = lax.broadcasted_iota(jnp.int32, q_t.shape, 0)
    km = km_ref[...]
    km_lane = lax.broadcasted_iota(jnp.int32, km.shape, 1)
    blk_id = lax.broadcasted_iota(jnp.int32, (nb, blk), 0)

    fold = lambda a, op: op(a.reshape(blk // 8, 8, blk), axis=0)

    qp = []
    for hh in range(2):
        lo = hh * A_HEAD_DIM
        q_h = jnp.where((feat >= lo) & (feat < lo + A_HEAD_DIM), q_t, 0.0)
        km_h = jnp.where((km_lane >= lo) & (km_lane < lo + A_HEAD_DIM), km, 0.0)
        k1, k2, _ = _split3(km_h)
        q1, q2, _ = _split3(q_h)
        gate = _dot(k1, q1) + _dot(k1, q2) + _dot(k2, q1)
        gate = jnp.where(blk_id < i, gate, -jnp.inf)
        attends = jnp.where(blk_id == i, 1.0, _top_k_mask(gate, MOBA_TOPK, 0))
        penalty = jnp.concatenate([jnp.where(attends > 0.0, 0.0, NEG_INF),
                                   jnp.zeros((128 - nb, blk), F32)], axis=0)
        qp.append(jnp.concatenate([q1, penalty.astype(BF16)], axis=0))

    per_trip = math.gcd(nb, MOBA_BLOCKS_PER_TRIP)
    trips = _div_pow2(i + per_trip, per_trip)

    def walk(body, init):
        def trip(j, carry):
            for u in range(per_trip):
                carry = body(per_trip * j + u, carry)
            return carry
        return lax.fori_loop(0, trips, trip, init)

    def logits(n, m8):
        out = []
        for hh in range(2):
            s = _dot(kb_ref[n], qp[hh]) + tt_ref[hh, jnp.maximum(i - n, 0)]
            s_ref[hh, n] = s
            out.append(jnp.maximum(m8[hh], fold(s, jnp.max)))
        return tuple(out)

    m8 = walk(logits, (jnp.full((8, blk), -jnp.inf, F32),) * 2)
    m = [jnp.max(a, axis=0, keepdims=True) for a in m8]

    def values(n, acc):
        return tuple(acc[hh] + _dot(vt_ref[n, hh], jnp.exp2(s_ref[hh, n] - m[hh]).astype(BF16)) for hh in range(2))

    acc = walk(values, (jnp.zeros((A_HEAD_DIM + MOBA_ONES_ROWS, blk), F32),) * 2)
    out_t = jnp.concatenate([a[:A_HEAD_DIM] / a[A_HEAD_DIM:A_HEAD_DIM + 1] for a in acc], axis=0)
    o_ref[...] = out_t.T.astype(o_ref.dtype)


def _moba_prompt(q_t, k_t, v_t, w_pairs, batch, seq):
    nb = seq // MOBA_BLOCK
    pairs = A_HEADS // 2
    q_tile = pl.BlockSpec((128, MOBA_BLOCK), lambda p, b, i: (b * pairs + p, i))
    whole = pl.BlockSpec((128, seq), lambda p, b, i: (b * pairs + p, 0))
    return pl.pallas_call(
        functools.partial(_moba_prompt_kernel, nb=nb),
        out_shape=jax.ShapeDtypeStruct((batch * seq, A_WIDTH), BF16),
        grid=(pairs, batch, nb),
        in_specs=[q_tile, whole, whole,
                  pl.BlockSpec((1, 2, seq + MOBA_BLOCK), lambda p, b, i: (p, 0, 0))],
        out_specs=pl.BlockSpec((MOBA_BLOCK, 128), lambda p, b, i: (b * nb + i, p)),
        scratch_shapes=[pltpu.VMEM((nb, MOBA_BLOCK, 256), BF16),
                        pltpu.VMEM((nb, 2, A_HEAD_DIM + MOBA_ONES_ROWS, MOBA_BLOCK), BF16),
                        pltpu.VMEM((nb, 128), F32),
                        pltpu.VMEM((2, nb, MOBA_BLOCK, MOBA_BLOCK), F32),
                        pltpu.VMEM((2, nb, MOBA_BLOCK, MOBA_BLOCK), F32)],
        compiler_params=_params(("arbitrary", "arbitrary", "arbitrary")),
        name="moba_prompt",
    )(q_t, k_t, v_t, w_pairs)


def _moba_sample_kernel(pt_ref, q_ref, kn_ref, vn_ref, sbt_ref, ob_ref, *refs, nbk, dec_seq, bps):
    del pt_ref
    n_pages = 2 * bps
    k_refs, v_refs = refs[:n_pages], refs[n_pages:2 * n_pages]
    o_ref, m_sc, l_sc, acc_sc, g_sc, q16_sc = refs[2 * n_pages:]
    j = pl.program_id(1)
    rows = dec_seq * A_HEADS
    head_of_row = lax.broadcasted_iota(jnp.int32, (A_HEADS, A_WIDTH), 0)
    head_of_lane = _div_pow2(lax.broadcasted_iota(jnp.int32, (A_HEADS, A_WIDTH), 1), A_HEAD_DIM)
    head_mask = head_of_row == head_of_lane
    wide = lambda a: jnp.broadcast_to(a, (rows, 128))

    @pl.when(j == 0)
    def _expand_queries():
        for t in range(dec_seq):
            q_t = jnp.where(head_mask, jnp.broadcast_to(q_ref[0, t:t + 1, :], head_mask.shape), 0.0)
            q16_sc[t * A_HEADS:(t + 1) * A_HEADS, :] = q_t.astype(BF16)

    q16 = q16_sc[...]
    for r in range(bps):
        n = j * bps + r
        kblk = jnp.concatenate([k_refs[2 * r][0], k_refs[2 * r + 1][0]], axis=1).astype(BF16)
        vblk = jnp.concatenate([v_refs[2 * r][0], v_refs[2 * r + 1][0]], axis=1).astype(BF16)
        raw = _dot(q16, kblk)
        s = raw + sbt_ref[:, r * MOBA_BLOCK:(r + 1) * MOBA_BLOCK]
        m = jnp.max(s, axis=1, keepdims=True)
        p = jnp.exp(s - m)
        m_sc[n] = wide(m)
        l_sc[n] = wide(jnp.sum(p, axis=1, keepdims=True))
        acc_sc[n] = _dot_nt(p.astype(BF16), vblk)
        g_sc[n] = wide(jnp.sum(raw, axis=1, keepdims=True) * (1.0 / MOBA_BLOCK))

    @pl.when(j == pl.num_programs(1) - 1)
    def _merge():
        gates = [g_sc[n] for n in range(nbk)]
        picked = [jnp.zeros((rows, 128), jnp.bool_)] * nbk
        for _ in range(min(MOBA_TOPK, nbk)):
            mx = functools.reduce(jnp.maximum, gates)
            first = functools.reduce(
                jnp.minimum, [jnp.where(g == mx, float(n), float(nbk)) for n, g in enumerate(gates)])
            for n in range(nbk):
                chosen = first == float(n)
                picked[n] = picked[n] | (chosen & (mx > -jnp.inf))
                gates[n] = jnp.where(chosen, -jnp.inf, gates[n])
        zeros = jnp.zeros((128 - kn_ref.shape[1], A_WIDTH), F32)
        k_own = jnp.concatenate([kn_ref[0], zeros], axis=0).astype(BF16)
        v_own = jnp.concatenate([vn_ref[0], zeros], axis=0).astype(BF16)
        s = _dot_nt(q16, k_own) + ob_ref[...]
        t_of_row = _div_pow2(lax.broadcasted_iota(jnp.int32, s.shape, 0), A_HEADS)
        s_idx = lax.broadcasted_iota(jnp.int32, s.shape, 1)
        s = jnp.where(s_idx <= t_of_row, s, NEG_INF)
        m_own = wide(jnp.max(s, axis=1, keepdims=True))
        p = jnp.exp(s - m_own)
        l_own = wide(jnp.sum(p, axis=1, keepdims=True))
        acc_own = _dot(p.astype(BF16), v_own)
        m_all = m_own
        for n in range(nbk):
            m_all = jnp.maximum(m_all, jnp.where(picked[n], m_sc[n], -jnp.inf))
        lanes4 = lambda a: jnp.concatenate([a] * (A_WIDTH // 128), axis=1)
        w = jnp.exp(m_own - m_all)
        l_all = w * l_own
        acc = lanes4(w) * acc_own
        for n in range(nbk):
            w = jnp.where(picked[n], jnp.exp(m_sc[n] - m_all), 0.0)
            l_all = l_all + w * l_sc[n]
            acc = acc + lanes4(w) * acc_sc[n]
        res = acc / lanes4(l_all)
        out = [jnp.sum(jnp.where(head_mask, res[t * A_HEADS:(t + 1) * A_HEADS, :], 0.0),
                       axis=0, keepdims=True) for t in range(dec_seq)]
        o_ref[0] = jnp.concatenate(out, axis=0).astype(o_ref.dtype)


def _moba_sample(q, k_new, v_new, cache_k, cache_v, page_table, sbt, ob, bps):
    db, dec_seq, _ = q.shape
    n_pages = page_table.shape[1]
    nbk = n_pages * PAGE_SIZE // MOBA_BLOCK
    rows = dec_seq * A_HEADS
    ppb = MOBA_BLOCK // PAGE_SIZE
    assert ppb == 2 and nbk % bps == 0
    per_seq = lambda b, j, pt: (b, 0, 0)
    page = lambda r: pl.BlockSpec(
        (1, A_WIDTH, PAGE_SIZE), lambda b, j, pt: (pt[b * n_pages + j * (ppb * bps) + r], 0, 0))
    pages = [page(r) for r in range(ppb * bps)]
    grid_spec = pltpu.PrefetchScalarGridSpec(
        num_scalar_prefetch=1,
        grid=(db, nbk // bps),
        in_specs=[pl.BlockSpec((1, dec_seq, A_WIDTH), per_seq),
                  pl.BlockSpec((1, k_new.shape[1], A_WIDTH), per_seq),
                  pl.BlockSpec((1, v_new.shape[1], A_WIDTH), per_seq),
                  pl.BlockSpec((rows, bps * MOBA_BLOCK), lambda b, j, pt: (0, j)),
                  pl.BlockSpec((rows, 128), lambda b, j, pt: (0, 0))] + pages + pages,
        out_specs=pl.BlockSpec((1, dec_seq, A_WIDTH), per_seq),
        scratch_shapes=[pltpu.VMEM((nbk, rows, 128), F32),
                        pltpu.VMEM((nbk, rows, 128), F32),
                        pltpu.VMEM((nbk, rows, A_WIDTH), F32),
                        pltpu.VMEM((nbk, rows, 128), F32),
                        pltpu.VMEM((rows, A_WIDTH), BF16)])
    return pl.pallas_call(
        functools.partial(_moba_sample_kernel, nbk=nbk, dec_seq=dec_seq, bps=bps),
        out_shape=jax.ShapeDtypeStruct((db, dec_seq, A_WIDTH), BF16),
        grid_spec=grid_spec,
        compiler_params=_params(("arbitrary", "arbitrary")),
        name="moba_sample",
    )(page_table.reshape(-1), q, k_new, v_new, sbt, ob,
      *([cache_k] * (ppb * bps)), *([cache_v] * (ppb * bps)))


def _hgrn_kernel(q_ref, k_ref, v_ref, g_ref, gate_ref, s0_ref, og_ref, tril_ref, pick_ref, rec_ref, sfin_ref,
                 st_ref, cum_ref, *, n_chunks):
    tb = pl.program_id(2)
    heads = range(q_ref.shape[1] // B_DIM)

    @pl.when(tb == 0)
    def _load_state():
        for h in heads:
            st_ref[h] = s0_ref[0, h].T

    head = lambda ref, h: ref.at[:, h * B_DIM:(h + 1) * B_DIM]
    used = n_chunks * HG_CHUNK
    work, terms = [], []
    for h in heads:
        o_blocks, off_blocks, term_blocks, v16_blocks, st_ref[h] = _hgrn_scores(
            head(q_ref, h), head(k_ref, h), head(v_ref, h), head(g_ref, h), tril_ref, st_ref[h], head(cum_ref, h),
            n_chunks)
        work.append((o_blocks, off_blocks, v16_blocks))
        terms += term_blocks
    half = (len(terms) + 1) // 2
    inside = [_dot(jnp.concatenate(part, axis=0), pick_ref[...]) for part in (terms[:half], terms[half:]) if part]
    inside = [a[at * used:(at + 1) * used] for a in inside for at in range(a.shape[0] // used)]
    for h in heads:
        n_blocks = len(work[h][0])
        _hgrn_finish(*work[h], inside[h * n_blocks:(h + 1) * n_blocks], head(gate_ref, h), head(og_ref, h),
                     head(rec_ref, h), n_chunks)

    @pl.when(tb == pl.num_programs(2) - 1)
    def _store_state():
        for h in heads:
            sfin_ref[0, h] = st_ref[h].T


def _hgrn_scores(q_ref, k_ref, v_ref, g_ref, tril_ref, st, cum_ref, n_chunks):
    n_blocks = q_ref.shape[0] // HG_BLOCK
    used = n_chunks * HG_CHUNK
    blocks = [slice(bi * HG_BLOCK, (bi + 1) * HG_BLOCK) for bi in range(n_blocks)]
    row_in_chunk = lax.broadcasted_iota(jnp.int32, (used, B_DIM), 0) & (HG_CHUNK - 1)

    tril = tril_ref[...]
    g_all = jnp.concatenate([g_ref[rows, :] for rows in blocks], axis=1) * LOG2_E
    g1, g2, g3 = _split3(g_all)
    cum_all = _dot(tril, g1) + _dot(tril, g2) + _dot(tril, g3)

    o_blocks, off_blocks, term_blocks, v16_blocks = [], [], [], []
    for bi, rows in enumerate(blocks):
        q, k, v = q_ref[rows, :], k_ref[rows, :], v_ref[rows, :]
        cum = cum_all[:, bi * B_DIM:(bi + 1) * B_DIM]
        last = cum[HG_BLOCK - 1:HG_BLOCK, :]
        v16_blocks.append(v.astype(BF16))
        o_blocks.append(_dot_nt((q * jnp.exp2(cum)).astype(BF16), st.astype(BF16)))
        st = st * jnp.exp2(last) + _dot(v.T.astype(BF16), (k * jnp.exp2(last - cum)).astype(BF16))
        parts = [jnp.zeros((HG_CHUNK, B_DIM), F32)]
        for c in range(1, n_chunks):
            lo = c * HG_CHUNK
            r = cum[lo - 1:lo, :]
            k_c = (k[:lo] * jnp.exp2(r - cum[:lo])).astype(BF16)
            k_c = jnp.concatenate([k_c, jnp.zeros((HG_BLOCK - lo, B_DIM), BF16)], axis=0)
            q_c = (q[lo:lo + HG_CHUNK] * jnp.exp2(cum[lo:lo + HG_CHUNK] - r)).astype(BF16)
            parts.append(_dot_nt(q_c, k_c))
        off_blocks.append(jnp.concatenate(parts, axis=0))
        cum_ref[rows, :] = cum
        r0 = bi * HG_BLOCK
        chunk_rows = lambda ref, s: jnp.concatenate(
            [jnp.broadcast_to(ref[r0 + c * HG_CHUNK + s:r0 + c * HG_CHUNK + s + 1, :], (HG_CHUNK, B_DIM))
             for c in range(n_chunks)], axis=0)
        terms = []
        for s in range(HG_CHUNK):
            decay = jnp.exp2(jnp.where(row_in_chunk >= s, cum[:used] - chunk_rows(cum_ref, s), -jnp.inf))
            terms.append((q[:used] * decay * chunk_rows(k_ref, s)).astype(BF16))
        term_blocks.append(jnp.concatenate(terms, axis=1))
    return o_blocks, off_blocks, term_blocks, v16_blocks, st


def _hgrn_finish(o_blocks, off_blocks, v16_blocks, inside_blocks, gate_ref, og_ref, rec_ref, n_chunks):
    used = n_chunks * HG_CHUNK
    row = lax.broadcasted_iota(jnp.int32, (used, B_DIM), 0)
    col = lax.broadcasted_iota(jnp.int32, (used, B_DIM), 1)
    same_chunk = _div_pow2(row, HG_CHUNK) == _div_pow2(col, HG_CHUNK)
    ones = jnp.ones((B_DIM, B_DIM), BF16)
    out = []
    for o, off, v16, inside in zip(o_blocks, off_blocks, v16_blocks, inside_blocks):
        score = off + jnp.where(same_chunk, inside, 0.0)
        intra = _dot(score.astype(BF16), v16)
        if used < HG_BLOCK:
            intra = jnp.concatenate([intra, jnp.zeros((HG_BLOCK - used, B_DIM), F32)], axis=0)
        out.append(o + intra)
    o = jnp.concatenate(out, axis=0)
    ms = _dot((o * o).astype(BF16), ones) * (1.0 / B_DIM)
    gate = gate_ref[...]
    rec_ref[...] = ((o * lax.rsqrt(ms + NORM_EPS) * og_ref[...]) * _silu(gate)).astype(rec_ref.dtype)


def _hgrn(qb, kb, ib, lf, gb, s0, out_gain, batch, seq, tile, n_chunks, heads_per_step):
    nt = seq // tile
    width = heads_per_step * B_DIM
    tok = pl.BlockSpec((tile, width), lambda b, h, t: (b * nt + t, h))
    state = pl.BlockSpec((1, heads_per_step, B_DIM, B_DIM), lambda b, h, t: (b, h, 0, 0))
    idx = jnp.arange(HG_BLOCK)
    tril = (idx[:, None] >= idx[None, :]).astype(BF16)
    pick = (jnp.repeat(jnp.arange(HG_CHUNK), B_DIM)[:, None] == (idx % HG_CHUNK)[None, :]).astype(BF16)
    return pl.pallas_call(
        functools.partial(_hgrn_kernel, n_chunks=n_chunks),
        out_shape=(jax.ShapeDtypeStruct((batch * seq, B_WIDTH), BF16),
                   jax.ShapeDtypeStruct(s0.shape, F32)),
        grid=(batch, B_HEADS // heads_per_step, nt),
        in_specs=[tok, tok, tok, tok, tok, state, pl.BlockSpec((1, width), lambda b, h, t: (0, h)),
                  _resident(tril.shape), _resident(pick.shape)],
        out_specs=(tok, state),
        scratch_shapes=[pltpu.VMEM((heads_per_step, B_DIM, B_DIM), F32),
                        pltpu.VMEM((tile, width), F32)],
        compiler_params=_params(("arbitrary", "arbitrary", "arbitrary")),
        name="hgrn",
    )(qb, kb, ib, lf, gb, s0, out_gain, tril, pick)


def kernel(x_prompt, x_sample, cache_k, cache_v, state_hgrn, page_table, rel_bias_table, lb_logits,
           ffn1_norm, ffn1_gate, ffn1_up, ffn1_down, mix_norm, w_in, q_norm, k_norm, hgrn_out_norm,
           w_out, ffn2_norm, ffn2_gate, ffn2_up, ffn2_down):
    batch, seq, _ = x_prompt.shape
    db, dec_seq, _ = x_sample.shape
    depth, n_phys = cache_k.shape[:2]
    assert depth == 1 and lb_logits.shape[0] == 2
    past = page_table.shape[1] * PAGE_SIZE
    assert seq % MOBA_BLOCK == 0 and past % MOBA_BLOCK == 0 and dec_seq <= HG_CHUNK
    n_p, n_s = batch * seq, db * dec_seq

    bf = lambda w: w[0].astype(BF16)
    gain = lambda g: g[0].reshape(1, -1)
    ffn1 = (gain(ffn1_norm), bf(ffn1_gate), bf(ffn1_up), bf(ffn1_down))
    ffn2 = (gain(ffn2_norm), bf(ffn2_gate), bf(ffn2_up), bf(ffn2_down))
    wo = bf(w_out)
    w_in16 = bf(w_in)
    per_lane = lambda g: jnp.broadcast_to(jnp.tile(g[0], A_HEADS)[:, None], (A_WIDTH, 128))
    proj_w = (gain(mix_norm), w_in16[:, :3 * GROUP_W].T, w_in16[:, 3 * GROUP_W:],
              per_lane(q_norm), per_lane(k_norm), lb_logits)
    out_gain = hgrn_out_norm[0].reshape(1, -1)
    w_bias, sbt, ob = _bias_tables(rel_bias_table.T, seq, past, dec_seq)

    tm = 512
    y = _ffn(x_prompt.reshape(n_p, D_MODEL), *ffn1, tm)
    q_t, kp_t, vp_t, qb, kb, ib, lf, gb = _proj(y, *proj_w, seq, tm)
    attn = _moba_prompt(q_t, kp_t, vp_t, w_bias.reshape(A_HEADS // 2, 2, -1), batch, seq)
    s0 = jnp.zeros((batch, B_HEADS, B_DIM, B_DIM), F32)
    rec, s_p = _hgrn(qb, kb, ib, lf, gb, s0, out_gain, batch, seq, 512, HG_BLOCK // HG_CHUNK, 1)
    y_p = _ffn(y, *ffn2, tm, mix=(attn, rec, wo))
    heads_p = lambda a: a.reshape(batch, A_HEADS, A_HEAD_DIM, seq).transpose(0, 3, 1, 2)[None]

    ys = _ffn(x_sample.reshape(n_s, D_MODEL), *ffn1, n_s)
    q_t, k_t, v_t, qb, kb, ib, lf, gb = _proj(ys, *proj_w, n_s, n_s)
    q_s, k_s, v_s = (a.T.reshape(db, dec_seq, A_WIDTH) for a in (q_t, k_t, v_t))
    pad_rows = lambda a, r: jnp.pad(a.reshape(db, dec_seq, -1), ((0, 0), (0, r - dec_seq), (0, 0)))
    pages = lambda c: c[0].transpose(0, 2, 3, 1).reshape(n_phys, A_WIDTH, PAGE_SIZE)
    attn = _moba_sample(q_s, pad_rows(k_s, 8), pad_rows(v_s, 8), pages(cache_k), pages(cache_v),
                        page_table, sbt, ob, bps=math.gcd(past // MOBA_BLOCK, SAMPLE_BLOCKS_PER_STEP))
    blocked = lambda a: pad_rows(a, HG_BLOCK).reshape(db * HG_BLOCK, -1)
    rec, s_s = _hgrn(blocked(qb), blocked(kb), blocked(ib), blocked(lf), blocked(gb), state_hgrn[0],
                     out_gain, db, HG_BLOCK, HG_BLOCK, 1, B_HEADS)
    rec = rec.reshape(db, HG_BLOCK, B_WIDTH)[:, :dec_seq].reshape(n_s, B_WIDTH)
    y_s = _ffn(ys, *ffn2, n_s, mix=(attn.reshape(n_s, A_WIDTH), rec, wo))
    heads_s = lambda a: a.reshape(1, db, dec_seq, A_HEADS, A_HEAD_DIM)

    return (y_p.reshape(batch, seq, D_MODEL), y_s.reshape(db, dec_seq, D_MODEL),
            heads_p(kp_t), heads_p(vp_t), s_p[None],
            heads_s(k_s), heads_s(v_s), s_s[None])
```

```python
import functools
import math

import jax
import jax.numpy as jnp
from jax import lax
from jax.experimental import pallas as pl
from jax.experimental.pallas import tpu as pltpu

F32 = jnp.float32
BF16 = jnp.bfloat16

D_MODEL = 1024
A_HEADS = 8
A_HEAD_DIM = 64
A_WIDTH = A_HEADS * A_HEAD_DIM
B_HEADS = 4
B_DIM = 128
B_WIDTH = B_HEADS * B_DIM
GROUP_W = 512
N_GROUPS = 7
D_FF = 2816
MOBA_BLOCK = 256
MOBA_TOPK = 3
PAGE_SIZE = 128
N_BUCKETS = 32
MAX_EXACT = N_BUCKETS // 2
REL_MAX_DIST = 4096
HG_CHUNK = 16
HG_BLOCK = 128
NORM_EPS = 1e-6
NEG_INF = -1e30
ATTN_SCALE = A_HEAD_DIM ** -0.5
LOG2_E = math.log2(math.e)

FF_CHUNK = 256
SAMPLE_BLOCKS_PER_STEP = 8
MOBA_BLOCKS_PER_TRIP = 4
MOBA_ONES_ROWS = 16
VMEM_LIMIT = 48 * 1024 * 1024


def _resident(shape):
    nd = len(shape)
    return pl.BlockSpec(shape, lambda *_: (0,) * nd, pipeline_mode=pl.Buffered(1))


def _params(semantics):
    return pltpu.CompilerParams(dimension_semantics=semantics, vmem_limit_bytes=VMEM_LIMIT)


def _dot(a, b):
    return jnp.dot(a, b, preferred_element_type=F32)


def _dot_nt(a, b):
    return lax.dot_general(a, b, (((1,), (1,)), ((), ())), preferred_element_type=F32)


def _split3(a):
    a1 = a.astype(BF16)
    r1 = a - a1.astype(F32)
    a2 = r1.astype(BF16)
    a3 = (r1 - a2.astype(F32)).astype(BF16)
    return a1, a2, a3


def _div_pow2(x, d):
    return lax.shift_right_logical(x, int(math.log2(d)))


def _rms_rows(x, gain):
    ms = jnp.mean(x * x, axis=-1, keepdims=True)
    return x * lax.rsqrt(ms + NORM_EPS) * gain


def _silu(x):
    return x * jax.nn.sigmoid(x)


def _ffn_kernel(*refs, with_mix):
    if with_mix:
        x_ref, attn_ref, rec_ref, wo_ref, gain_ref, wg_ref, wu_ref, wd_ref, o_ref = refs
        y = (x_ref[...] + _dot(attn_ref[...], wo_ref[:A_WIDTH, :])
             + _dot(rec_ref[...], wo_ref[A_WIDTH:, :]))
    else:
        x_ref, gain_ref, wg_ref, wu_ref, wd_ref, o_ref = refs
        y = x_ref[...]
    h = _rms_rows(y, gain_ref[...]).astype(BF16)
    acc = jnp.zeros(y.shape, F32)
    for j in range(D_FF // FF_CHUNK):
        cols = slice(j * FF_CHUNK, (j + 1) * FF_CHUNK)
        g = _dot(h, wg_ref[:, cols])
        u = _dot(h, wu_ref[:, cols])
        acc = acc + _dot((_silu(g) * u).astype(BF16), wd_ref[cols, :])
    o_ref[...] = y + 0.5 * acc


def _ffn(x, gain, wg, wu, wd, tm, mix=None):
    n = x.shape[0]
    row = lambda w: pl.BlockSpec((tm, w), lambda i: (i, 0))
    args, specs = [x], [row(D_MODEL)]
    if mix is not None:
        attn, rec, wo = mix
        args += [attn, rec, wo]
        specs += [row(A_WIDTH), row(B_WIDTH), _resident(wo.shape)]
    args += [gain, wg, wu, wd]
    specs += [_resident(gain.shape), _resident(wg.shape), _resident(wu.shape), _resident(wd.shape)]
    return pl.pallas_call(
        functools.partial(_ffn_kernel, with_mix=mix is not None),
        out_shape=jax.ShapeDtypeStruct((n, D_MODEL), F32),
        grid=(n // tm,),
        in_specs=specs,
        out_specs=row(D_MODEL),
        compiler_params=_params(("parallel",)),
        name="ffn_mix" if mix is not None else "ffn",
    )(*args)


def _proj_kernel(y_ref, gain_ref, wat_ref, wb_ref, qg_ref, kg_ref, lbl_ref,
                 q_ref, k_ref, v_ref, qb_ref, kb_ref, ib_ref, lf_ref, gb_ref):
    h = _rms_rows(y_ref[...], gain_ref[...]).astype(BF16)
    tm = h.shape[0]

    def group_t(i):
        return _dot_nt(wat_ref[i * GROUP_W:(i + 1) * GROUP_W, :], h)

    def group(i):
        return _dot(h, wb_ref[:, i * GROUP_W:(i + 1) * GROUP_W])

    def head_norm(z, g):
        out = []
        for hd in range(A_HEADS):
            zh = z[hd * A_HEAD_DIM:(hd + 1) * A_HEAD_DIM, :]
            out.append(zh * lax.rsqrt(jnp.mean(zh * zh, axis=0, keepdims=True) + NORM_EPS))
        return jnp.concatenate(out, axis=0) * jnp.concatenate([g] * pl.cdiv(tm, 128), axis=1)[:, :tm]

    q_ref[...] = head_norm(group_t(0), qg_ref[...]) * ATTN_SCALE
    k_ref[...] = head_norm(group_t(1), kg_ref[...])
    v_ref[...] = group_t(2)
    qb_ref[...] = _silu(group(0))
    lbl = lbl_ref[...]
    e = jnp.exp(lbl - jnp.max(lbl, axis=0, keepdims=True))
    lb = e[0:1, :] / jnp.sum(e, axis=0, keepdims=True)
    forget = lb + (1.0 - lb) * jax.nn.sigmoid(group(1))
    lf_ref[...] = jnp.log(forget)
    kb_ref[...] = 1.0 - forget
    ib_ref[...] = group(2)
    gb_ref[...] = group(3)


def _proj(y, gain, wat, wb, qg, kg, lbl, seq, tm):
    n = y.shape[0]
    nt = seq // tm
    row = lambda w: pl.BlockSpec((tm, w), lambda i: (i, 0))
    col = pl.BlockSpec((GROUP_W, tm), lambda i: (i // nt, i % nt))
    out_t = jax.ShapeDtypeStruct((n // seq * GROUP_W, seq), F32)
    out = jax.ShapeDtypeStruct((n, GROUP_W), F32)
    return pl.pallas_call(
        _proj_kernel,
        out_shape=(out_t,) * 3 + (out,) * 5,
        grid=(n // tm,),
        in_specs=[row(D_MODEL)] + [_resident(a.shape) for a in (gain, wat, wb, qg, kg, lbl)],
        out_specs=(col,) * 3 + (row(GROUP_W),) * 5,
        compiler_params=_params(("parallel",)),
        name="proj",
    )(y, gain, wat, wb, qg, kg, lbl)


def _rel_bucket(n):
    nf = jnp.maximum(n, 1).astype(F32)
    large = MAX_EXACT + (jnp.log(nf / MAX_EXACT) / math.log(REL_MAX_DIST / MAX_EXACT)
                         * (N_BUCKETS - MAX_EXACT)).astype(jnp.int32)
    large = jnp.minimum(large, N_BUCKETS - 1)
    return jnp.where(n < MAX_EXACT, n, large)


def _bias_kernel(tab_ref, w_ref, sbt_ref, ob_ref, *, past, dec_seq):
    tab = tab_ref[...]
    tab_rows = jnp.concatenate([tab] * dec_seq, axis=0)

    def lookup(bucket, table):
        out = jnp.zeros(bucket.shape, F32)
        for b in range(N_BUCKETS):
            out = jnp.where(bucket == b, table[:, b:b + 1], out)
        return out

    c = lax.broadcasted_iota(jnp.int32, w_ref.shape, 1)
    w_ref[...] = lookup(_rel_bucket(jnp.maximum(c - MOBA_BLOCK, 0)), tab)
    t = _div_pow2(lax.broadcasted_iota(jnp.int32, sbt_ref.shape, 0), A_HEADS)
    kpos = lax.broadcasted_iota(jnp.int32, sbt_ref.shape, 1)
    sbt_ref[...] = lookup(_rel_bucket(jnp.maximum(past + t - kpos, 0)), tab_rows)
    t = _div_pow2(lax.broadcasted_iota(jnp.int32, ob_ref.shape, 0), A_HEADS)
    s = lax.broadcasted_iota(jnp.int32, ob_ref.shape, 1)
    ob_ref[...] = lookup(_rel_bucket(jnp.maximum(t - s, 0)), tab_rows)


def _bias_tables(tab_t, seq, past, dec_seq):
    rows = dec_seq * A_HEADS
    return pl.pallas_call(
        functools.partial(_bias_kernel, past=past, dec_seq=dec_seq),
        out_shape=(jax.ShapeDtypeStruct((A_HEADS, seq + MOBA_BLOCK), F32),
                   jax.ShapeDtypeStruct((rows, past), F32),
                   jax.ShapeDtypeStruct((rows, 128), F32)),
        name="rel_bias",
    )(tab_t)


def _top_k_mask(gate, k, axis):
    index = lax.broadcasted_iota(jnp.int32, gate.shape, axis).astype(F32)
    sel = jnp.zeros(gate.shape, F32)
    for _ in range(k):
        mx = jnp.max(gate, axis=axis, keepdims=True)
        first = jnp.min(jnp.where(gate == mx, index, float(gate.shape[axis])), axis=axis, keepdims=True)
        chosen = index == first
        sel = jnp.where(chosen & (mx > -jnp.inf), 1.0, sel)
        gate = jnp.where(chosen, -jnp.inf, gate)
    return sel


def _moba_prompt_kernel(q_ref, k_ref, v_ref, w_ref, o_ref, kb_ref, vt_ref, km_ref, tt_ref, s_ref, *, nb):
    b = pl.program_id(1)
    i = pl.program_id(2)
    blk = MOBA_BLOCK

    key_j = lax.broadcasted_iota(jnp.int32, (blk, blk), 0)
    qry_c = lax.broadcasted_iota(jnp.int32, (blk, blk), 1)

    @pl.when((b == 0) & (i == 0))
    def _build_bias_tiles():
        for hh in range(2):
            for d in range(nb):
                win = jnp.broadcast_to(w_ref[0, hh:hh + 1, d * blk:(d + 2) * blk], (blk, 2 * blk))
                tile = pltpu.roll(win, 0, 1, stride=1, stride_axis=0)[:, blk:] * LOG2_E
                tt_ref[hh, d] = jnp.where(qry_c >= key_j, tile, NEG_INF) if d == 0 else tile

    @pl.when(i == 0)
    def _prepare_sequence():
        lane = lax.broadcasted_iota(jnp.int32, (blk, 128), 1)
        ones = jnp.ones((MOBA_ONES_ROWS, blk), F32)
        for n in range(nb):
            cols = slice(n * blk, (n + 1) * blk)
            kblk = k_ref[:, cols].T
            kb_ref[n] = jnp.concatenate([kblk, jnp.where(lane == n, 1.0, 0.0)], axis=1).astype(BF16)
            km_ref[n:n + 1, :] = jnp.mean(kblk, axis=0, keepdims=True)
            for hh in range(2):
                v_h = v_ref[hh * A_HEAD_DIM:(hh + 1) * A_HEAD_DIM, cols]
                vt_ref[n, hh] = jnp.concatenate([v_h, ones], axis=0).astype(BF16)

    q_t = q_ref[...] * LOG2_E
    feat = lax.broadcasted_iota(jnp.int32, q_t.shape, 0)
    km = km_ref[...]
    km_lane = lax.broadcasted_iota(jnp.int32, km.shape, 1)
    blk_id = lax.broadcasted_iota(jnp.int32, (nb, blk), 0)

    fold = lambda a, op: op(a.reshape(blk // 8, 8, blk), axis=0)

    qp = []
    for hh in range(2):
        lo = hh * A_HEAD_DIM
        q_h = jnp.where((feat >= lo) & (feat < lo + A_HEAD_DIM), q_t, 0.0)
        km_h = jnp.where((km_lane >= lo) & (km_lane < lo + A_HEAD_DIM), km, 0.0)
        k1, k2, _ = _split3(km_h)
        q1, q2, _ = _split3(q_h)
        gate = _dot(k1, q1) + _dot(k1, q2) + _dot(k2, q1)
        gate = jnp.where(blk_id < i, gate, -jnp.inf)
        attends = jnp.where(blk_id == i, 1.0, _top_k_mask(gate, MOBA_TOPK, 0))
        penalty = jnp.concatenate([jnp.where(attends > 0.0, 0.0, NEG_INF),
                                   jnp.zeros((128 - nb, blk), F32)], axis=0)
        qp.append(jnp.concatenate([q1, penalty.astype(BF16)], axis=0))

    per_trip = math.gcd(nb, MOBA_BLOCKS_PER_TRIP)
    trips = _div_pow2(i + per_trip, per_trip)

    pairs = lambda j: [(per_trip * j + u, hh) for u in range(per_trip) for hh in range(2)]

    def logits(j, m8):
        raw = [_dot(kb_ref[n], qp[hh]) for n, hh in pairs(j)]
        out = list(m8)
        for (n, hh), r in zip(pairs(j), raw):
            s = r + tt_ref[hh, jnp.maximum(i - n, 0)]
            s_ref[hh, n] = s
            out[hh] = jnp.maximum(out[hh], fold(s, jnp.max))
        return tuple(out)

    m8 = lax.fori_loop(0, trips, logits, (jnp.full((8, blk), -jnp.inf, F32),) * 2)
    m = [jnp.max(a, axis=0, keepdims=True) for a in m8]

    def values(j, acc):
        p = [jnp.exp2(s_ref[hh, n] - m[hh]).astype(BF16) for n, hh in pairs(j)]
        out = list(acc)
        for (n, hh), ph in zip(pairs(j), p):
            out[hh] = out[hh] + _dot(vt_ref[n, hh], ph)
        return tuple(out)

    acc = lax.fori_loop(0, trips, values, (jnp.zeros((A_HEAD_DIM + MOBA_ONES_ROWS, blk), F32),) * 2)
    out_t = jnp.concatenate([a[:A_HEAD_DIM] / a[A_HEAD_DIM:A_HEAD_DIM + 1] for a in acc], axis=0)
    o_ref[...] = out_t.T.astype(o_ref.dtype)


def _moba_prompt(q_t, k_t, v_t, w_pairs, batch, seq):
    nb = seq // MOBA_BLOCK
    pairs = A_HEADS // 2
    q_tile = pl.BlockSpec((128, MOBA_BLOCK), lambda p, b, i: (b * pairs + p, i))
    whole = pl.BlockSpec((128, seq), lambda p, b, i: (b * pairs + p, 0))
    return pl.pallas_call(
        functools.partial(_moba_prompt_kernel, nb=nb),
        out_shape=jax.ShapeDtypeStruct((batch * seq, A_WIDTH), BF16),
        grid=(pairs, batch, nb),
        in_specs=[q_tile, whole, whole,
                  pl.BlockSpec((1, 2, seq + MOBA_BLOCK), lambda p, b, i: (p, 0, 0))],
        out_specs=pl.BlockSpec((MOBA_BLOCK, 128), lambda p, b, i: (b * nb + i, p)),
        scratch_shapes=[pltpu.VMEM((nb, MOBA_BLOCK, 256), BF16),
                        pltpu.VMEM((nb, 2, A_HEAD_DIM + MOBA_ONES_ROWS, MOBA_BLOCK), BF16),
                        pltpu.VMEM((nb, 128), F32),
                        pltpu.VMEM((2, nb, MOBA_BLOCK, MOBA_BLOCK), F32),
                        pltpu.VMEM((2, nb, MOBA_BLOCK, MOBA_BLOCK), F32)],
        compiler_params=_params(("arbitrary", "arbitrary", "arbitrary")),
        name="moba_prompt",
    )(q_t, k_t, v_t, w_pairs)


def _moba_sample_kernel(pt_ref, q_ref, kn_ref, vn_ref, sbt_ref, ob_ref, ck_ref, cv_ref, o_ref,
                        kbuf, vbuf, sem, m_sc, l_sc, acc_sc, g_sc, *, nbk, dec_seq, bpg):
    b = pl.program_id(0)
    n_seq = pl.num_programs(0)
    n_groups = nbk // bpg
    ppg = 2 * bpg
    rows = dec_seq * A_HEADS
    head_of_row = lax.broadcasted_iota(jnp.int32, (A_HEADS, A_WIDTH), 0)
    head_of_lane = _div_pow2(lax.broadcasted_iota(jnp.int32, (A_HEADS, A_WIDTH), 1), A_HEAD_DIM)
    head_mask = head_of_row == head_of_lane
    wide = lambda a: jnp.broadcast_to(a, (rows, 128))

    def group_copies(seq, g, slot):
        out = []
        for r in range(ppg):
            page = pt_ref[seq * (n_groups * ppg) + g * ppg + r]
            out.append(pltpu.make_async_copy(ck_ref.at[page], kbuf.at[slot, r], sem.at[slot, 0]))
            out.append(pltpu.make_async_copy(cv_ref.at[page], vbuf.at[slot, r], sem.at[slot, 1]))
        return out

    @pl.when(b == 0)
    def _first_group():
        for cp in group_copies(0, 0, 0):
            cp.start()

    q16 = jnp.concatenate(
        [jnp.where(head_mask, jnp.broadcast_to(q_ref[0, t:t + 1, :], head_mask.shape), 0.0) for t in range(dec_seq)],
        axis=0).astype(BF16)

    def group(g, _):
        slot = (b * n_groups + g) & 1
        last_group = g + 1 == n_groups

        @pl.when(jnp.logical_not(last_group & (b + 1 == n_seq)))
        def _prefetch():
            nxt_seq = jnp.where(last_group, b + 1, b)
            nxt_g = jnp.where(last_group, 0, g + 1)
            for cp in group_copies(nxt_seq, nxt_g, 1 - slot):
                cp.start()

        for cp in group_copies(b, g, slot):
            cp.wait()
        raw = jnp.concatenate(
            [_dot(q16, jnp.concatenate([kbuf[slot, 2 * r], kbuf[slot, 2 * r + 1]], axis=1).astype(BF16))
             for r in range(bpg)], axis=1)
        width = bpg * MOBA_BLOCK
        s = raw + sbt_ref[:, pl.ds(pl.multiple_of(g * width, width), width)]
        per_block = lambda a: [a[:, r * MOBA_BLOCK:(r + 1) * MOBA_BLOCK] for r in range(bpg)]
        m = [jnp.max(a, axis=1, keepdims=True) for a in per_block(s)]
        p = [jnp.exp(a - mr) for a, mr in zip(per_block(s), m)]
        for r in range(bpg):
            n = g * bpg + r
            vblk = jnp.concatenate([vbuf[slot, 2 * r], vbuf[slot, 2 * r + 1]], axis=1).astype(BF16)
            m_sc[n] = wide(m[r])
            l_sc[n] = wide(jnp.sum(p[r], axis=1, keepdims=True))
            acc_sc[n] = _dot_nt(p[r].astype(BF16), vblk)
            g_sc[n] = wide(jnp.sum(per_block(raw)[r], axis=1, keepdims=True) * (1.0 / MOBA_BLOCK))
        return 0

    lax.fori_loop(0, n_groups, group, 0)

    def merge():
        gates = [g_sc[n] for n in range(nbk)]
        picked = [jnp.zeros((rows, 128), jnp.bool_)] * nbk
        for _ in range(min(MOBA_TOPK, nbk)):
            mx = functools.reduce(jnp.maximum, gates)
            first = functools.reduce(
                jnp.minimum, [jnp.where(g == mx, float(n), float(nbk)) for n, g in enumerate(gates)])
            for n in range(nbk):
                chosen = first == float(n)
                picked[n] = picked[n] | (chosen & (mx > -jnp.inf))
                gates[n] = jnp.where(chosen, -jnp.inf, gates[n])
        zeros = jnp.zeros((128 - kn_ref.shape[1], A_WIDTH), F32)
        k_own = jnp.concatenate([kn_ref[0], zeros], axis=0).astype(BF16)
        v_own = jnp.concatenate([vn_ref[0], zeros], axis=0).astype(BF16)
        s = _dot_nt(q16, k_own) + ob_ref[...]
        t_of_row = _div_pow2(lax.broadcasted_iota(jnp.int32, s.shape, 0), A_HEADS)
        s_idx = lax.broadcasted_iota(jnp.int32, s.shape, 1)
        s = jnp.where(s_idx <= t_of_row, s, NEG_INF)
        m_own = wide(jnp.max(s, axis=1, keepdims=True))
        p = jnp.exp(s - m_own)
        l_own = wide(jnp.sum(p, axis=1, keepdims=True))
        acc_own = _dot(p.astype(BF16), v_own)
        m_all = m_own
        for n in range(nbk):
            m_all = jnp.maximum(m_all, jnp.where(picked[n], m_sc[n], -jnp.inf))
        lanes4 = lambda a: jnp.concatenate([a] * (A_WIDTH // 128), axis=1)
        w = jnp.exp(m_own - m_all)
        l_all = w * l_own
        acc = lanes4(w) * acc_own
        for n in range(nbk):
            w = jnp.where(picked[n], jnp.exp(m_sc[n] - m_all), 0.0)
            l_all = l_all + w * l_sc[n]
            acc = acc + lanes4(w) * acc_sc[n]
        res = acc / lanes4(l_all)
        out = [jnp.sum(jnp.where(head_mask, res[t * A_HEADS:(t + 1) * A_HEADS, :], 0.0),
                       axis=0, keepdims=True) for t in range(dec_seq)]
        o_ref[0] = jnp.concatenate(out, axis=0).astype(o_ref.dtype)

    merge()


def _moba_sample(q, k_new, v_new, cache_k, cache_v, page_table, sbt, ob, bpg):
    db, dec_seq, _ = q.shape
    n_pages = page_table.shape[1]
    nbk = n_pages * PAGE_SIZE // MOBA_BLOCK
    rows = dec_seq * A_HEADS
    assert MOBA_BLOCK == 2 * PAGE_SIZE and nbk % bpg == 0
    per_seq = lambda b, pt: (b, 0, 0)
    whole = lambda a: pl.BlockSpec(a.shape, lambda b, pt: (0,) * a.ndim, pipeline_mode=pl.Buffered(1))
    grid_spec = pltpu.PrefetchScalarGridSpec(
        num_scalar_prefetch=1,
        grid=(db,),
        in_specs=[pl.BlockSpec((1, dec_seq, A_WIDTH), per_seq),
                  pl.BlockSpec((1, k_new.shape[1], A_WIDTH), per_seq),
                  pl.BlockSpec((1, v_new.shape[1], A_WIDTH), per_seq),
                  whole(sbt), whole(ob),
                  pl.BlockSpec(memory_space=pl.ANY), pl.BlockSpec(memory_space=pl.ANY)],
        out_specs=pl.BlockSpec((1, dec_seq, A_WIDTH), per_seq),
        scratch_shapes=[pltpu.VMEM((2, 2 * bpg, A_WIDTH, PAGE_SIZE), F32),
                        pltpu.VMEM((2, 2 * bpg, A_WIDTH, PAGE_SIZE), F32),
                        pltpu.SemaphoreType.DMA((2, 2)),
                        pltpu.VMEM((nbk, rows, 128), F32),
                        pltpu.VMEM((nbk, rows, 128), F32),
                        pltpu.VMEM((nbk, rows, A_WIDTH), F32),
                        pltpu.VMEM((nbk, rows, 128), F32)])
    return pl.pallas_call(
        functools.partial(_moba_sample_kernel, nbk=nbk, dec_seq=dec_seq, bpg=bpg),
        out_shape=jax.ShapeDtypeStruct((db, dec_seq, A_WIDTH), BF16),
        grid_spec=grid_spec,
        compiler_params=_params(("arbitrary",)),
        name="moba_sample",
    )(page_table.reshape(-1), q, k_new, v_new, sbt, ob, cache_k, cache_v)


def _hgrn_kernel(q_ref, k_ref, v_ref, g_ref, gate_ref, s0_ref, og_ref, tril_ref, pick_ref, rec_ref, sfin_ref,
                 st_ref, cum_ref, *, n_chunks):
    tb = pl.program_id(2)
    heads = range(q_ref.shape[1] // B_DIM)

    @pl.when(tb == 0)
    def _load_state():
        for h in heads:
            st_ref[h] = s0_ref[0, h].T

    head = lambda ref, h: ref.at[:, h * B_DIM:(h + 1) * B_DIM]
    used = n_chunks * HG_CHUNK
    work, terms = [], []
    for h in heads:
        o_blocks, off_blocks, term_blocks, v16_blocks, st_ref[h] = _hgrn_scores(
            head(q_ref, h), head(k_ref, h), head(v_ref, h), head(g_ref, h), tril_ref, st_ref[h], head(cum_ref, h),
            n_chunks)
        work.append((o_blocks, off_blocks, v16_blocks))
        terms += term_blocks
    half = (len(terms) + 1) // 2
    inside = [_dot(jnp.concatenate(part, axis=0), pick_ref[...]) for part in (terms[:half], terms[half:]) if part]
    inside = [a[at * used:(at + 1) * used] for a in inside for at in range(a.shape[0] // used)]
    for h in heads:
        n_blocks = len(work[h][0])
        _hgrn_finish(*work[h], inside[h * n_blocks:(h + 1) * n_blocks], head(gate_ref, h), head(og_ref, h),
                     head(rec_ref, h), n_chunks)

    @pl.when(tb == pl.num_programs(2) - 1)
    def _store_state():
        for h in heads:
            sfin_ref[0, h] = st_ref[h].T


def _hgrn_scores(q_ref, k_ref, v_ref, g_ref, tril_ref, st, cum_ref, n_chunks):
    n_blocks = q_ref.shape[0] // HG_BLOCK
    used = n_chunks * HG_CHUNK
    blocks = [slice(bi * HG_BLOCK, (bi + 1) * HG_BLOCK) for bi in range(n_blocks)]
    row_in_chunk = lax.broadcasted_iota(jnp.int32, (used, B_DIM), 0) & (HG_CHUNK - 1)

    tril = tril_ref[...]
    g_all = jnp.concatenate([g_ref[rows, :] for rows in blocks], axis=1) * LOG2_E
    g1, g2, g3 = _split3(g_all)
    cum_all = _dot(tril, g1) + _dot(tril, g2) + _dot(tril, g3)

    o_blocks, off_blocks, term_blocks, v16_blocks = [], [], [], []
    for bi, rows in enumerate(blocks):
        q, k, v = q_ref[rows, :], k_ref[rows, :], v_ref[rows, :]
        cum = cum_all[:, bi * B_DIM:(bi + 1) * B_DIM]
        last = cum[HG_BLOCK - 1:HG_BLOCK, :]
        v16_blocks.append(v.astype(BF16))
        o_blocks.append(_dot_nt((q * jnp.exp2(cum)).astype(BF16), st.astype(BF16)))
        st = st * jnp.exp2(last) + _dot(v.T.astype(BF16), (k * jnp.exp2(last - cum)).astype(BF16))
        parts = [jnp.zeros((HG_CHUNK, B_DIM), F32)]
        for c in range(1, n_chunks):
            lo = c * HG_CHUNK
            r = cum[lo - 1:lo, :]
            k_c = (k[:lo] * jnp.exp2(r - cum[:lo])).astype(BF16)
            k_c = jnp.concatenate([k_c, jnp.zeros((HG_BLOCK - lo, B_DIM), BF16)], axis=0)
            q_c = (q[lo:lo + HG_CHUNK] * jnp.exp2(cum[lo:lo + HG_CHUNK] - r)).astype(BF16)
            parts.append(_dot_nt(q_c, k_c))
        off_blocks.append(jnp.concatenate(parts, axis=0))
        cum_ref[rows, :] = cum
        r0 = bi * HG_BLOCK
        chunk_rows = lambda ref, s: jnp.concatenate(
            [jnp.broadcast_to(ref[r0 + c * HG_CHUNK + s:r0 + c * HG_CHUNK + s + 1, :], (HG_CHUNK, B_DIM))
             for c in range(n_chunks)], axis=0)
        terms = []
        for s in range(HG_CHUNK):
            decay = jnp.exp2(jnp.where(row_in_chunk >= s, cum[:used] - chunk_rows(cum_ref, s), -jnp.inf))
            terms.append((q[:used] * decay * chunk_rows(k_ref, s)).astype(BF16))
        term_blocks.append(jnp.concatenate(terms, axis=1))
    return o_blocks, off_blocks, term_blocks, v16_blocks, st


def _hgrn_finish(o_blocks, off_blocks, v16_blocks, inside_blocks, gate_ref, og_ref, rec_ref, n_chunks):
    used = n_chunks * HG_CHUNK
    row = lax.broadcasted_iota(jnp.int32, (used, B_DIM), 0)
    col = lax.broadcasted_iota(jnp.int32, (used, B_DIM), 1)
    same_chunk = _div_pow2(row, HG_CHUNK) == _div_pow2(col, HG_CHUNK)
    ones = jnp.ones((B_DIM, B_DIM), BF16)
    out = []
    for o, off, v16, inside in zip(o_blocks, off_blocks, v16_blocks, inside_blocks):
        score = off + jnp.where(same_chunk, inside, 0.0)
        intra = _dot(score.astype(BF16), v16)
        if used < HG_BLOCK:
            intra = jnp.concatenate([intra, jnp.zeros((HG_BLOCK - used, B_DIM), F32)], axis=0)
        out.append(o + intra)
    o = jnp.concatenate(out, axis=0)
    ms = _dot((o * o).astype(BF16), ones) * (1.0 / B_DIM)
    gate = gate_ref[...]
    rec_ref[...] = ((o * lax.rsqrt(ms + NORM_EPS) * og_ref[...]) * _silu(gate)).astype(rec_ref.dtype)


def _hgrn(qb, kb, ib, lf, gb, s0, out_gain, batch, seq, tile, n_chunks, heads_per_step):
    nt = seq // tile
    width = heads_per_step * B_DIM
    tok = pl.BlockSpec((tile, width), lambda b, h, t: (b * nt + t, h))
    state = pl.BlockSpec((1, heads_per_step, B_DIM, B_DIM), lambda b, h, t: (b, h, 0, 0))
    idx = jnp.arange(HG_BLOCK)
    tril = (idx[:, None] >= idx[None, :]).astype(BF16)
    pick = (jnp.repeat(jnp.arange(HG_CHUNK), B_DIM)[:, None] == (idx % HG_CHUNK)[None, :]).astype(BF16)
    return pl.pallas_call(
        functools.partial(_hgrn_kernel, n_chunks=n_chunks),
        out_shape=(jax.ShapeDtypeStruct((batch * seq, B_WIDTH), BF16),
                   jax.ShapeDtypeStruct(s0.shape, F32)),
        grid=(batch, B_HEADS // heads_per_step, nt),
        in_specs=[tok, tok, tok, tok, tok, state, pl.BlockSpec((1, width), lambda b, h, t: (0, h)),
                  _resident(tril.shape), _resident(pick.shape)],
        out_specs=(tok, state),
        scratch_shapes=[pltpu.VMEM((heads_per_step, B_DIM, B_DIM), F32),
                        pltpu.VMEM((tile, width), F32)],
        compiler_params=_params(("arbitrary", "arbitrary", "arbitrary")),
        name="hgrn",
    )(qb, kb, ib, lf, gb, s0, out_gain, tril, pick)


def kernel(x_prompt, x_sample, cache_k, cache_v, state_hgrn, page_table, rel_bias_table, lb_logits,
           ffn1_norm, ffn1_gate, ffn1_up, ffn1_down, mix_norm, w_in, q_norm, k_norm, hgrn_out_norm,
           w_out, ffn2_norm, ffn2_gate, ffn2_up, ffn2_down):
    batch, seq, _ = x_prompt.shape
    db, dec_seq, _ = x_sample.shape
    depth, n_phys = cache_k.shape[:2]
    assert depth == 1 and lb_logits.shape[0] == 2
    past = page_table.shape[1] * PAGE_SIZE
    assert seq % MOBA_BLOCK == 0 and past % MOBA_BLOCK == 0 and dec_seq <= HG_CHUNK
    n_p, n_s = batch * seq, db * dec_seq

    bf = lambda w: w[0].astype(BF16)
    gain = lambda g: g[0].reshape(1, -1)
    ffn1 = (gain(ffn1_norm), bf(ffn1_gate), bf(ffn1_up), bf(ffn1_down))
    ffn2 = (gain(ffn2_norm), bf(ffn2_gate), bf(ffn2_up), bf(ffn2_down))
    wo = bf(w_out)
    w_in16 = bf(w_in)
    per_lane = lambda g: jnp.broadcast_to(jnp.tile(g[0], A_HEADS)[:, None], (A_WIDTH, 128))
    proj_w = (gain(mix_norm), w_in16[:, :3 * GROUP_W].T, w_in16[:, 3 * GROUP_W:],
              per_lane(q_norm), per_lane(k_norm), lb_logits)
    out_gain = hgrn_out_norm[0].reshape(1, -1)
    w_bias, sbt, ob = _bias_tables(rel_bias_table.T, seq, past, dec_seq)

    tm = 512
    y = _ffn(x_prompt.reshape(n_p, D_MODEL), *ffn1, tm)
    q_t, kp_t, vp_t, qb, kb, ib, lf, gb = _proj(y, *proj_w, seq, tm)
    attn = _moba_prompt(q_t, kp_t, vp_t, w_bias.reshape(A_HEADS // 2, 2, -1), batch, seq)
    s0 = jnp.zeros((batch, B_HEADS, B_DIM, B_DIM), F32)
    rec, s_p = _hgrn(qb, kb, ib, lf, gb, s0, out_gain, batch, seq, 512, HG_BLOCK // HG_CHUNK, 1)
    y_p = _ffn(y, *ffn2, tm, mix=(attn, rec, wo))
    heads_p = lambda a: a.reshape(batch, A_HEADS, A_HEAD_DIM, seq).transpose(0, 3, 1, 2)[None]

    ys = _ffn(x_sample.reshape(n_s, D_MODEL), *ffn1, n_s)
    q_t, k_t, v_t, qb, kb, ib, lf, gb = _proj(ys, *proj_w, n_s, n_s)
    q_s, k_s, v_s = (a.T.reshape(db, dec_seq, A_WIDTH) for a in (q_t, k_t, v_t))
    pad_rows = lambda a, r: jnp.pad(a.reshape(db, dec_seq, -1), ((0, 0), (0, r - dec_seq), (0, 0)))
    pages = lambda c: c[0].transpose(0, 2, 3, 1).reshape(n_phys, A_WIDTH, PAGE_SIZE)
    attn = _moba_sample(q_s, pad_rows(k_s, 8), pad_rows(v_s, 8), pages(cache_k), pages(cache_v),
                        page_table, sbt, ob, bpg=math.gcd(past // MOBA_BLOCK, SAMPLE_BLOCKS_PER_STEP))
    blocked = lambda a: pad_rows(a, HG_BLOCK).reshape(db * HG_BLOCK, -1)
    rec, s_s = _hgrn(blocked(qb), blocked(kb), blocked(ib), blocked(lf), blocked(gb), state_hgrn[0],
                     out_gain, db, HG_BLOCK, HG_BLOCK, 1, B_HEADS)
    rec = rec.reshape(db, HG_BLOCK, B_WIDTH)[:, :dec_seq].reshape(n_s, B_WIDTH)
    y_s = _ffn(ys, *ffn2, n_s, mix=(attn.reshape(n_s, A_WIDTH), rec, wo))
    heads_s = lambda a: a.reshape(1, db, dec_seq, A_HEADS, A_HEAD_DIM)

    return (y_p.reshape(batch, seq, D_MODEL), y_s.reshape(db, dec_seq, D_MODEL),
            heads_p(kp_t), heads_p(vp_t), s_p[None],
            heads_s(k_s), heads_s(v_s), s_s[None])
```

```python
import functools
import math

import jax
import jax.numpy as jnp
from jax import lax
from jax.experimental import pallas as pl
from jax.experimental.pallas import tpu as pltpu

F32 = jnp.float32
BF16 = jnp.bfloat16

D_MODEL = 1024
A_HEADS = 8
A_HEAD_DIM = 64
A_WIDTH = A_HEADS * A_HEAD_DIM
B_HEADS = 4
B_DIM = 128
B_WIDTH = B_HEADS * B_DIM
GROUP_W = 512
N_GROUPS = 7
D_FF = 2816
MOBA_BLOCK = 256
MOBA_TOPK = 3
PAGE_SIZE = 128
N_BUCKETS = 32
MAX_EXACT = N_BUCKETS // 2
REL_MAX_DIST = 4096
HG_CHUNK = 16
HG_BLOCK = 128
NORM_EPS = 1e-6
NEG_INF = -1e30
ATTN_SCALE = A_HEAD_DIM ** -0.5
LOG2_E = math.log2(math.e)

FF_CHUNK = 256
SAMPLE_BLOCKS_PER_STEP = 8
MOBA_BLOCKS_PER_TRIP = 4
MOBA_ONES_ROWS = 16
VMEM_LIMIT = 48 * 1024 * 1024
VMEM_LIMIT_FUSED = 56 * 1024 * 1024


def _resident(shape):
    nd = len(shape)
    return pl.BlockSpec(shape, lambda *_: (0,) * nd, pipeline_mode=pl.Buffered(1))


def _params(semantics):
    return pltpu.CompilerParams(dimension_semantics=semantics, vmem_limit_bytes=VMEM_LIMIT)


def _dot(a, b):
    return jnp.dot(a, b, preferred_element_type=F32)


def _dot_nt(a, b):
    return lax.dot_general(a, b, (((1,), (1,)), ((), ())), preferred_element_type=F32)


def _split3(a):
    a1 = a.astype(BF16)
    r1 = a - a1.astype(F32)
    a2 = r1.astype(BF16)
    a3 = (r1 - a2.astype(F32)).astype(BF16)
    return a1, a2, a3


def _div_pow2(x, d):
    return lax.shift_right_logical(x, int(math.log2(d)))


def _rms_rows(x, gain):
    ms = jnp.mean(x * x, axis=-1, keepdims=True)
    return x * lax.rsqrt(ms + NORM_EPS) * gain


def _silu(x):
    return x * jax.nn.sigmoid(x)


def _ffn_chunks(h, wg_ref, wu_ref, wd_ref, acc, chunks):
    for j in chunks:
        cols = slice(j * FF_CHUNK, (j + 1) * FF_CHUNK)
        g = _dot(h, wg_ref[:, cols])
        u = _dot(h, wu_ref[:, cols])
        acc = acc + _dot((_silu(g) * u).astype(BF16), wd_ref[cols, :])
    return acc


def _ffn_kernel(*refs, with_mix):
    if with_mix:
        x_ref, attn_ref, rec_ref, wo_ref, gain_ref, wg_ref, wu_ref, wd_ref, o_ref = refs
        y = (x_ref[...] + _dot(attn_ref[...], wo_ref[:A_WIDTH, :])
             + _dot(rec_ref[...], wo_ref[A_WIDTH:, :]))
    else:
        x_ref, gain_ref, wg_ref, wu_ref, wd_ref, o_ref = refs
        y = x_ref[...]
    h = _rms_rows(y, gain_ref[...]).astype(BF16)
    acc = _ffn_chunks(h, wg_ref, wu_ref, wd_ref, jnp.zeros(y.shape, F32), range(D_FF // FF_CHUNK))
    o_ref[...] = y + 0.5 * acc


def _ffn_sample_kernel(pt_ref, x_ref, gain_ref, wg_ref, wu_ref, wd_ref, q_ref, kn_ref, vn_ref, sbt_ref, ob_ref,
                       ck_ref, cv_ref, y_ref, attn_ref, *scratch, nbk, dec_seq, bpg):
    stages = _sample_stages(pl.program_id(0), pl.num_programs(0), pt_ref, q_ref, kn_ref, vn_ref, sbt_ref, ob_ref,
                            ck_ref, cv_ref, attn_ref, *scratch, nbk=nbk, dec_seq=dec_seq, bpg=bpg)
    y = x_ref[...]
    h = _rms_rows(y, gain_ref[...]).astype(BF16)
    acc = jnp.zeros(y.shape, F32)
    n_chunks = D_FF // FF_CHUNK
    cuts = [k * n_chunks // len(stages) for k in range(len(stages) + 1)]
    for k, stage in enumerate(stages):
        stage()
        acc = _ffn_chunks(h, wg_ref, wu_ref, wd_ref, acc, range(cuts[k], cuts[k + 1]))
    y_ref[...] = y + 0.5 * acc


def _ffn(x, gain, wg, wu, wd, tm, mix=None):
    n = x.shape[0]
    row = lambda w: pl.BlockSpec((tm, w), lambda i: (i, 0))
    args, specs = [x], [row(D_MODEL)]
    if mix is not None:
        attn, rec, wo = mix
        args += [attn, rec, wo]
        specs += [row(A_WIDTH), row(B_WIDTH), _resident(wo.shape)]
    args += [gain, wg, wu, wd]
    specs += [_resident(gain.shape), _resident(wg.shape), _resident(wu.shape), _resident(wd.shape)]
    return pl.pallas_call(
        functools.partial(_ffn_kernel, with_mix=mix is not None),
        out_shape=jax.ShapeDtypeStruct((n, D_MODEL), F32),
        grid=(n // tm,),
        in_specs=specs,
        out_specs=row(D_MODEL),
        compiler_params=_params(("parallel",)),
        name="ffn_mix" if mix is not None else "ffn",
    )(*args)


def _proj_kernel(y_ref, gain_ref, wat_ref, wb_ref, qg_ref, kg_ref, lbl_ref,
                 q_ref, k_ref, v_ref, qb_ref, kb_ref, ib_ref, lf_ref, gb_ref):
    h = _rms_rows(y_ref[...], gain_ref[...]).astype(BF16)
    tm = h.shape[0]

    def group_t(i):
        return _dot_nt(wat_ref[i * GROUP_W:(i + 1) * GROUP_W, :], h)

    def group(i):
        return _dot(h, wb_ref[:, i * GROUP_W:(i + 1) * GROUP_W])

    def head_norm(z, g):
        out = []
        for hd in range(A_HEADS):
            zh = z[hd * A_HEAD_DIM:(hd + 1) * A_HEAD_DIM, :]
            out.append(zh * lax.rsqrt(jnp.mean(zh * zh, axis=0, keepdims=True) + NORM_EPS))
        return jnp.concatenate(out, axis=0) * jnp.concatenate([g] * pl.cdiv(tm, 128), axis=1)[:, :tm]

    q_ref[...] = head_norm(group_t(0), qg_ref[...]) * ATTN_SCALE
    k_ref[...] = head_norm(group_t(1), kg_ref[...])
    v_ref[...] = group_t(2)
    qb_ref[...] = _silu(group(0))
    lbl = lbl_ref[...]
    e = jnp.exp(lbl - jnp.max(lbl, axis=0, keepdims=True))
    lb = e[0:1, :] / jnp.sum(e, axis=0, keepdims=True)
    forget = lb + (1.0 - lb) * jax.nn.sigmoid(group(1))
    lf_ref[...] = jnp.log(forget)
    kb_ref[...] = 1.0 - forget
    ib_ref[...] = group(2)
    gb_ref[...] = group(3)


def _proj(y, gain, wat, wb, qg, kg, lbl, seq, tm):
    n = y.shape[0]
    nt = seq // tm
    row = lambda w: pl.BlockSpec((tm, w), lambda i: (i, 0))
    col = pl.BlockSpec((GROUP_W, tm), lambda i: (i // nt, i % nt))
    out_t = jax.ShapeDtypeStruct((n // seq * GROUP_W, seq), F32)
    out = jax.ShapeDtypeStruct((n, GROUP_W), F32)
    return pl.pallas_call(
        _proj_kernel,
        out_shape=(out_t,) * 3 + (out,) * 5,
        grid=(n // tm,),
        in_specs=[row(D_MODEL)] + [_resident(a.shape) for a in (gain, wat, wb, qg, kg, lbl)],
        out_specs=(col,) * 3 + (row(GROUP_W),) * 5,
        compiler_params=_params(("parallel",)),
        name="proj",
    )(y, gain, wat, wb, qg, kg, lbl)


def _rel_bucket(n):
    nf = jnp.maximum(n, 1).astype(F32)
    large = MAX_EXACT + (jnp.log(nf / MAX_EXACT) / math.log(REL_MAX_DIST / MAX_EXACT)
                         * (N_BUCKETS - MAX_EXACT)).astype(jnp.int32)
    large = jnp.minimum(large, N_BUCKETS - 1)
    return jnp.where(n < MAX_EXACT, n, large)


def _bias_kernel(tab_ref, w_ref, sbt_ref, ob_ref, *, past, dec_seq):
    tab = tab_ref[...]
    tab_rows = jnp.concatenate([tab] * dec_seq, axis=0)

    def lookup(bucket, table):
        out = jnp.zeros(bucket.shape, F32)
        for b in range(N_BUCKETS):
            out = jnp.where(bucket == b, table[:, b:b + 1], out)
        return out

    c = lax.broadcasted_iota(jnp.int32, w_ref.shape, 1)
    w_ref[...] = lookup(_rel_bucket(jnp.maximum(c - MOBA_BLOCK, 0)), tab)
    t = _div_pow2(lax.broadcasted_iota(jnp.int32, sbt_ref.shape, 0), A_HEADS)
    kpos = lax.broadcasted_iota(jnp.int32, sbt_ref.shape, 1)
    sbt_ref[...] = lookup(_rel_bucket(jnp.maximum(past + t - kpos, 0)), tab_rows)
    t = _div_pow2(lax.broadcasted_iota(jnp.int32, ob_ref.shape, 0), A_HEADS)
    s = lax.broadcasted_iota(jnp.int32, ob_ref.shape, 1)
    ob_ref[...] = lookup(_rel_bucket(jnp.maximum(t - s, 0)), tab_rows)


def _bias_tables(tab_t, seq, past, dec_seq):
    rows = dec_seq * A_HEADS
    return pl.pallas_call(
        functools.partial(_bias_kernel, past=past, dec_seq=dec_seq),
        out_shape=(jax.ShapeDtypeStruct((A_HEADS, seq + MOBA_BLOCK), F32),
                   jax.ShapeDtypeStruct((rows, past), F32),
                   jax.ShapeDtypeStruct((rows, 128), F32)),
        name="rel_bias",
    )(tab_t)


def _top_k_mask(gate, k, axis):
    index = lax.broadcasted_iota(jnp.int32, gate.shape, axis).astype(F32)
    sel = jnp.zeros(gate.shape, F32)
    for _ in range(k):
        mx = jnp.max(gate, axis=axis, keepdims=True)
        first = jnp.min(jnp.where(gate == mx, index, float(gate.shape[axis])), axis=axis, keepdims=True)
        chosen = index == first
        sel = jnp.where(chosen & (mx > -jnp.inf), 1.0, sel)
        gate = jnp.where(chosen, -jnp.inf, gate)
    return sel


def _moba_prompt_kernel(q_ref, k_ref, v_ref, w_ref, o_ref, kb_ref, vt_ref, km_ref, tt_ref, s_ref, *, nb):
    b = pl.program_id(1)
    i = pl.program_id(2)
    blk = MOBA_BLOCK

    key_j = lax.broadcasted_iota(jnp.int32, (blk, blk), 0)
    qry_c = lax.broadcasted_iota(jnp.int32, (blk, blk), 1)

    @pl.when((b == 0) & (i == 0))
    def _build_bias_tiles():
        for hh in range(2):
            for d in range(nb):
                win = jnp.broadcast_to(w_ref[0, hh:hh + 1, d * blk:(d + 2) * blk], (blk, 2 * blk))
                tile = pltpu.roll(win, 0, 1, stride=1, stride_axis=0)[:, blk:] * LOG2_E
                tt_ref[hh, d] = jnp.where(qry_c >= key_j, tile, NEG_INF) if d == 0 else tile

    @pl.when(i == 0)
    def _prepare_sequence():
        lane = lax.broadcasted_iota(jnp.int32, (blk, 128), 1)
        ones = jnp.ones((MOBA_ONES_ROWS, blk), F32)
        for n in range(nb):
            cols = slice(n * blk, (n + 1) * blk)
            kblk = k_ref[:, cols].T
            kb_ref[n] = jnp.concatenate([kblk, jnp.where(lane == n, 1.0, 0.0)], axis=1).astype(BF16)
            km_ref[n:n + 1, :] = jnp.mean(kblk, axis=0, keepdims=True)
            for hh in range(2):
                v_h = v_ref[hh * A_HEAD_DIM:(hh + 1) * A_HEAD_DIM, cols]
                vt_ref[n, hh] = jnp.concatenate([v_h, ones], axis=0).astype(BF16)

    q_t = q_ref[...] * LOG2_E
    feat = lax.broadcasted_iota(jnp.int32, q_t.shape, 0)
    km = km_ref[...]
    km_lane = lax.broadcasted_iota(jnp.int32, km.shape, 1)
    blk_id = lax.broadcasted_iota(jnp.int32, (nb, blk), 0)

    fold = lambda a, op: op(a.reshape(blk // 8, 8, blk), axis=0)

    qp = []
    for hh in range(2):
        lo = hh * A_HEAD_DIM
        q_h = jnp.where((feat >= lo) & (feat < lo + A_HEAD_DIM), q_t, 0.0)
        km_h = jnp.where((km_lane >= lo) & (km_lane < lo + A_HEAD_DIM), km, 0.0)
        k1, k2, _ = _split3(km_h)
        q1, q2, _ = _split3(q_h)
        gate = _dot(k1, q1) + _dot(k1, q2) + _dot(k2, q1)
        gate = jnp.where(blk_id < i, gate, -jnp.inf)
        attends = jnp.where(blk_id == i, 1.0, _top_k_mask(gate, MOBA_TOPK, 0))
        penalty = jnp.concatenate([jnp.where(attends > 0.0, 0.0, NEG_INF),
                                   jnp.zeros((128 - nb, blk), F32)], axis=0)
        qp.append(jnp.concatenate([q1, penalty.astype(BF16)], axis=0))

    per_trip = math.gcd(nb, MOBA_BLOCKS_PER_TRIP)
    trips = _div_pow2(i + per_trip, per_trip)

    pairs = lambda j: [(per_trip * j + u, hh) for u in range(per_trip) for hh in range(2)]

    def logits(j, m8):
        raw = [_dot(kb_ref[n], qp[hh]) for n, hh in pairs(j)]
        out = list(m8)
        for (n, hh), r in zip(pairs(j), raw):
            s = r + tt_ref[hh, jnp.maximum(i - n, 0)]
            s_ref[hh, n] = s
            out[hh] = jnp.maximum(out[hh], fold(s, jnp.max))
        return tuple(out)

    m8 = lax.fori_loop(0, trips, logits, (jnp.full((8, blk), -jnp.inf, F32),) * 2)
    m = [jnp.max(a, axis=0, keepdims=True) for a in m8]

    def values(j, acc):
        p = [jnp.exp2(s_ref[hh, n] - m[hh]).astype(BF16) for n, hh in pairs(j)]
        out = list(acc)
        for (n, hh), ph in zip(pairs(j), p):
            out[hh] = out[hh] + _dot(vt_ref[n, hh], ph)
        return tuple(out)

    acc = lax.fori_loop(0, trips, values, (jnp.zeros((A_HEAD_DIM + MOBA_ONES_ROWS, blk), F32),) * 2)
    out_t = jnp.concatenate([a[:A_HEAD_DIM] / a[A_HEAD_DIM:A_HEAD_DIM + 1] for a in acc], axis=0)
    o_ref[...] = out_t.T.astype(o_ref.dtype)


def _moba_prompt(q_t, k_t, v_t, w_pairs, batch, seq):
    nb = seq // MOBA_BLOCK
    pairs = A_HEADS // 2
    q_tile = pl.BlockSpec((128, MOBA_BLOCK), lambda p, b, i: (b * pairs + p, i))
    whole = pl.BlockSpec((128, seq), lambda p, b, i: (b * pairs + p, 0))
    return pl.pallas_call(
        functools.partial(_moba_prompt_kernel, nb=nb),
        out_shape=jax.ShapeDtypeStruct((batch * seq, A_WIDTH), BF16),
        grid=(pairs, batch, nb),
        in_specs=[q_tile, whole, whole,
                  pl.BlockSpec((1, 2, seq + MOBA_BLOCK), lambda p, b, i: (p, 0, 0))],
        out_specs=pl.BlockSpec((MOBA_BLOCK, 128), lambda p, b, i: (b * nb + i, p)),
        scratch_shapes=[pltpu.VMEM((nb, MOBA_BLOCK, 256), BF16),
                        pltpu.VMEM((nb, 2, A_HEAD_DIM + MOBA_ONES_ROWS, MOBA_BLOCK), BF16),
                        pltpu.VMEM((nb, 128), F32),
                        pltpu.VMEM((2, nb, MOBA_BLOCK, MOBA_BLOCK), F32),
                        pltpu.VMEM((2, nb, MOBA_BLOCK, MOBA_BLOCK), F32)],
        compiler_params=_params(("arbitrary", "arbitrary", "arbitrary")),
        name="moba_prompt",
    )(q_t, k_t, v_t, w_pairs)


def _moba_sample_kernel(*refs, nbk, dec_seq, bpg):
    for stage in _sample_stages(pl.program_id(0), pl.num_programs(0), *refs, nbk=nbk, dec_seq=dec_seq, bpg=bpg):
        stage()


def _sample_stages(b, n_seq, pt_ref, q_ref, kn_ref, vn_ref, sbt_ref, ob_ref, ck_ref, cv_ref, o_ref,
                   kbuf, vbuf, sem, m_sc, l_sc, acc_sc, g_sc, *, nbk, dec_seq, bpg):
    n_groups = nbk // bpg
    ppg = 2 * bpg
    rows = dec_seq * A_HEADS
    head_of_row = lax.broadcasted_iota(jnp.int32, (A_HEADS, A_WIDTH), 0)
    head_of_lane = _div_pow2(lax.broadcasted_iota(jnp.int32, (A_HEADS, A_WIDTH), 1), A_HEAD_DIM)
    head_mask = head_of_row == head_of_lane
    wide = lambda a: jnp.broadcast_to(a, (rows, 128))

    def group_copies(seq, g, slot):
        out = []
        for r in range(ppg):
            page = pt_ref[seq * (n_groups * ppg) + g * ppg + r]
            out.append(pltpu.make_async_copy(ck_ref.at[page], kbuf.at[slot, r], sem.at[slot, 0]))
            out.append(pltpu.make_async_copy(cv_ref.at[page], vbuf.at[slot, r], sem.at[slot, 1]))
        return out

    def queries():
        return jnp.concatenate(
            [jnp.where(head_mask, jnp.broadcast_to(q_ref[0, t:t + 1, :], head_mask.shape), 0.0)
             for t in range(dec_seq)], axis=0).astype(BF16)

    def group(g):
        slot = (b * n_groups + g) & 1
        if g == 0:
            @pl.when(b == 0)
            def _first_group():
                for cp in group_copies(0, 0, 0):
                    cp.start()
        if g + 1 < n_groups:
            for cp in group_copies(b, g + 1, 1 - slot):
                cp.start()
        else:
            @pl.when(b + 1 < n_seq)
            def _next_sequence():
                for cp in group_copies(b + 1, 0, 1 - slot):
                    cp.start()
        for cp in group_copies(b, g, slot):
            cp.wait()
        q16 = queries()
        raw = jnp.concatenate(
            [_dot(q16, jnp.concatenate([kbuf[slot, 2 * r], kbuf[slot, 2 * r + 1]], axis=1).astype(BF16))
             for r in range(bpg)], axis=1)
        width = bpg * MOBA_BLOCK
        s = raw + sbt_ref[:, g * width:(g + 1) * width]
        per_block = lambda a: [a[:, r * MOBA_BLOCK:(r + 1) * MOBA_BLOCK] for r in range(bpg)]
        m = [jnp.max(a, axis=1, keepdims=True) for a in per_block(s)]
        p = [jnp.exp(a - mr) for a, mr in zip(per_block(s), m)]
        for r in range(bpg):
            n = g * bpg + r
            vblk = jnp.concatenate([vbuf[slot, 2 * r], vbuf[slot, 2 * r + 1]], axis=1).astype(BF16)
            m_sc[n] = wide(m[r])
            l_sc[n] = wide(jnp.sum(p[r], axis=1, keepdims=True))
            acc_sc[n] = _dot_nt(p[r].astype(BF16), vblk)
            g_sc[n] = wide(jnp.sum(per_block(raw)[r], axis=1, keepdims=True) * (1.0 / MOBA_BLOCK))

    def merge():
        q16 = queries()
        gates = [g_sc[n] for n in range(nbk)]
        picked = [jnp.zeros((rows, 128), jnp.bool_)] * nbk
        for _ in range(min(MOBA_TOPK, nbk)):
            mx = functools.reduce(jnp.maximum, gates)
            first = functools.reduce(
                jnp.minimum, [jnp.where(g == mx, float(n), float(nbk)) for n, g in enumerate(gates)])
            for n in range(nbk):
                chosen = first == float(n)
                picked[n] = picked[n] | (chosen & (mx > -jnp.inf))
                gates[n] = jnp.where(chosen, -jnp.inf, gates[n])
        zeros = jnp.zeros((128 - kn_ref.shape[1], A_WIDTH), F32)
        k_own = jnp.concatenate([kn_ref[0], zeros], axis=0).astype(BF16)
        v_own = jnp.concatenate([vn_ref[0], zeros], axis=0).astype(BF16)
        s = _dot_nt(q16, k_own) + ob_ref[...]
        t_of_row = _div_pow2(lax.broadcasted_iota(jnp.int32, s.shape, 0), A_HEADS)
        s_idx = lax.broadcasted_iota(jnp.int32, s.shape, 1)
        s = jnp.where(s_idx <= t_of_row, s, NEG_INF)
        m_own = wide(jnp.max(s, axis=1, keepdims=True))
        p = jnp.exp(s - m_own)
        l_own = wide(jnp.sum(p, axis=1, keepdims=True))
        acc_own = _dot(p.astype(BF16), v_own)
        m_all = m_own
        for n in range(nbk):
            m_all = jnp.maximum(m_all, jnp.where(picked[n], m_sc[n], -jnp.inf))
        lanes4 = lambda a: jnp.concatenate([a] * (A_WIDTH // 128), axis=1)
        w = jnp.exp(m_own - m_all)
        l_all = w * l_own
        acc = lanes4(w) * acc_own
        for n in range(nbk):
            w = jnp.where(picked[n], jnp.exp(m_sc[n] - m_all), 0.0)
            l_all = l_all + w * l_sc[n]
            acc = acc + lanes4(w) * acc_sc[n]
        res = acc / lanes4(l_all)
        out = [jnp.sum(jnp.where(head_mask, res[t * A_HEADS:(t + 1) * A_HEADS, :], 0.0),
                       axis=0, keepdims=True) for t in range(dec_seq)]
        o_ref[0] = jnp.concatenate(out, axis=0).astype(o_ref.dtype)

    return [functools.partial(group, g) for g in range(n_groups)] + [merge]


def _moba_sample(q, k_new, v_new, cache_k, cache_v, page_table, sbt, ob, bpg, ffn=None):
    db, dec_seq, _ = q.shape
    n_pages = page_table.shape[1]
    nbk = n_pages * PAGE_SIZE // MOBA_BLOCK
    rows = dec_seq * A_HEADS
    assert MOBA_BLOCK == 2 * PAGE_SIZE and nbk % bpg == 0
    per_seq = lambda b, pt: (b, 0, 0)
    whole = lambda a: pl.BlockSpec(a.shape, lambda b, pt: (0,) * a.ndim, pipeline_mode=pl.Buffered(1))
    sample_specs = [pl.BlockSpec((1, dec_seq, A_WIDTH), per_seq),
                    pl.BlockSpec((1, k_new.shape[1], A_WIDTH), per_seq),
                    pl.BlockSpec((1, v_new.shape[1], A_WIDTH), per_seq),
                    whole(sbt), whole(ob),
                    pl.BlockSpec(memory_space=pl.ANY), pl.BlockSpec(memory_space=pl.ANY)]
    sample_args = (q, k_new, v_new, sbt, ob, cache_k, cache_v)
    attn_spec = pl.BlockSpec((1, dec_seq, A_WIDTH), per_seq)
    attn_shape = jax.ShapeDtypeStruct((db, dec_seq, A_WIDTH), BF16)
    scratch = [pltpu.VMEM((2, 2 * bpg, A_WIDTH, PAGE_SIZE), F32),
               pltpu.VMEM((2, 2 * bpg, A_WIDTH, PAGE_SIZE), F32),
               pltpu.SemaphoreType.DMA((2, 2)),
               pltpu.VMEM((nbk, rows, 128), F32),
               pltpu.VMEM((nbk, rows, 128), F32),
               pltpu.VMEM((nbk, rows, A_WIDTH), F32),
               pltpu.VMEM((nbk, rows, 128), F32)]
    static = dict(nbk=nbk, dec_seq=dec_seq, bpg=bpg)
    if ffn is None:
        return pl.pallas_call(
            functools.partial(_moba_sample_kernel, **static),
            out_shape=attn_shape,
            grid_spec=pltpu.PrefetchScalarGridSpec(num_scalar_prefetch=1, grid=(db,), in_specs=sample_specs,
                                                   out_specs=attn_spec, scratch_shapes=scratch),
            compiler_params=_params(("arbitrary",)),
            name="moba_sample",
        )(page_table.reshape(-1), *sample_args)
    x, gain, wg, wu, wd, tm = ffn
    assert x.shape[0] == db * tm
    row = pl.BlockSpec((tm, D_MODEL), lambda b, pt: (b, 0))
    return pl.pallas_call(
        functools.partial(_ffn_sample_kernel, **static),
        out_shape=(jax.ShapeDtypeStruct(x.shape, F32), attn_shape),
        grid_spec=pltpu.PrefetchScalarGridSpec(
            num_scalar_prefetch=1, grid=(db,),
            in_specs=[row] + [whole(a) for a in (gain, wg, wu, wd)] + sample_specs,
            out_specs=(row, attn_spec), scratch_shapes=scratch),
        compiler_params=pltpu.CompilerParams(dimension_semantics=("arbitrary",),
                                             vmem_limit_bytes=VMEM_LIMIT_FUSED),
        name="ffn_sample",
    )(page_table.reshape(-1), x, gain, wg, wu, wd, *sample_args)


def _hgrn_kernel(q_ref, k_ref, v_ref, g_ref, gate_ref, s0_ref, og_ref, tril_ref, pick_ref, rec_ref, sfin_ref,
                 st_ref, cum_ref, *, n_chunks):
    tb = pl.program_id(2)
    heads = range(q_ref.shape[1] // B_DIM)

    @pl.when(tb == 0)
    def _load_state():
        for h in heads:
            st_ref[h] = s0_ref[0, h].T

    head = lambda ref, h: ref.at[:, h * B_DIM:(h + 1) * B_DIM]
    used = n_chunks * HG_CHUNK
    work, terms = [], []
    for h in heads:
        o_blocks, off_blocks, term_blocks, v16_blocks, st_ref[h] = _hgrn_scores(
            head(q_ref, h), head(k_ref, h), head(v_ref, h), head(g_ref, h), tril_ref, st_ref[h], head(cum_ref, h),
            n_chunks)
        work.append((o_blocks, off_blocks, v16_blocks))
        terms += term_blocks
    half = (len(terms) + 1) // 2
    inside = [_dot(jnp.concatenate(part, axis=0), pick_ref[...]) for part in (terms[:half], terms[half:]) if part]
    inside = [a[at * used:(at + 1) * used] for a in inside for at in range(a.shape[0] // used)]
    for h in heads:
        n_blocks = len(work[h][0])
        _hgrn_finish(*work[h], inside[h * n_blocks:(h + 1) * n_blocks], head(gate_ref, h), head(og_ref, h),
                     head(rec_ref, h), n_chunks)

    @pl.when(tb == pl.num_programs(2) - 1)
    def _store_state():
        for h in heads:
            sfin_ref[0, h] = st_ref[h].T


def _hgrn_scores(q_ref, k_ref, v_ref, g_ref, tril_ref, st, cum_ref, n_chunks):
    n_blocks = q_ref.shape[0] // HG_BLOCK
    used = n_chunks * HG_CHUNK
    blocks = [slice(bi * HG_BLOCK, (bi + 1) * HG_BLOCK) for bi in range(n_blocks)]

    tril = tril_ref[...]
    g_all = jnp.concatenate([g_ref[rows, :] for rows in blocks], axis=1) * LOG2_E
    g1, g2, g3 = _split3(g_all)
    cum_all = _dot(tril, g1) + _dot(tril, g2) + _dot(tril, g3)

    o_blocks, off_blocks, term_blocks, v16_blocks = [], [], [], []
    for bi, rows in enumerate(blocks):
        q, k, v = q_ref[rows, :], k_ref[rows, :], v_ref[rows, :]
        cum = cum_all[:, bi * B_DIM:(bi + 1) * B_DIM]
        last = cum[HG_BLOCK - 1:HG_BLOCK, :]
        v16_blocks.append(v.astype(BF16))
        o_blocks.append(_dot_nt((q * jnp.exp2(cum)).astype(BF16), st.astype(BF16)))
        st = st * jnp.exp2(last) + _dot(v.T.astype(BF16), (k * jnp.exp2(last - cum)).astype(BF16))
        parts = [jnp.zeros((HG_CHUNK, B_DIM), F32)]
        for c in range(1, n_chunks):
            lo = c * HG_CHUNK
            r = cum[lo - 1:lo, :]
            k_c = (k[:lo] * jnp.exp2(r - cum[:lo])).astype(BF16)
            k_c = jnp.concatenate([k_c, jnp.zeros((HG_BLOCK - lo, B_DIM), BF16)], axis=0)
            q_c = (q[lo:lo + HG_CHUNK] * jnp.exp2(cum[lo:lo + HG_CHUNK] - r)).astype(BF16)
            parts.append(_dot_nt(q_c, k_c))
        off_blocks.append(jnp.concatenate(parts, axis=0))
        cum_ref[rows, :] = cum

    assert used == HG_BLOCK or n_blocks == 1
    n_rows = n_blocks * used
    chunk_rows = lambda ref, s: jnp.concatenate(
        [jnp.broadcast_to(ref[c * HG_CHUNK + s:c * HG_CHUNK + s + 1, :], (HG_CHUNK, B_DIM))
         for c in range(n_rows // HG_CHUNK)], axis=0)
    row_in_chunk = lax.broadcasted_iota(jnp.int32, (n_rows, B_DIM), 0) & (HG_CHUNK - 1)
    q_rows, cum_rows = q_ref[:n_rows, :], cum_ref[:n_rows, :]
    terms = []
    for s in range(HG_CHUNK):
        decay = jnp.exp2(jnp.where(row_in_chunk >= s, cum_rows - chunk_rows(cum_ref, s), -jnp.inf))
        terms.append((q_rows * decay * chunk_rows(k_ref, s)).astype(BF16))
    terms = jnp.concatenate(terms, axis=1)
    term_blocks = [terms[bi * used:(bi + 1) * used] for bi in range(n_blocks)]
    return o_blocks, off_blocks, term_blocks, v16_blocks, st


def _hgrn_finish(o_blocks, off_blocks, v16_blocks, inside_blocks, gate_ref, og_ref, rec_ref, n_chunks):
    used = n_chunks * HG_CHUNK
    row = lax.broadcasted_iota(jnp.int32, (used, B_DIM), 0)
    col = lax.broadcasted_iota(jnp.int32, (used, B_DIM), 1)
    same_chunk = _div_pow2(row, HG_CHUNK) == _div_pow2(col, HG_CHUNK)
    ones = jnp.ones((B_DIM, B_DIM), BF16)
    out = []
    for o, off, v16, inside in zip(o_blocks, off_blocks, v16_blocks, inside_blocks):
        score = off + jnp.where(same_chunk, inside, 0.0)
        intra = _dot(score.astype(BF16), v16)
        if used < HG_BLOCK:
            intra = jnp.concatenate([intra, jnp.zeros((HG_BLOCK - used, B_DIM), F32)], axis=0)
        out.append(o + intra)
    o = jnp.concatenate(out, axis=0)
    ms = _dot((o * o).astype(BF16), ones) * (1.0 / B_DIM)
    gate = gate_ref[...]
    rec_ref[...] = ((o * lax.rsqrt(ms + NORM_EPS) * og_ref[...]) * _silu(gate)).astype(rec_ref.dtype)


def _hgrn(qb, kb, ib, lf, gb, s0, out_gain, batch, seq, tile, n_chunks, heads_per_step):
    nt = seq // tile
    width = heads_per_step * B_DIM
    tok = pl.BlockSpec((tile, width), lambda b, h, t: (b * nt + t, h))
    state = pl.BlockSpec((1, heads_per_step, B_DIM, B_DIM), lambda b, h, t: (b, h, 0, 0))
    idx = jnp.arange(HG_BLOCK)
    tril = (idx[:, None] >= idx[None, :]).astype(BF16)
    pick = (jnp.repeat(jnp.arange(HG_CHUNK), B_DIM)[:, None] == (idx % HG_CHUNK)[None, :]).astype(BF16)
    return pl.pallas_call(
        functools.partial(_hgrn_kernel, n_chunks=n_chunks),
        out_shape=(jax.ShapeDtypeStruct((batch * seq, B_WIDTH), BF16),
                   jax.ShapeDtypeStruct(s0.shape, F32)),
        grid=(batch, B_HEADS // heads_per_step, nt),
        in_specs=[tok, tok, tok, tok, tok, state, pl.BlockSpec((1, width), lambda b, h, t: (0, h)),
                  _resident(tril.shape), _resident(pick.shape)],
        out_specs=(tok, state),
        scratch_shapes=[pltpu.VMEM((heads_per_step, B_DIM, B_DIM), F32),
                        pltpu.VMEM((tile, width), F32)],
        compiler_params=_params(("arbitrary", "arbitrary", "arbitrary")),
        name="hgrn",
    )(qb, kb, ib, lf, gb, s0, out_gain, tril, pick)


def kernel(x_prompt, x_sample, cache_k, cache_v, state_hgrn, page_table, rel_bias_table, lb_logits,
           ffn1_norm, ffn1_gate, ffn1_up, ffn1_down, mix_norm, w_in, q_norm, k_norm, hgrn_out_norm,
           w_out, ffn2_norm, ffn2_gate, ffn2_up, ffn2_down):
    batch, seq, _ = x_prompt.shape
    db, dec_seq, _ = x_sample.shape
    depth, n_phys = cache_k.shape[:2]
    assert depth == 1 and lb_logits.shape[0] == 2
    past = page_table.shape[1] * PAGE_SIZE
    assert seq % MOBA_BLOCK == 0 and past % MOBA_BLOCK == 0 and dec_seq <= HG_CHUNK
    n_p, n_s = batch * seq, db * dec_seq

    bf = lambda w: w[0].astype(BF16)
    gain = lambda g: g[0].reshape(1, -1)
    ffn1 = (gain(ffn1_norm), bf(ffn1_gate), bf(ffn1_up), bf(ffn1_down))
    ffn2 = (gain(ffn2_norm), bf(ffn2_gate), bf(ffn2_up), bf(ffn2_down))
    wo = bf(w_out)
    w_in16 = bf(w_in)
    per_lane = lambda g: jnp.broadcast_to(jnp.tile(g[0], A_HEADS)[:, None], (A_WIDTH, 128))
    proj_w = (gain(mix_norm), w_in16[:, :3 * GROUP_W].T, w_in16[:, 3 * GROUP_W:],
              per_lane(q_norm), per_lane(k_norm), lb_logits)
    out_gain = hgrn_out_norm[0].reshape(1, -1)
    w_bias, sbt, ob = _bias_tables(rel_bias_table.T, seq, past, dec_seq)

    ys = _ffn(x_sample.reshape(n_s, D_MODEL), *ffn1, n_s)
    q_t, k_t, v_t, qb_s, kb_s, ib_s, lf_s, gb_s = _proj(ys, *proj_w, n_s, n_s)
    q_s, k_s, v_s = (a.T.reshape(db, dec_seq, A_WIDTH) for a in (q_t, k_t, v_t))
    pad_rows = lambda a, r: jnp.pad(a.reshape(db, dec_seq, -1), ((0, 0), (0, r - dec_seq), (0, 0)))
    pages = lambda c: c[0].transpose(0, 2, 3, 1).reshape(n_phys, A_WIDTH, PAGE_SIZE)
    sample_attn = functools.partial(
        _moba_sample, q_s, pad_rows(k_s, 8), pad_rows(v_s, 8), pages(cache_k), pages(cache_v), page_table, sbt, ob,
        bpg=math.gcd(past // MOBA_BLOCK, SAMPLE_BLOCKS_PER_STEP))

    tm = 512
    x_p = x_prompt.reshape(n_p, D_MODEL)
    if n_p // tm == db:
        y, attn_s = sample_attn(ffn=(x_p, *ffn1, tm))
    else:
        y, attn_s = _ffn(x_p, *ffn1, tm), sample_attn()
    q_t, kp_t, vp_t, qb, kb, ib, lf, gb = _proj(y, *proj_w, seq, tm)
    attn = _moba_prompt(q_t, kp_t, vp_t, w_bias.reshape(A_HEADS // 2, 2, -1), batch, seq)
    s0 = jnp.zeros((batch, B_HEADS, B_DIM, B_DIM), F32)
    rec, s_p = _hgrn(qb, kb, ib, lf, gb, s0, out_gain, batch, seq, 512, HG_BLOCK // HG_CHUNK, 1)
    y_p = _ffn(y, *ffn2, tm, mix=(attn, rec, wo))
    heads_p = lambda a: a.reshape(batch, A_HEADS, A_HEAD_DIM, seq).transpose(0, 3, 1, 2)[None]

    blocked = lambda a: pad_rows(a, HG_BLOCK).reshape(db * HG_BLOCK, -1)
    rec, s_s = _hgrn(blocked(qb_s), blocked(kb_s), blocked(ib_s), blocked(lf_s), blocked(gb_s), state_hgrn[0],
                     out_gain, db, HG_BLOCK, HG_BLOCK, 1, B_HEADS)
    rec = rec.reshape(db, HG_BLOCK, B_WIDTH)[:, :dec_seq].reshape(n_s, B_WIDTH)
    y_s = _ffn(ys, *ffn2, n_s, mix=(attn_s.reshape(n_s, A_WIDTH), rec, wo))
    heads_s = lambda a: a.reshape(1, db, dec_seq, A_HEADS, A_HEAD_DIM)

    return (y_p.reshape(batch, seq, D_MODEL), y_s.reshape(db, dec_seq, D_MODEL),
            heads_p(kp_t), heads_p(vp_t), s_p[None],
            heads_s(k_s), heads_s(v_s), s_s[None])
```

```python
import functools
import math

import jax
import jax.numpy as jnp
from jax import lax
from jax.experimental import pallas as pl
from jax.experimental.pallas import tpu as pltpu

F32 = jnp.float32
BF16 = jnp.bfloat16

D_MODEL = 1024
A_HEADS = 8
A_HEAD_DIM = 64
A_WIDTH = A_HEADS * A_HEAD_DIM
B_HEADS = 4
B_DIM = 128
B_WIDTH = B_HEADS * B_DIM
GROUP_W = 512
N_GROUPS = 7
D_FF = 2816
MOBA_BLOCK = 256
MOBA_TOPK = 3
PAGE_SIZE = 128
N_BUCKETS = 32
MAX_EXACT = N_BUCKETS // 2
REL_MAX_DIST = 4096
HG_CHUNK = 16
HG_BLOCK = 128
NORM_EPS = 1e-6
NEG_INF = -1e30
ATTN_SCALE = A_HEAD_DIM ** -0.5
LOG2_E = math.log2(math.e)

FF_CHUNK = 256
SAMPLE_BLOCKS_PER_STEP = 8
MOBA_BLOCKS_PER_TRIP = 4
HGRN_TILE = 2048
MOBA_ONES_ROWS = 16
VMEM_LIMIT = 48 * 1024 * 1024
VMEM_LIMIT_FUSED = 56 * 1024 * 1024


def _resident(shape):
    nd = len(shape)
    return pl.BlockSpec(shape, lambda *_: (0,) * nd, pipeline_mode=pl.Buffered(1))


def _params(semantics):
    return pltpu.CompilerParams(dimension_semantics=semantics, vmem_limit_bytes=VMEM_LIMIT)


def _dot(a, b):
    return jnp.dot(a, b, preferred_element_type=F32)


def _dot_nt(a, b):
    return lax.dot_general(a, b, (((1,), (1,)), ((), ())), preferred_element_type=F32)


def _split3(a):
    a1 = a.astype(BF16)
    r1 = a - a1.astype(F32)
    a2 = r1.astype(BF16)
    a3 = (r1 - a2.astype(F32)).astype(BF16)
    return a1, a2, a3


def _div_pow2(x, d):
    return lax.shift_right_logical(x, int(math.log2(d)))


def _rms_rows(x, gain):
    ms = jnp.mean(x * x, axis=-1, keepdims=True)
    return x * lax.rsqrt(ms + NORM_EPS) * gain


def _silu(x):
    return x * jax.nn.sigmoid(x)


def _ffn_chunks(h, wg_ref, wu_ref, wd_ref, acc, chunks):
    for j in chunks:
        cols = slice(j * FF_CHUNK, (j + 1) * FF_CHUNK)
        g = _dot(h, wg_ref[:, cols])
        u = _dot(h, wu_ref[:, cols])
        acc = acc + _dot((_silu(g) * u).astype(BF16), wd_ref[cols, :])
    return acc


def _ffn_kernel(*refs, with_mix):
    if with_mix:
        x_ref, attn_ref, rec_ref, wo_ref, gain_ref, wg_ref, wu_ref, wd_ref, o_ref = refs
        y = (x_ref[...] + _dot(attn_ref[...], wo_ref[:A_WIDTH, :])
             + _dot(rec_ref[...], wo_ref[A_WIDTH:, :]))
    else:
        x_ref, gain_ref, wg_ref, wu_ref, wd_ref, o_ref = refs
        y = x_ref[...]
    h = _rms_rows(y, gain_ref[...]).astype(BF16)
    acc = _ffn_chunks(h, wg_ref, wu_ref, wd_ref, jnp.zeros(y.shape, F32), range(D_FF // FF_CHUNK))
    o_ref[...] = y + 0.5 * acc


def _ffn_sample_kernel(pt_ref, x_ref, gain_ref, wg_ref, wu_ref, wd_ref, q_ref, kn_ref, vn_ref, sbt_ref, ob_ref,
                       ck_ref, cv_ref, y_ref, attn_ref, *scratch, nbk, dec_seq, bpg):
    stages = _sample_stages(pl.program_id(0), pl.num_programs(0), pt_ref, q_ref, kn_ref, vn_ref, sbt_ref, ob_ref,
                            ck_ref, cv_ref, attn_ref, *scratch, nbk=nbk, dec_seq=dec_seq, bpg=bpg)
    y = x_ref[...]
    h = _rms_rows(y, gain_ref[...]).astype(BF16)
    acc = jnp.zeros(y.shape, F32)
    n_chunks = D_FF // FF_CHUNK
    cuts = [k * n_chunks // len(stages) for k in range(len(stages) + 1)]
    for k, stage in enumerate(stages):
        stage()
        acc = _ffn_chunks(h, wg_ref, wu_ref, wd_ref, acc, range(cuts[k], cuts[k + 1]))
    y_ref[...] = y + 0.5 * acc


def _ffn(x, gain, wg, wu, wd, tm, mix=None):
    n = x.shape[0]
    row = lambda w: pl.BlockSpec((tm, w), lambda i: (i, 0))
    args, specs = [x], [row(D_MODEL)]
    if mix is not None:
        attn, rec, wo = mix
        args += [attn, rec, wo]
        specs += [row(A_WIDTH), row(B_WIDTH), _resident(wo.shape)]
    args += [gain, wg, wu, wd]
    specs += [_resident(gain.shape), _resident(wg.shape), _resident(wu.shape), _resident(wd.shape)]
    return pl.pallas_call(
        functools.partial(_ffn_kernel, with_mix=mix is not None),
        out_shape=jax.ShapeDtypeStruct((n, D_MODEL), F32),
        grid=(n // tm,),
        in_specs=specs,
        out_specs=row(D_MODEL),
        compiler_params=_params(("parallel",)),
        name="ffn_mix" if mix is not None else "ffn",
    )(*args)


def _proj_kernel(y_ref, gain_ref, wat_ref, wb_ref, qg_ref, kg_ref, lbl_ref,
                 q_ref, k_ref, v_ref, qb_ref, kb_ref, ib_ref, lf_ref, gb_ref):
    h = _rms_rows(y_ref[...], gain_ref[...]).astype(BF16)
    tm = h.shape[0]

    def group_t(i):
        return _dot_nt(wat_ref[i * GROUP_W:(i + 1) * GROUP_W, :], h)

    def group(i):
        return _dot(h, wb_ref[:, i * GROUP_W:(i + 1) * GROUP_W])

    def head_norm(z, g):
        out = []
        for hd in range(A_HEADS):
            zh = z[hd * A_HEAD_DIM:(hd + 1) * A_HEAD_DIM, :]
            out.append(zh * lax.rsqrt(jnp.mean(zh * zh, axis=0, keepdims=True) + NORM_EPS))
        return jnp.concatenate(out, axis=0) * jnp.concatenate([g] * pl.cdiv(tm, 128), axis=1)[:, :tm]

    q_ref[...] = head_norm(group_t(0), qg_ref[...]) * ATTN_SCALE
    k_ref[...] = head_norm(group_t(1), kg_ref[...])
    v_ref[...] = group_t(2)
    qb_ref[...] = _silu(group(0))
    lbl = lbl_ref[...]
    e = jnp.exp(lbl - jnp.max(lbl, axis=0, keepdims=True))
    lb = e[0:1, :] / jnp.sum(e, axis=0, keepdims=True)
    forget = lb + (1.0 - lb) * jax.nn.sigmoid(group(1))
    lf_ref[...] = jnp.log(forget)
    kb_ref[...] = 1.0 - forget
    ib_ref[...] = group(2)
    gb_ref[...] = group(3)


def _proj(y, gain, wat, wb, qg, kg, lbl, seq, tm):
    n = y.shape[0]
    nt = seq // tm
    row = lambda w: pl.BlockSpec((tm, w), lambda i: (i, 0))
    col = pl.BlockSpec((GROUP_W, tm), lambda i: (i // nt, i % nt))
    out_t = jax.ShapeDtypeStruct((n // seq * GROUP_W, seq), F32)
    out = jax.ShapeDtypeStruct((n, GROUP_W), F32)
    return pl.pallas_call(
        _proj_kernel,
        out_shape=(out_t,) * 3 + (out,) * 5,
        grid=(n // tm,),
        in_specs=[row(D_MODEL)] + [_resident(a.shape) for a in (gain, wat, wb, qg, kg, lbl)],
        out_specs=(col,) * 3 + (row(GROUP_W),) * 5,
        compiler_params=_params(("parallel",)),
        name="proj",
    )(y, gain, wat, wb, qg, kg, lbl)


def _rel_bucket(n):
    nf = jnp.maximum(n, 1).astype(F32)
    large = MAX_EXACT + (jnp.log(nf / MAX_EXACT) / math.log(REL_MAX_DIST / MAX_EXACT)
                         * (N_BUCKETS - MAX_EXACT)).astype(jnp.int32)
    large = jnp.minimum(large, N_BUCKETS - 1)
    return jnp.where(n < MAX_EXACT, n, large)


def _bias_kernel(tab_ref, w_ref, sbt_ref, ob_ref, *, past, dec_seq):
    tab = tab_ref[...]
    tab_rows = jnp.concatenate([tab] * dec_seq, axis=0)

    def lookup(bucket, table):
        out = jnp.zeros(bucket.shape, F32)
        for b in range(N_BUCKETS):
            out = jnp.where(bucket == b, table[:, b:b + 1], out)
        return out

    c = lax.broadcasted_iota(jnp.int32, w_ref.shape, 1)
    w_ref[...] = lookup(_rel_bucket(jnp.maximum(c - MOBA_BLOCK, 0)), tab)
    t = _div_pow2(lax.broadcasted_iota(jnp.int32, sbt_ref.shape, 0), A_HEADS)
    kpos = lax.broadcasted_iota(jnp.int32, sbt_ref.shape, 1)
    sbt_ref[...] = lookup(_rel_bucket(jnp.maximum(past + t - kpos, 0)), tab_rows)
    t = _div_pow2(lax.broadcasted_iota(jnp.int32, ob_ref.shape, 0), A_HEADS)
    s = lax.broadcasted_iota(jnp.int32, ob_ref.shape, 1)
    ob_ref[...] = lookup(_rel_bucket(jnp.maximum(t - s, 0)), tab_rows)


def _bias_tables(tab_t, seq, past, dec_seq):
    rows = dec_seq * A_HEADS
    return pl.pallas_call(
        functools.partial(_bias_kernel, past=past, dec_seq=dec_seq),
        out_shape=(jax.ShapeDtypeStruct((A_HEADS, seq + MOBA_BLOCK), F32),
                   jax.ShapeDtypeStruct((rows, past), F32),
                   jax.ShapeDtypeStruct((rows, 128), F32)),
        name="rel_bias",
    )(tab_t)


def _top_k_mask(gate, k, axis):
    index = lax.broadcasted_iota(jnp.int32, gate.shape, axis).astype(F32)
    sel = jnp.zeros(gate.shape, F32)
    for _ in range(k):
        mx = jnp.max(gate, axis=axis, keepdims=True)
        first = jnp.min(jnp.where(gate == mx, index, float(gate.shape[axis])), axis=axis, keepdims=True)
        chosen = index == first
        sel = jnp.where(chosen & (mx > -jnp.inf), 1.0, sel)
        gate = jnp.where(chosen, -jnp.inf, gate)
    return sel


def _moba_prompt_kernel(q_ref, k_ref, v_ref, w_ref, o_ref, kb_ref, vt_ref, km_ref, tt_ref, s_ref, *, nb):
    b = pl.program_id(1)
    i = pl.program_id(2)
    blk = MOBA_BLOCK

    key_j = lax.broadcasted_iota(jnp.int32, (blk, blk), 0)
    qry_c = lax.broadcasted_iota(jnp.int32, (blk, blk), 1)

    @pl.when((b == 0) & (i == 0))
    def _build_bias_tiles():
        for hh in range(2):
            for d in range(nb):
                win = jnp.broadcast_to(w_ref[0, hh:hh + 1, d * blk:(d + 2) * blk], (blk, 2 * blk))
                tile = pltpu.roll(win, 0, 1, stride=1, stride_axis=0)[:, blk:] * LOG2_E
                tt_ref[hh, d] = jnp.where(qry_c >= key_j, tile, NEG_INF) if d == 0 else tile

    @pl.when(i == 0)
    def _prepare_sequence():
        lane = lax.broadcasted_iota(jnp.int32, (blk, 128), 1)
        ones = jnp.ones((MOBA_ONES_ROWS, blk), F32)
        for n in range(nb):
            cols = slice(n * blk, (n + 1) * blk)
            kblk = k_ref[:, cols].T
            kb_ref[n] = jnp.concatenate([kblk, jnp.where(lane == n, 1.0, 0.0)], axis=1).astype(BF16)
            km_ref[n:n + 1, :] = jnp.mean(kblk, axis=0, keepdims=True)
            for hh in range(2):
                v_h = v_ref[hh * A_HEAD_DIM:(hh + 1) * A_HEAD_DIM, cols]
                vt_ref[n, hh] = jnp.concatenate([v_h, ones], axis=0).astype(BF16)

    q_t = q_ref[...] * LOG2_E
    feat = lax.broadcasted_iota(jnp.int32, q_t.shape, 0)
    km = km_ref[...]
    km_lane = lax.broadcasted_iota(jnp.int32, km.shape, 1)
    blk_id = lax.broadcasted_iota(jnp.int32, (nb, blk), 0)

    fold = lambda a, op: op(a.reshape(blk // 8, 8, blk), axis=0)

    qp = []
    for hh in range(2):
        lo = hh * A_HEAD_DIM
        q_h = jnp.where((feat >= lo) & (feat < lo + A_HEAD_DIM), q_t, 0.0)
        km_h = jnp.where((km_lane >= lo) & (km_lane < lo + A_HEAD_DIM), km, 0.0)
        k1, k2, _ = _split3(km_h)
        q1, q2, _ = _split3(q_h)
        gate = _dot(k1, q1) + _dot(k1, q2) + _dot(k2, q1)
        gate = jnp.where(blk_id < i, gate, -jnp.inf)
        attends = jnp.where(blk_id == i, 1.0, _top_k_mask(gate, MOBA_TOPK, 0))
        penalty = jnp.concatenate([jnp.where(attends > 0.0, 0.0, NEG_INF),
                                   jnp.zeros((128 - nb, blk), F32)], axis=0)
        qp.append(jnp.concatenate([q1, penalty.astype(BF16)], axis=0))

    per_trip = math.gcd(nb, MOBA_BLOCKS_PER_TRIP)
    full_trips = _div_pow2(i + 1, per_trip)
    left_over = (i + 1) & (per_trip - 1)

    def walk(body, carry):
        carry = lax.fori_loop(0, full_trips, lambda j, c: body(per_trip * j, per_trip, c), carry)
        for left in range(1, per_trip):
            count = left if 2 * left <= per_trip else per_trip
            carry = lax.cond(left_over == left,
                             lambda c, count=count: body(per_trip * full_trips, count, c), lambda c: c, carry)
        return carry

    pairs = lambda first, count: [(first + u, hh) for u in range(count) for hh in range(2)]

    def logits(first, count, m8):
        raw = [_dot(kb_ref[n], qp[hh]) for n, hh in pairs(first, count)]
        out = list(m8)
        for (n, hh), r in zip(pairs(first, count), raw):
            s = r + tt_ref[hh, jnp.maximum(i - n, 0)]
            s_ref[hh, n] = s
            out[hh] = jnp.maximum(out[hh], fold(s, jnp.max))
        return tuple(out)

    m8 = walk(logits, (jnp.full((8, blk), -jnp.inf, F32),) * 2)
    m = [jnp.max(a, axis=0, keepdims=True) for a in m8]

    def values(first, count, acc):
        p = [jnp.exp2(s_ref[hh, n] - m[hh]).astype(BF16) for n, hh in pairs(first, count)]
        out = list(acc)
        for (n, hh), ph in zip(pairs(first, count), p):
            out[hh] = out[hh] + _dot(vt_ref[n, hh], ph)
        return tuple(out)

    acc = walk(values, (jnp.zeros((A_HEAD_DIM + MOBA_ONES_ROWS, blk), F32),) * 2)
    out_t = jnp.concatenate([a[:A_HEAD_DIM] / a[A_HEAD_DIM:A_HEAD_DIM + 1] for a in acc], axis=0)
    o_ref[...] = out_t.T.astype(o_ref.dtype)


def _moba_prompt(q_t, k_t, v_t, w_pairs, batch, seq):
    nb = seq // MOBA_BLOCK
    pairs = A_HEADS // 2
    q_tile = pl.BlockSpec((128, MOBA_BLOCK), lambda p, b, i: (b * pairs + p, i))
    whole = pl.BlockSpec((128, seq), lambda p, b, i: (b * pairs + p, 0))
    return pl.pallas_call(
        functools.partial(_moba_prompt_kernel, nb=nb),
        out_shape=jax.ShapeDtypeStruct((batch * seq, A_WIDTH), BF16),
        grid=(pairs, batch, nb),
        in_specs=[q_tile, whole, whole,
                  pl.BlockSpec((1, 2, seq + MOBA_BLOCK), lambda p, b, i: (p, 0, 0))],
        out_specs=pl.BlockSpec((MOBA_BLOCK, 128), lambda p, b, i: (b * nb + i, p)),
        scratch_shapes=[pltpu.VMEM((nb, MOBA_BLOCK, 256), BF16),
                        pltpu.VMEM((nb, 2, A_HEAD_DIM + MOBA_ONES_ROWS, MOBA_BLOCK), BF16),
                        pltpu.VMEM((nb, 128), F32),
                        pltpu.VMEM((2, nb, MOBA_BLOCK, MOBA_BLOCK), F32),
                        pltpu.VMEM((2, nb, MOBA_BLOCK, MOBA_BLOCK), F32)],
        compiler_params=_params(("arbitrary", "arbitrary", "arbitrary")),
        name="moba_prompt",
    )(q_t, k_t, v_t, w_pairs)


def _moba_sample_kernel(*refs, nbk, dec_seq, bpg):
    for stage in _sample_stages(pl.program_id(0), pl.num_programs(0), *refs, nbk=nbk, dec_seq=dec_seq, bpg=bpg):
        stage()


def _sample_stages(b, n_seq, pt_ref, q_ref, kn_ref, vn_ref, sbt_ref, ob_ref, ck_ref, cv_ref, o_ref,
                   kbuf, vbuf, sem, m_sc, l_sc, acc_sc, g_sc, *, nbk, dec_seq, bpg):
    n_groups = nbk // bpg
    ppg = 2 * bpg
    rows = dec_seq * A_HEADS
    head_of_row = lax.broadcasted_iota(jnp.int32, (A_HEADS, A_WIDTH), 0)
    head_of_lane = _div_pow2(lax.broadcasted_iota(jnp.int32, (A_HEADS, A_WIDTH), 1), A_HEAD_DIM)
    head_mask = head_of_row == head_of_lane
    wide = lambda a: jnp.broadcast_to(a, (rows, 128))

    def group_copies(seq, g, slot):
        out = []
        for r in range(ppg):
            page = pt_ref[seq * (n_groups * ppg) + g * ppg + r]
            out.append(pltpu.make_async_copy(ck_ref.at[page], kbuf.at[slot, r], sem.at[slot, 0]))
            out.append(pltpu.make_async_copy(cv_ref.at[page], vbuf.at[slot, r], sem.at[slot, 1]))
        return out

    def queries():
        return jnp.concatenate(
            [jnp.where(head_mask, jnp.broadcast_to(q_ref[0, t:t + 1, :], head_mask.shape), 0.0)
             for t in range(dec_seq)], axis=0).astype(BF16)

    live = {}

    def group_keys(g):
        slot = (b * n_groups + g) & 1
        if g == 0:
            @pl.when(b == 0)
            def _first_group():
                for cp in group_copies(0, 0, 0):
                    cp.start()
        if g + 1 < n_groups:
            for cp in group_copies(b, g + 1, 1 - slot):
                cp.start()
        else:
            @pl.when(b + 1 < n_seq)
            def _next_sequence():
                for cp in group_copies(b + 1, 0, 1 - slot):
                    cp.start()
        for cp in group_copies(b, g, slot):
            cp.wait()
        q16 = queries()
        live["raw"] = jnp.concatenate(
            [_dot(q16, jnp.concatenate([kbuf[slot, 2 * r], kbuf[slot, 2 * r + 1]], axis=1).astype(BF16))
             for r in range(bpg)], axis=1)

    per_block = lambda a: [a[:, r * MOBA_BLOCK:(r + 1) * MOBA_BLOCK] for r in range(bpg)]

    def group_softmax(g):
        width = bpg * MOBA_BLOCK
        s = per_block(live["raw"] + sbt_ref[:, g * width:(g + 1) * width])
        live["m"] = [jnp.max(a, axis=1, keepdims=True) for a in s]
        live["p"] = [jnp.exp(a - mr) for a, mr in zip(s, live["m"])]

    def group_values(g):
        slot = (b * n_groups + g) & 1
        raw, m, p = per_block(live.pop("raw")), live.pop("m"), live.pop("p")
        for r in range(bpg):
            n = g * bpg + r
            vblk = jnp.concatenate([vbuf[slot, 2 * r], vbuf[slot, 2 * r + 1]], axis=1).astype(BF16)
            m_sc[n] = wide(m[r])
            l_sc[n] = wide(jnp.sum(p[r], axis=1, keepdims=True))
            acc_sc[n] = _dot_nt(p[r].astype(BF16), vblk)
            g_sc[n] = wide(jnp.sum(raw[r], axis=1, keepdims=True) * (1.0 / MOBA_BLOCK))

    def merge():
        q16 = queries()
        gates = [g_sc[n] for n in range(nbk)]
        picked = [jnp.zeros((rows, 128), jnp.bool_)] * nbk
        for _ in range(min(MOBA_TOPK, nbk)):
            mx = functools.reduce(jnp.maximum, gates)
            first = functools.reduce(
                jnp.minimum, [jnp.where(g == mx, float(n), float(nbk)) for n, g in enumerate(gates)])
            for n in range(nbk):
                chosen = first == float(n)
                picked[n] = picked[n] | (chosen & (mx > -jnp.inf))
                gates[n] = jnp.where(chosen, -jnp.inf, gates[n])
        zeros = jnp.zeros((128 - kn_ref.shape[1], A_WIDTH), F32)
        k_own = jnp.concatenate([kn_ref[0], zeros], axis=0).astype(BF16)
        v_own = jnp.concatenate([vn_ref[0], zeros], axis=0).astype(BF16)
        s = _dot_nt(q16, k_own) + ob_ref[...]
        t_of_row = _div_pow2(lax.broadcasted_iota(jnp.int32, s.shape, 0), A_HEADS)
        s_idx = lax.broadcasted_iota(jnp.int32, s.shape, 1)
        s = jnp.where(s_idx <= t_of_row, s, NEG_INF)
        m_own = wide(jnp.max(s, axis=1, keepdims=True))
        p = jnp.exp(s - m_own)
        l_own = wide(jnp.sum(p, axis=1, keepdims=True))
        acc_own = _dot(p.astype(BF16), v_own)
        m_all = m_own
        for n in range(nbk):
            m_all = jnp.maximum(m_all, jnp.where(picked[n], m_sc[n], -jnp.inf))
        lanes4 = lambda a: jnp.concatenate([a] * (A_WIDTH // 128), axis=1)
        w = jnp.exp(m_own - m_all)
        l_all = w * l_own
        acc = lanes4(w) * acc_own
        for n in range(nbk):
            w = jnp.where(picked[n], jnp.exp(m_sc[n] - m_all), 0.0)
            l_all = l_all + w * l_sc[n]
            acc = acc + lanes4(w) * acc_sc[n]
        res = acc / lanes4(l_all)
        out = [jnp.sum(jnp.where(head_mask, res[t * A_HEADS:(t + 1) * A_HEADS, :], 0.0),
                       axis=0, keepdims=True) for t in range(dec_seq)]
        o_ref[0] = jnp.concatenate(out, axis=0).astype(o_ref.dtype)

    return [functools.partial(stage, g) for g in range(n_groups)
            for stage in (group_keys, group_softmax, group_values)] + [merge]


def _moba_sample(q, k_new, v_new, cache_k, cache_v, page_table, sbt, ob, bpg, ffn=None):
    db, dec_seq, _ = q.shape
    n_pages = page_table.shape[1]
    nbk = n_pages * PAGE_SIZE // MOBA_BLOCK
    rows = dec_seq * A_HEADS
    assert MOBA_BLOCK == 2 * PAGE_SIZE and nbk % bpg == 0
    per_seq = lambda b, pt: (b, 0, 0)
    whole = lambda a: pl.BlockSpec(a.shape, lambda b, pt: (0,) * a.ndim, pipeline_mode=pl.Buffered(1))
    sample_specs = [pl.BlockSpec((1, dec_seq, A_WIDTH), per_seq),
                    pl.BlockSpec((1, k_new.shape[1], A_WIDTH), per_seq),
                    pl.BlockSpec((1, v_new.shape[1], A_WIDTH), per_seq),
                    whole(sbt), whole(ob),
                    pl.BlockSpec(memory_space=pl.ANY), pl.BlockSpec(memory_space=pl.ANY)]
    sample_args = (q, k_new, v_new, sbt, ob, cache_k, cache_v)
    attn_spec = pl.BlockSpec((1, dec_seq, A_WIDTH), per_seq)
    attn_shape = jax.ShapeDtypeStruct((db, dec_seq, A_WIDTH), BF16)
    scratch = [pltpu.VMEM((2, 2 * bpg, A_WIDTH, PAGE_SIZE), F32),
               pltpu.VMEM((2, 2 * bpg, A_WIDTH, PAGE_SIZE), F32),
               pltpu.SemaphoreType.DMA((2, 2)),
               pltpu.VMEM((nbk, rows, 128), F32),
               pltpu.VMEM((nbk, rows, 128), F32),
               pltpu.VMEM((nbk, rows, A_WIDTH), F32),
               pltpu.VMEM((nbk, rows, 128), F32)]
    static = dict(nbk=nbk, dec_seq=dec_seq, bpg=bpg)
    if ffn is None:
        return pl.pallas_call(
            functools.partial(_moba_sample_kernel, **static),
            out_shape=attn_shape,
            grid_spec=pltpu.PrefetchScalarGridSpec(num_scalar_prefetch=1, grid=(db,), in_specs=sample_specs,
                                                   out_specs=attn_spec, scratch_shapes=scratch),
            compiler_params=_params(("arbitrary",)),
            name="moba_sample",
        )(page_table.reshape(-1), *sample_args)
    x, gain, wg, wu, wd, tm = ffn
    assert x.shape[0] == db * tm
    row = pl.BlockSpec((tm, D_MODEL), lambda b, pt: (b, 0))
    return pl.pallas_call(
        functools.partial(_ffn_sample_kernel, **static),
        out_shape=(jax.ShapeDtypeStruct(x.shape, F32), attn_shape),
        grid_spec=pltpu.PrefetchScalarGridSpec(
            num_scalar_prefetch=1, grid=(db,),
            in_specs=[row] + [whole(a) for a in (gain, wg, wu, wd)] + sample_specs,
            out_specs=(row, attn_spec), scratch_shapes=scratch),
        compiler_params=pltpu.CompilerParams(dimension_semantics=("arbitrary",),
                                             vmem_limit_bytes=VMEM_LIMIT_FUSED),
        name="ffn_sample",
    )(page_table.reshape(-1), x, gain, wg, wu, wd, *sample_args)


def _hgrn_kernel(q_ref, k_ref, v_ref, g_ref, gate_ref, s0_ref, og_ref, tril_ref, pick_ref, rec_ref, sfin_ref,
                 st_ref, cum_ref, *, n_chunks):
    tb = pl.program_id(2)
    heads = range(q_ref.shape[1] // B_DIM)

    @pl.when(tb == 0)
    def _load_state():
        for h in heads:
            st_ref[h] = s0_ref[0, h].T

    head = lambda ref, h: ref.at[:, h * B_DIM:(h + 1) * B_DIM]
    used = n_chunks * HG_CHUNK
    work, terms = [], []
    for h in heads:
        o_blocks, off_blocks, term_blocks, v16_blocks, st_ref[h] = _hgrn_scores(
            head(q_ref, h), head(k_ref, h), head(v_ref, h), head(g_ref, h), tril_ref, st_ref[h], head(cum_ref, h),
            n_chunks)
        work.append((o_blocks, off_blocks, v16_blocks))
        terms += term_blocks
    half = (len(terms) + 1) // 2
    inside = [_dot(jnp.concatenate(part, axis=0), pick_ref[...]) for part in (terms[:half], terms[half:]) if part]
    inside = [a[at * used:(at + 1) * used] for a in inside for at in range(a.shape[0] // used)]
    for h in heads:
        n_blocks = len(work[h][0])
        _hgrn_finish(*work[h], inside[h * n_blocks:(h + 1) * n_blocks], head(gate_ref, h), head(og_ref, h),
                     head(rec_ref, h), n_chunks)

    @pl.when(tb == pl.num_programs(2) - 1)
    def _store_state():
        for h in heads:
            sfin_ref[0, h] = st_ref[h].T


def _hgrn_scores(q_ref, k_ref, v_ref, g_ref, tril_ref, st, cum_ref, n_chunks):
    n_blocks = q_ref.shape[0] // HG_BLOCK
    used = n_chunks * HG_CHUNK
    blocks = [slice(bi * HG_BLOCK, (bi + 1) * HG_BLOCK) for bi in range(n_blocks)]

    tril = tril_ref[...]
    g_all = jnp.concatenate([g_ref[rows, :] for rows in blocks], axis=1) * LOG2_E
    g1, g2, g3 = _split3(g_all)
    cum_all = _dot(tril, g1) + _dot(tril, g2) + _dot(tril, g3)

    o_blocks, off_blocks, term_blocks, v16_blocks = [], [], [], []
    for bi, rows in enumerate(blocks):
        q, k, v = q_ref[rows, :], k_ref[rows, :], v_ref[rows, :]
        cum = cum_all[:, bi * B_DIM:(bi + 1) * B_DIM]
        last = cum[HG_BLOCK - 1:HG_BLOCK, :]
        v16_blocks.append(v.astype(BF16))
        o_blocks.append(_dot_nt((q * jnp.exp2(cum)).astype(BF16), st.astype(BF16)))
        st = st * jnp.exp2(last) + _dot(v.T.astype(BF16), (k * jnp.exp2(last - cum)).astype(BF16))
        parts = [jnp.zeros((HG_CHUNK, B_DIM), F32)]
        for c in range(1, n_chunks):
            lo = c * HG_CHUNK
            r = cum[lo - 1:lo, :]
            k_c = (k[:lo] * jnp.exp2(r - cum[:lo])).astype(BF16)
            k_c = jnp.concatenate([k_c, jnp.zeros((HG_BLOCK - lo, B_DIM), BF16)], axis=0)
            q_c = (q[lo:lo + HG_CHUNK] * jnp.exp2(cum[lo:lo + HG_CHUNK] - r)).astype(BF16)
            parts.append(_dot_nt(q_c, k_c))
        off_blocks.append(jnp.concatenate(parts, axis=0))
        cum_ref[rows, :] = cum

    assert used == HG_BLOCK or n_blocks == 1
    n_rows = n_blocks * used
    chunk_rows = lambda ref, s: jnp.concatenate(
        [jnp.broadcast_to(ref[c * HG_CHUNK + s:c * HG_CHUNK + s + 1, :], (HG_CHUNK, B_DIM))
         for c in range(n_rows // HG_CHUNK)], axis=0)
    row_in_chunk = lax.broadcasted_iota(jnp.int32, (n_rows, B_DIM), 0) & (HG_CHUNK - 1)
    q_rows, cum_rows = q_ref[:n_rows, :], cum_ref[:n_rows, :]
    terms = []
    for s in range(HG_CHUNK):
        decay = jnp.exp2(jnp.where(row_in_chunk >= s, cum_rows - chunk_rows(cum_ref, s), -jnp.inf))
        terms.append((q_rows * decay * chunk_rows(k_ref, s)).astype(BF16))
    terms = jnp.concatenate(terms, axis=1)
    term_blocks = [terms[bi * used:(bi + 1) * used] for bi in range(n_blocks)]
    return o_blocks, off_blocks, term_blocks, v16_blocks, st


def _hgrn_finish(o_blocks, off_blocks, v16_blocks, inside_blocks, gate_ref, og_ref, rec_ref, n_chunks):
    used = n_chunks * HG_CHUNK
    row = lax.broadcasted_iota(jnp.int32, (used, B_DIM), 0)
    col = lax.broadcasted_iota(jnp.int32, (used, B_DIM), 1)
    same_chunk = _div_pow2(row, HG_CHUNK) == _div_pow2(col, HG_CHUNK)
    ones = jnp.ones((B_DIM, B_DIM), BF16)
    out = []
    for o, off, v16, inside in zip(o_blocks, off_blocks, v16_blocks, inside_blocks):
        score = off + jnp.where(same_chunk, inside, 0.0)
        intra = _dot(score.astype(BF16), v16)
        if used < HG_BLOCK:
            intra = jnp.concatenate([intra, jnp.zeros((HG_BLOCK - used, B_DIM), F32)], axis=0)
        out.append(o + intra)
    o = jnp.concatenate(out, axis=0)
    ms = _dot((o * o).astype(BF16), ones) * (1.0 / B_DIM)
    gate = gate_ref[...]
    rec_ref[...] = ((o * lax.rsqrt(ms + NORM_EPS) * og_ref[...]) * _silu(gate)).astype(rec_ref.dtype)


def _hgrn(qb, kb, ib, lf, gb, s0, out_gain, batch, seq, tile, n_chunks, heads_per_step):
    nt = seq // tile
    width = heads_per_step * B_DIM
    tok = pl.BlockSpec((tile, width), lambda b, h, t: (b * nt + t, h))
    state = pl.BlockSpec((1, heads_per_step, B_DIM, B_DIM), lambda b, h, t: (b, h, 0, 0))
    idx = jnp.arange(HG_BLOCK)
    tril = (idx[:, None] >= idx[None, :]).astype(BF16)
    pick = (jnp.repeat(jnp.arange(HG_CHUNK), B_DIM)[:, None] == (idx % HG_CHUNK)[None, :]).astype(BF16)
    return pl.pallas_call(
        functools.partial(_hgrn_kernel, n_chunks=n_chunks),
        out_shape=(jax.ShapeDtypeStruct((batch * seq, B_WIDTH), BF16),
                   jax.ShapeDtypeStruct(s0.shape, F32)),
        grid=(batch, B_HEADS // heads_per_step, nt),
        in_specs=[tok, tok, tok, tok, tok, state, pl.BlockSpec((1, width), lambda b, h, t: (0, h)),
                  _resident(tril.shape), _resident(pick.shape)],
        out_specs=(tok, state),
        scratch_shapes=[pltpu.VMEM((heads_per_step, B_DIM, B_DIM), F32),
                        pltpu.VMEM((tile, width), F32)],
        compiler_params=_params(("arbitrary", "arbitrary", "arbitrary")),
        name="hgrn",
    )(qb, kb, ib, lf, gb, s0, out_gain, tril, pick)


def kernel(x_prompt, x_sample, cache_k, cache_v, state_hgrn, page_table, rel_bias_table, lb_logits,
           ffn1_norm, ffn1_gate, ffn1_up, ffn1_down, mix_norm, w_in, q_norm, k_norm, hgrn_out_norm,
           w_out, ffn2_norm, ffn2_gate, ffn2_up, ffn2_down):
    batch, seq, _ = x_prompt.shape
    db, dec_seq, _ = x_sample.shape
    depth, n_phys = cache_k.shape[:2]
    assert depth == 1 and lb_logits.shape[0] == 2
    past = page_table.shape[1] * PAGE_SIZE
    assert seq % MOBA_BLOCK == 0 and past % MOBA_BLOCK == 0 and dec_seq <= HG_CHUNK
    n_p, n_s = batch * seq, db * dec_seq

    bf = lambda w: w[0].astype(BF16)
    gain = lambda g: g[0].reshape(1, -1)
    ffn1 = (gain(ffn1_norm), bf(ffn1_gate), bf(ffn1_up), bf(ffn1_down))
    ffn2 = (gain(ffn2_norm), bf(ffn2_gate), bf(ffn2_up), bf(ffn2_down))
    wo = bf(w_out)
    w_in16 = bf(w_in)
    per_lane = lambda g: jnp.broadcast_to(jnp.tile(g[0], A_HEADS)[:, None], (A_WIDTH, 128))
    proj_w = (gain(mix_norm), w_in16[:, :3 * GROUP_W].T, w_in16[:, 3 * GROUP_W:],
              per_lane(q_norm), per_lane(k_norm), lb_logits)
    out_gain = hgrn_out_norm[0].reshape(1, -1)
    w_bias, sbt, ob = _bias_tables(rel_bias_table.T, seq, past, dec_seq)

    ys = _ffn(x_sample.reshape(n_s, D_MODEL), *ffn1, n_s)
    q_t, k_t, v_t, qb_s, kb_s, ib_s, lf_s, gb_s = _proj(ys, *proj_w, n_s, n_s)
    q_s, k_s, v_s = (a.T.reshape(db, dec_seq, A_WIDTH) for a in (q_t, k_t, v_t))
    pad_rows = lambda a, r: jnp.pad(a.reshape(db, dec_seq, -1), ((0, 0), (0, r - dec_seq), (0, 0)))
    pages = lambda c: c[0].transpose(0, 2, 3, 1).reshape(n_phys, A_WIDTH, PAGE_SIZE)
    sample_attn = functools.partial(
        _moba_sample, q_s, pad_rows(k_s, 8), pad_rows(v_s, 8), pages(cache_k), pages(cache_v), page_table, sbt, ob,
        bpg=math.gcd(past // MOBA_BLOCK, SAMPLE_BLOCKS_PER_STEP))

    tm = 512
    x_p = x_prompt.reshape(n_p, D_MODEL)
    if n_p // tm == db:
        y, attn_s = sample_attn(ffn=(x_p, *ffn1, tm))
    else:
        y, attn_s = _ffn(x_p, *ffn1, tm), sample_attn()
    q_t, kp_t, vp_t, qb, kb, ib, lf, gb = _proj(y, *proj_w, seq, tm)
    attn = _moba_prompt(q_t, kp_t, vp_t, w_bias.reshape(A_HEADS // 2, 2, -1), batch, seq)
    s0 = jnp.zeros((batch, B_HEADS, B_DIM, B_DIM), F32)
    rec, s_p = _hgrn(qb, kb, ib, lf, gb, s0, out_gain, batch, seq, math.gcd(seq, HGRN_TILE), HG_BLOCK // HG_CHUNK, 1)
    y_p = _ffn(y, *ffn2, tm, mix=(attn, rec, wo))
    heads_p = lambda a: a.reshape(batch, A_HEADS, A_HEAD_DIM, seq).transpose(0, 3, 1, 2)[None]

    blocked = lambda a: pad_rows(a, HG_BLOCK).reshape(db * HG_BLOCK, -1)
    rec, s_s = _hgrn(blocked(qb_s), blocked(kb_s), blocked(ib_s), blocked(lf_s), blocked(gb_s), state_hgrn[0],
                     out_gain, db, HG_BLOCK, HG_BLOCK, 1, B_HEADS)
    rec = rec.reshape(db, HG_BLOCK, B_WIDTH)[:, :dec_seq].reshape(n_s, B_WIDTH)
    y_s = _ffn(ys, *ffn2, n_s, mix=(attn_s.reshape(n_s, A_WIDTH), rec, wo))
    heads_s = lambda a: a.reshape(1, db, dec_seq, A_HEADS, A_HEAD_DIM)

    return (y_p.reshape(batch, seq, D_MODEL), y_s.reshape(db, dec_seq, D_MODEL),
            heads_p(kp_t), heads_p(vp_t), s_p[None],
            heads_s(k_s), heads_s(v_s), s_s[None])
```

```python
import functools
import math

import jax
import jax.numpy as jnp
from jax import lax
from jax.experimental import pallas as pl
from jax.experimental.pallas import tpu as pltpu

F32 = jnp.float32
BF16 = jnp.bfloat16

D_MODEL = 1024
A_HEADS = 8
A_HEAD_DIM = 64
A_WIDTH = A_HEADS * A_HEAD_DIM
B_HEADS = 4
B_DIM = 128
B_WIDTH = B_HEADS * B_DIM
GROUP_W = 512
N_GROUPS = 7
D_FF = 2816
MOBA_BLOCK = 256
MOBA_TOPK = 3
PAGE_SIZE = 128
N_BUCKETS = 32
MAX_EXACT = N_BUCKETS // 2
REL_MAX_DIST = 4096
HG_CHUNK = 16
HG_BLOCK = 128
NORM_EPS = 1e-6
NEG_INF = -1e30
ATTN_SCALE = A_HEAD_DIM ** -0.5
LOG2_E = math.log2(math.e)

FF_CHUNK = 256
SAMPLE_BLOCKS_PER_STEP = 8
MOBA_BLOCKS_PER_TRIP = 4
HGRN_TILE = 2048
MOBA_ONES_ROWS = 16
VMEM_LIMIT = 48 * 1024 * 1024
VMEM_LIMIT_FUSED = 56 * 1024 * 1024


def _resident(shape):
    nd = len(shape)
    return pl.BlockSpec(shape, lambda *_: (0,) * nd, pipeline_mode=pl.Buffered(1))


def _params(semantics):
    return pltpu.CompilerParams(dimension_semantics=semantics, vmem_limit_bytes=VMEM_LIMIT)


def _dot(a, b):
    return jnp.dot(a, b, preferred_element_type=F32)


def _dot_nt(a, b):
    return lax.dot_general(a, b, (((1,), (1,)), ((), ())), preferred_element_type=F32)


def _split3(a):
    a1 = a.astype(BF16)
    r1 = a - a1.astype(F32)
    a2 = r1.astype(BF16)
    a3 = (r1 - a2.astype(F32)).astype(BF16)
    return a1, a2, a3


def _div_pow2(x, d):
    return lax.shift_right_logical(x, int(math.log2(d)))


def _rms_rows(x, gain):
    ms = jnp.mean(x * x, axis=-1, keepdims=True)
    return x * lax.rsqrt(ms + NORM_EPS) * gain


def _silu(x):
    return x * jax.nn.sigmoid(x)


def _ffn_chunks(h, wg_ref, wu_ref, wd_ref, acc, chunks):
    for j in chunks:
        cols = slice(j * FF_CHUNK, (j + 1) * FF_CHUNK)
        g = _dot(h, wg_ref[:, cols])
        u = _dot(h, wu_ref[:, cols])
        acc = acc + _dot((_silu(g) * u).astype(BF16), wd_ref[cols, :])
    return acc


def _ffn_kernel(*refs, with_mix):
    if with_mix:
        x_ref, attn_ref, rec_ref, wo_ref, gain_ref, wg_ref, wu_ref, wd_ref, o_ref = refs
        y = (x_ref[...] + _dot(attn_ref[...], wo_ref[:A_WIDTH, :])
             + _dot(rec_ref[...], wo_ref[A_WIDTH:, :]))
    else:
        x_ref, gain_ref, wg_ref, wu_ref, wd_ref, o_ref = refs
        y = x_ref[...]
    h = _rms_rows(y, gain_ref[...]).astype(BF16)
    acc = _ffn_chunks(h, wg_ref, wu_ref, wd_ref, jnp.zeros(y.shape, F32), range(D_FF // FF_CHUNK))
    o_ref[...] = y + 0.5 * acc


def _ffn_sample_kernel(pt_ref, x_ref, gain_ref, wg_ref, wu_ref, wd_ref, q_ref, kn_ref, vn_ref, sbt_ref, ob_ref,
                       ck_ref, cv_ref, y_ref, attn_ref, *scratch, nbk, dec_seq, bpg):
    stages = _sample_stages(pl.program_id(0), pl.num_programs(0), pt_ref, q_ref, kn_ref, vn_ref, sbt_ref, ob_ref,
                            ck_ref, cv_ref, attn_ref, *scratch, nbk=nbk, dec_seq=dec_seq, bpg=bpg)
    y = x_ref[...]
    h = _rms_rows(y, gain_ref[...]).astype(BF16)
    acc = jnp.zeros(y.shape, F32)
    n_chunks = D_FF // FF_CHUNK
    cuts = [k * n_chunks // len(stages) for k in range(len(stages) + 1)]
    for k, stage in enumerate(stages):
        stage()
        acc = _ffn_chunks(h, wg_ref, wu_ref, wd_ref, acc, range(cuts[k], cuts[k + 1]))
    y_ref[...] = y + 0.5 * acc


def _ffn(x, gain, wg, wu, wd, tm, mix=None):
    n = x.shape[0]
    row = lambda w: pl.BlockSpec((tm, w), lambda i: (i, 0))
    args, specs = [x], [row(D_MODEL)]
    if mix is not None:
        attn, rec, wo = mix
        args += [attn, rec, wo]
        specs += [row(A_WIDTH), row(B_WIDTH), _resident(wo.shape)]
    args += [gain, wg, wu, wd]
    specs += [_resident(gain.shape), _resident(wg.shape), _resident(wu.shape), _resident(wd.shape)]
    return pl.pallas_call(
        functools.partial(_ffn_kernel, with_mix=mix is not None),
        out_shape=jax.ShapeDtypeStruct((n, D_MODEL), F32),
        grid=(n // tm,),
        in_specs=specs,
        out_specs=row(D_MODEL),
        compiler_params=_params(("parallel",)),
        name="ffn_mix" if mix is not None else "ffn",
    )(*args)


def _proj_kernel(y_ref, gain_ref, wat_ref, wb_ref, qg_ref, kg_ref, lbl_ref,
                 q_ref, k_ref, v_ref, qb_ref, kb_ref, ib_ref, lf_ref, gb_ref):
    h = _rms_rows(y_ref[...], gain_ref[...]).astype(BF16)
    tm = h.shape[0]

    def group_t(i):
        return _dot_nt(wat_ref[i * GROUP_W:(i + 1) * GROUP_W, :], h)

    def group(i):
        return _dot(h, wb_ref[:, i * GROUP_W:(i + 1) * GROUP_W])

    def head_norm(z, g):
        out = []
        for hd in range(A_HEADS):
            zh = z[hd * A_HEAD_DIM:(hd + 1) * A_HEAD_DIM, :]
            out.append(zh * lax.rsqrt(jnp.mean(zh * zh, axis=0, keepdims=True) + NORM_EPS))
        return jnp.concatenate(out, axis=0) * jnp.concatenate([g] * pl.cdiv(tm, 128), axis=1)[:, :tm]

    q_ref[...] = head_norm(group_t(0), qg_ref[...]) * ATTN_SCALE
    k_ref[...] = head_norm(group_t(1), kg_ref[...])
    v_ref[...] = group_t(2)
    qb_ref[...] = _silu(group(0))
    lbl = lbl_ref[...]
    e = jnp.exp(lbl - jnp.max(lbl, axis=0, keepdims=True))
    lb = e[0:1, :] / jnp.sum(e, axis=0, keepdims=True)
    forget = lb + (1.0 - lb) * jax.nn.sigmoid(group(1))
    lf_ref[...] = jnp.log(forget)
    kb_ref[...] = 1.0 - forget
    ib_ref[...] = group(2)
    gb_ref[...] = group(3)


def _proj(y, gain, wat, wb, qg, kg, lbl, seq, tm):
    n = y.shape[0]
    nt = seq // tm
    row = lambda w: pl.BlockSpec((tm, w), lambda i: (i, 0))
    col = pl.BlockSpec((GROUP_W, tm), lambda i: (i // nt, i % nt))
    out_t = jax.ShapeDtypeStruct((n // seq * GROUP_W, seq), F32)
    out = jax.ShapeDtypeStruct((n, GROUP_W), F32)
    return pl.pallas_call(
        _proj_kernel,
        out_shape=(out_t,) * 3 + (out,) * 5,
        grid=(n // tm,),
        in_specs=[row(D_MODEL)] + [_resident(a.shape) for a in (gain, wat, wb, qg, kg, lbl)],
        out_specs=(col,) * 3 + (row(GROUP_W),) * 5,
        compiler_params=_params(("parallel",)),
        name="proj",
    )(y, gain, wat, wb, qg, kg, lbl)


def _rel_bucket(n):
    nf = jnp.maximum(n, 1).astype(F32)
    large = MAX_EXACT + (jnp.log(nf / MAX_EXACT) / math.log(REL_MAX_DIST / MAX_EXACT)
                         * (N_BUCKETS - MAX_EXACT)).astype(jnp.int32)
    large = jnp.minimum(large, N_BUCKETS - 1)
    return jnp.where(n < MAX_EXACT, n, large)


def _bias_kernel(tab_ref, w_ref, sbt_ref, ob_ref, *, past, dec_seq):
    tab = tab_ref[...]
    tab_rows = jnp.concatenate([tab] * dec_seq, axis=0)

    def lookup(bucket, table):
        out = jnp.zeros(bucket.shape, F32)
        for b in range(N_BUCKETS):
            out = jnp.where(bucket == b, table[:, b:b + 1], out)
        return out

    c = lax.broadcasted_iota(jnp.int32, w_ref.shape, 1)
    w_ref[...] = lookup(_rel_bucket(jnp.maximum(c - MOBA_BLOCK, 0)), tab)
    t = _div_pow2(lax.broadcasted_iota(jnp.int32, sbt_ref.shape, 0), A_HEADS)
    kpos = lax.broadcasted_iota(jnp.int32, sbt_ref.shape, 1)
    sbt_ref[...] = lookup(_rel_bucket(jnp.maximum(past + t - kpos, 0)), tab_rows)
    t = _div_pow2(lax.broadcasted_iota(jnp.int32, ob_ref.shape, 0), A_HEADS)
    s = lax.broadcasted_iota(jnp.int32, ob_ref.shape, 1)
    ob_ref[...] = lookup(_rel_bucket(jnp.maximum(t - s, 0)), tab_rows)


def _bias_tables(tab_t, seq, past, dec_seq):
    rows = dec_seq * A_HEADS
    return pl.pallas_call(
        functools.partial(_bias_kernel, past=past, dec_seq=dec_seq),
        out_shape=(jax.ShapeDtypeStruct((A_HEADS, seq + MOBA_BLOCK), F32),
                   jax.ShapeDtypeStruct((rows, past), F32),
                   jax.ShapeDtypeStruct((rows, 128), F32)),
        name="rel_bias",
    )(tab_t)


def _top_k_mask(gate, k, axis):
    index = lax.broadcasted_iota(jnp.int32, gate.shape, axis).astype(F32)
    sel = jnp.zeros(gate.shape, F32)
    for _ in range(k):
        mx = jnp.max(gate, axis=axis, keepdims=True)
        first = jnp.min(jnp.where(gate == mx, index, float(gate.shape[axis])), axis=axis, keepdims=True)
        chosen = index == first
        sel = jnp.where(chosen & (mx > -jnp.inf), 1.0, sel)
        gate = jnp.where(chosen, -jnp.inf, gate)
    return sel


def _moba_prompt_kernel(q_ref, k_ref, v_ref, w_ref, o_ref, kb_ref, vt_ref, km_ref, tt_ref, s_ref, qp_ref, *, nb):
    b = pl.program_id(1)
    i = pl.program_id(2)
    blk = MOBA_BLOCK

    key_j = lax.broadcasted_iota(jnp.int32, (blk, blk), 0)
    qry_c = lax.broadcasted_iota(jnp.int32, (blk, blk), 1)

    @pl.when((b == 0) & (i == 0))
    def _build_bias_tiles():
        for hh in range(2):
            for d in range(nb):
                win = jnp.broadcast_to(w_ref[0, hh:hh + 1, d * blk:(d + 2) * blk], (blk, 2 * blk))
                tile = pltpu.roll(win, 0, 1, stride=1, stride_axis=0)[:, blk:] * LOG2_E
                tt_ref[hh, d] = jnp.where(qry_c >= key_j, tile, NEG_INF) if d == 0 else tile

    @pl.when(i == 0)
    def _prepare_sequence():
        lane = lax.broadcasted_iota(jnp.int32, (blk, 128), 1)
        ones = jnp.ones((MOBA_ONES_ROWS, blk), F32)
        for n in range(nb):
            cols = slice(n * blk, (n + 1) * blk)
            kblk = k_ref[:, cols].T
            kb_ref[n] = jnp.concatenate([kblk, jnp.where(lane == n, 1.0, 0.0)], axis=1).astype(BF16)
            km_ref[n:n + 1, :] = jnp.mean(kblk, axis=0, keepdims=True)
            for hh in range(2):
                v_h = v_ref[hh * A_HEAD_DIM:(hh + 1) * A_HEAD_DIM, cols]
                vt_ref[n, hh] = jnp.concatenate([v_h, ones], axis=0).astype(BF16)
        km = km_ref[...]
        km_lane = lax.broadcasted_iota(jnp.int32, km.shape, 1)
        feat = lax.broadcasted_iota(jnp.int32, (128, blk), 0)
        blk_id = lax.broadcasted_iota(jnp.int32, (nb, blk), 0)
        for t in range(nb):
            q_t = q_ref[:, t * blk:(t + 1) * blk] * LOG2_E
            for hh in range(2):
                lo = hh * A_HEAD_DIM
                q_h = jnp.where((feat >= lo) & (feat < lo + A_HEAD_DIM), q_t, 0.0)
                km_h = jnp.where((km_lane >= lo) & (km_lane < lo + A_HEAD_DIM), km, 0.0)
                k1, k2, _ = _split3(km_h)
                q1, q2, _ = _split3(q_h)
                gate = _dot(k1, q1) + _dot(k1, q2) + _dot(k2, q1)
                gate = jnp.where(blk_id < t, gate, -jnp.inf)
                attends = jnp.where(blk_id == t, 1.0, _top_k_mask(gate, MOBA_TOPK, 0))
                penalty = jnp.concatenate([jnp.where(attends > 0.0, 0.0, NEG_INF),
                                           jnp.zeros((128 - nb, blk), F32)], axis=0)
                qp_ref[t, hh] = jnp.concatenate([q1, penalty.astype(BF16)], axis=0)

    fold = lambda a, op: op(a.reshape(blk // 8, 8, blk), axis=0)
    qp = [qp_ref[i, hh] for hh in range(2)]

    per_trip = math.gcd(nb, MOBA_BLOCKS_PER_TRIP)
    full_trips = _div_pow2(i + 1, per_trip)
    left_over = (i + 1) & (per_trip - 1)

    def walk(body, carry):
        carry = lax.fori_loop(0, full_trips, lambda j, c: body(per_trip * j, per_trip, c), carry)
        for left in range(1, per_trip):
            count = left if 2 * left <= per_trip else per_trip
            carry = lax.cond(left_over == left,
                             lambda c, count=count: body(per_trip * full_trips, count, c), lambda c: c, carry)
        return carry

    pairs = lambda first, count: [(first + u, hh) for u in range(count) for hh in range(2)]

    def logits(first, count, m8):
        raw = [_dot(kb_ref[n], qp[hh]) for n, hh in pairs(first, count)]
        out = list(m8)
        for (n, hh), r in zip(pairs(first, count), raw):
            s = r + tt_ref[hh, jnp.maximum(i - n, 0)]
            s_ref[hh, n] = s
            out[hh] = jnp.maximum(out[hh], fold(s, jnp.max))
        return tuple(out)

    m8 = walk(logits, (jnp.full((8, blk), -jnp.inf, F32),) * 2)
    m = [jnp.max(a, axis=0, keepdims=True) for a in m8]

    def values(first, count, acc):
        p = [jnp.exp2(s_ref[hh, n] - m[hh]).astype(BF16) for n, hh in pairs(first, count)]
        out = list(acc)
        for (n, hh), ph in zip(pairs(first, count), p):
            out[hh] = out[hh] + _dot(vt_ref[n, hh], ph)
        return tuple(out)

    acc = walk(values, (jnp.zeros((A_HEAD_DIM + MOBA_ONES_ROWS, blk), F32),) * 2)
    out_t = jnp.concatenate([a[:A_HEAD_DIM] / a[A_HEAD_DIM:A_HEAD_DIM + 1] for a in acc], axis=0)
    o_ref[...] = out_t.T.astype(o_ref.dtype)


def _moba_prompt(q_t, k_t, v_t, w_pairs, batch, seq):
    nb = seq // MOBA_BLOCK
    pairs = A_HEADS // 2
    whole = pl.BlockSpec((128, seq), lambda p, b, i: (b * pairs + p, 0))
    return pl.pallas_call(
        functools.partial(_moba_prompt_kernel, nb=nb),
        out_shape=jax.ShapeDtypeStruct((batch * seq, A_WIDTH), BF16),
        grid=(pairs, batch, nb),
        in_specs=[whole, whole, whole,
                  pl.BlockSpec((1, 2, seq + MOBA_BLOCK), lambda p, b, i: (p, 0, 0))],
        out_specs=pl.BlockSpec((MOBA_BLOCK, 128), lambda p, b, i: (b * nb + i, p)),
        scratch_shapes=[pltpu.VMEM((nb, MOBA_BLOCK, 256), BF16),
                        pltpu.VMEM((nb, 2, A_HEAD_DIM + MOBA_ONES_ROWS, MOBA_BLOCK), BF16),
                        pltpu.VMEM((nb, 128), F32),
                        pltpu.VMEM((2, nb, MOBA_BLOCK, MOBA_BLOCK), F32),
                        pltpu.VMEM((2, nb, MOBA_BLOCK, MOBA_BLOCK), F32),
                        pltpu.VMEM((nb, 2, 256, MOBA_BLOCK), BF16)],
        compiler_params=_params(("arbitrary", "arbitrary", "arbitrary")),
        name="moba_prompt",
    )(q_t, k_t, v_t, w_pairs)


def _moba_sample_kernel(*refs, nbk, dec_seq, bpg):
    for stage in _sample_stages(pl.program_id(0), pl.num_programs(0), *refs, nbk=nbk, dec_seq=dec_seq, bpg=bpg):
        stage()


def _sample_stages(b, n_seq, pt_ref, q_ref, kn_ref, vn_ref, sbt_ref, ob_ref, ck_ref, cv_ref, o_ref,
                   kbuf, vbuf, sem, m_sc, l_sc, acc_sc, g_sc, *, nbk, dec_seq, bpg):
    n_groups = nbk // bpg
    ppg = 2 * bpg
    rows = dec_seq * A_HEADS
    head_of_row = lax.broadcasted_iota(jnp.int32, (A_HEADS, A_WIDTH), 0)
    head_of_lane = _div_pow2(lax.broadcasted_iota(jnp.int32, (A_HEADS, A_WIDTH), 1), A_HEAD_DIM)
    head_mask = head_of_row == head_of_lane
    wide = lambda a: jnp.broadcast_to(a, (rows, 128))

    def group_copies(seq, g, slot):
        out = []
        for r in range(ppg):
            page = pt_ref[seq * (n_groups * ppg) + g * ppg + r]
            out.append(pltpu.make_async_copy(ck_ref.at[page], kbuf.at[slot, r], sem.at[slot, 0]))
            out.append(pltpu.make_async_copy(cv_ref.at[page], vbuf.at[slot, r], sem.at[slot, 1]))
        return out

    def queries():
        return jnp.concatenate(
            [jnp.where(head_mask, jnp.broadcast_to(q_ref[0, t:t + 1, :], head_mask.shape), 0.0)
             for t in range(dec_seq)], axis=0).astype(BF16)

    live = {}

    def group_keys(g):
        slot = (b * n_groups + g) & 1
        if g == 0:
            @pl.when(b == 0)
            def _first_group():
                for cp in group_copies(0, 0, 0):
                    cp.start()
        if g + 1 < n_groups:
            for cp in group_copies(b, g + 1, 1 - slot):
                cp.start()
        else:
            @pl.when(b + 1 < n_seq)
            def _next_sequence():
                for cp in group_copies(b + 1, 0, 1 - slot):
                    cp.start()
        for cp in group_copies(b, g, slot):
            cp.wait()
        q16 = queries()
        live["raw"] = jnp.concatenate(
            [_dot(q16, jnp.concatenate([kbuf[slot, 2 * r], kbuf[slot, 2 * r + 1]], axis=1).astype(BF16))
             for r in range(bpg)], axis=1)

    per_block = lambda a: [a[:, r * MOBA_BLOCK:(r + 1) * MOBA_BLOCK] for r in range(bpg)]

    def group_softmax(g):
        width = bpg * MOBA_BLOCK
        s = per_block(live["raw"] + sbt_ref[:, g * width:(g + 1) * width])
        live["m"] = [jnp.max(a, axis=1, keepdims=True) for a in s]
        live["p"] = [jnp.exp(a - mr) for a, mr in zip(s, live["m"])]

    def group_values(g):
        slot = (b * n_groups + g) & 1
        raw, m, p = per_block(live.pop("raw")), live.pop("m"), live.pop("p")
        for r in range(bpg):
            n = g * bpg + r
            vblk = jnp.concatenate([vbuf[slot, 2 * r], vbuf[slot, 2 * r + 1]], axis=1).astype(BF16)
            m_sc[n] = wide(m[r])
            l_sc[n] = wide(jnp.sum(p[r], axis=1, keepdims=True))
            acc_sc[n] = _dot_nt(p[r].astype(BF16), vblk)
            g_sc[n] = wide(jnp.sum(raw[r], axis=1, keepdims=True) * (1.0 / MOBA_BLOCK))

    def merge():
        q16 = queries()
        gates = [g_sc[n] for n in range(nbk)]
        picked = [jnp.zeros((rows, 128), jnp.bool_)] * nbk
        for _ in range(min(MOBA_TOPK, nbk)):
            mx = functools.reduce(jnp.maximum, gates)
            first = functools.reduce(
                jnp.minimum, [jnp.where(g == mx, float(n), float(nbk)) for n, g in enumerate(gates)])
            for n in range(nbk):
                chosen = first == float(n)
                picked[n] = picked[n] | (chosen & (mx > -jnp.inf))
                gates[n] = jnp.where(chosen, -jnp.inf, gates[n])
        zeros = jnp.zeros((128 - kn_ref.shape[1], A_WIDTH), F32)
        k_own = jnp.concatenate([kn_ref[0], zeros], axis=0).astype(BF16)
        v_own = jnp.concatenate([vn_ref[0], zeros], axis=0).astype(BF16)
        s = _dot_nt(q16, k_own) + ob_ref[...]
        t_of_row = _div_pow2(lax.broadcasted_iota(jnp.int32, s.shape, 0), A_HEADS)
        s_idx = lax.broadcasted_iota(jnp.int32, s.shape, 1)
        s = jnp.where(s_idx <= t_of_row, s, NEG_INF)
        m_own = wide(jnp.max(s, axis=1, keepdims=True))
        p = jnp.exp(s - m_own)
        l_own = wide(jnp.sum(p, axis=1, keepdims=True))
        acc_own = _dot(p.astype(BF16), v_own)
        m_all = m_own
        for n in range(nbk):
            m_all = jnp.maximum(m_all, jnp.where(picked[n], m_sc[n], -jnp.inf))
        lanes4 = lambda a: jnp.concatenate([a] * (A_WIDTH // 128), axis=1)
        w = jnp.exp(m_own - m_all)
        l_all = w * l_own
        acc = lanes4(w) * acc_own
        for n in range(nbk):
            w = jnp.where(picked[n], jnp.exp(m_sc[n] - m_all), 0.0)
            l_all = l_all + w * l_sc[n]
            acc = acc + lanes4(w) * acc_sc[n]
        res = acc / lanes4(l_all)
        out = [jnp.sum(jnp.where(head_mask, res[t * A_HEADS:(t + 1) * A_HEADS, :], 0.0),
                       axis=0, keepdims=True) for t in range(dec_seq)]
        o_ref[0] = jnp.concatenate(out, axis=0).astype(o_ref.dtype)

    return [functools.partial(stage, g) for g in range(n_groups)
            for stage in (group_keys, group_softmax, group_values)] + [merge]


def _moba_sample(q, k_new, v_new, cache_k, cache_v, page_table, sbt, ob, bpg, ffn=None):
    db, dec_seq, _ = q.shape
    n_pages = page_table.shape[1]
    nbk = n_pages * PAGE_SIZE // MOBA_BLOCK
    rows = dec_seq * A_HEADS
    assert MOBA_BLOCK == 2 * PAGE_SIZE and nbk % bpg == 0
    per_seq = lambda b, pt: (b, 0, 0)
    whole = lambda a: pl.BlockSpec(a.shape, lambda b, pt: (0,) * a.ndim, pipeline_mode=pl.Buffered(1))
    sample_specs = [pl.BlockSpec((1, dec_seq, A_WIDTH), per_seq),
                    pl.BlockSpec((1, k_new.shape[1], A_WIDTH), per_seq),
                    pl.BlockSpec((1, v_new.shape[1], A_WIDTH), per_seq),
                    whole(sbt), whole(ob),
                    pl.BlockSpec(memory_space=pl.ANY), pl.BlockSpec(memory_space=pl.ANY)]
    sample_args = (q, k_new, v_new, sbt, ob, cache_k, cache_v)
    attn_spec = pl.BlockSpec((1, dec_seq, A_WIDTH), per_seq)
    attn_shape = jax.ShapeDtypeStruct((db, dec_seq, A_WIDTH), BF16)
    scratch = [pltpu.VMEM((2, 2 * bpg, A_WIDTH, PAGE_SIZE), F32),
               pltpu.VMEM((2, 2 * bpg, A_WIDTH, PAGE_SIZE), F32),
               pltpu.SemaphoreType.DMA((2, 2)),
               pltpu.VMEM((nbk, rows, 128), F32),
               pltpu.VMEM((nbk, rows, 128), F32),
               pltpu.VMEM((nbk, rows, A_WIDTH), F32),
               pltpu.VMEM((nbk, rows, 128), F32)]
    static = dict(nbk=nbk, dec_seq=dec_seq, bpg=bpg)
    if ffn is None:
        return pl.pallas_call(
            functools.partial(_moba_sample_kernel, **static),
            out_shape=attn_shape,
            grid_spec=pltpu.PrefetchScalarGridSpec(num_scalar_prefetch=1, grid=(db,), in_specs=sample_specs,
                                                   out_specs=attn_spec, scratch_shapes=scratch),
            compiler_params=_params(("arbitrary",)),
            name="moba_sample",
        )(page_table.reshape(-1), *sample_args)
    x, gain, wg, wu, wd, tm = ffn
    assert x.shape[0] == db * tm
    row = pl.BlockSpec((tm, D_MODEL), lambda b, pt: (b, 0))
    return pl.pallas_call(
        functools.partial(_ffn_sample_kernel, **static),
        out_shape=(jax.ShapeDtypeStruct(x.shape, F32), attn_shape),
        grid_spec=pltpu.PrefetchScalarGridSpec(
            num_scalar_prefetch=1, grid=(db,),
            in_specs=[row] + [whole(a) for a in (gain, wg, wu, wd)] + sample_specs,
            out_specs=(row, attn_spec), scratch_shapes=scratch),
        compiler_params=pltpu.CompilerParams(dimension_semantics=("arbitrary",),
                                             vmem_limit_bytes=VMEM_LIMIT_FUSED),
        name="ffn_sample",
    )(page_table.reshape(-1), x, gain, wg, wu, wd, *sample_args)


def _hgrn_kernel(q_ref, k_ref, v_ref, g_ref, gate_ref, s0_ref, og_ref, tril_ref, pick_ref, rec_ref, sfin_ref,
                 st_ref, cum_ref, *, n_chunks):
    tb = pl.program_id(2)
    heads = range(q_ref.shape[1] // B_DIM)

    @pl.when(tb == 0)
    def _load_state():
        for h in heads:
            st_ref[h] = s0_ref[0, h].T

    head = lambda ref, h: ref.at[:, h * B_DIM:(h + 1) * B_DIM]
    used = n_chunks * HG_CHUNK
    work, terms = [], []
    for h in heads:
        o_blocks, off_blocks, term_blocks, v16_blocks, st_ref[h] = _hgrn_scores(
            head(q_ref, h), head(k_ref, h), head(v_ref, h), head(g_ref, h), tril_ref, st_ref[h], head(cum_ref, h),
            n_chunks)
        work.append((o_blocks, off_blocks, v16_blocks))
        terms += term_blocks
    half = (len(terms) + 1) // 2
    inside = [_dot(jnp.concatenate(part, axis=0), pick_ref[...]) for part in (terms[:half], terms[half:]) if part]
    inside = [a[at * used:(at + 1) * used] for a in inside for at in range(a.shape[0] // used)]
    for h in heads:
        n_blocks = len(work[h][0])
        _hgrn_finish(*work[h], inside[h * n_blocks:(h + 1) * n_blocks], head(gate_ref, h), head(og_ref, h),
                     head(rec_ref, h), n_chunks)

    @pl.when(tb == pl.num_programs(2) - 1)
    def _store_state():
        for h in heads:
            sfin_ref[0, h] = st_ref[h].T


def _hgrn_scores(q_ref, k_ref, v_ref, g_ref, tril_ref, st, cum_ref, n_chunks):
    n_blocks = q_ref.shape[0] // HG_BLOCK
    used = n_chunks * HG_CHUNK
    blocks = [slice(bi * HG_BLOCK, (bi + 1) * HG_BLOCK) for bi in range(n_blocks)]

    tril = tril_ref[...]
    g_all = jnp.concatenate([g_ref[rows, :] for rows in blocks], axis=1) * LOG2_E
    g1, g2, g3 = _split3(g_all)
    cum_all = _dot(tril, g1) + _dot(tril, g2) + _dot(tril, g3)

    o_blocks, off_blocks, term_blocks, v16_blocks = [], [], [], []
    for bi, rows in enumerate(blocks):
        q, k, v = q_ref[rows, :], k_ref[rows, :], v_ref[rows, :]
        cum = cum_all[:, bi * B_DIM:(bi + 1) * B_DIM]
        last = cum[HG_BLOCK - 1:HG_BLOCK, :]
        v16_blocks.append(v.astype(BF16))
        o_blocks.append(_dot_nt((q * jnp.exp2(cum)).astype(BF16), st.astype(BF16)))
        st = st * jnp.exp2(last) + _dot(v.T.astype(BF16), (k * jnp.exp2(last - cum)).astype(BF16))
        parts = [jnp.zeros((HG_CHUNK, B_DIM), F32)]
        for c in range(1, n_chunks):
            lo = c * HG_CHUNK
            r = cum[lo - 1:lo, :]
            k_c = (k[:lo] * jnp.exp2(r - cum[:lo])).astype(BF16)
            k_c = jnp.concatenate([k_c, jnp.zeros((HG_BLOCK - lo, B_DIM), BF16)], axis=0)
            q_c = (q[lo:lo + HG_CHUNK] * jnp.exp2(cum[lo:lo + HG_CHUNK] - r)).astype(BF16)
            parts.append(_dot_nt(q_c, k_c))
        off_blocks.append(jnp.concatenate(parts, axis=0))
        cum_ref[rows, :] = cum

    assert used == HG_BLOCK or n_blocks == 1
    n_rows = n_blocks * used
    chunk_rows = lambda ref, s: jnp.concatenate(
        [jnp.broadcast_to(ref[c * HG_CHUNK + s:c * HG_CHUNK + s + 1, :], (HG_CHUNK, B_DIM))
         for c in range(n_rows // HG_CHUNK)], axis=0)
    row_in_chunk = lax.broadcasted_iota(jnp.int32, (n_rows, B_DIM), 0) & (HG_CHUNK - 1)
    q_rows, cum_rows = q_ref[:n_rows, :], cum_ref[:n_rows, :]
    terms = []
    for s in range(HG_CHUNK):
        decay = jnp.exp2(jnp.where(row_in_chunk >= s, cum_rows - chunk_rows(cum_ref, s), -jnp.inf))
        terms.append((q_rows * decay * chunk_rows(k_ref, s)).astype(BF16))
    terms = jnp.concatenate(terms, axis=1)
    term_blocks = [terms[bi * used:(bi + 1) * used] for bi in range(n_blocks)]
    return o_blocks, off_blocks, term_blocks, v16_blocks, st


def _hgrn_finish(o_blocks, off_blocks, v16_blocks, inside_blocks, gate_ref, og_ref, rec_ref, n_chunks):
    used = n_chunks * HG_CHUNK
    row = lax.broadcasted_iota(jnp.int32, (used, B_DIM), 0)
    col = lax.broadcasted_iota(jnp.int32, (used, B_DIM), 1)
    same_chunk = _div_pow2(row, HG_CHUNK) == _div_pow2(col, HG_CHUNK)
    ones = jnp.ones((B_DIM, B_DIM), BF16)
    out = []
    for o, off, v16, inside in zip(o_blocks, off_blocks, v16_blocks, inside_blocks):
        score = off + jnp.where(same_chunk, inside, 0.0)
        intra = _dot(score.astype(BF16), v16)
        if used < HG_BLOCK:
            intra = jnp.concatenate([intra, jnp.zeros((HG_BLOCK - used, B_DIM), F32)], axis=0)
        out.append(o + intra)
    o = jnp.concatenate(out, axis=0)
    ms = _dot((o * o).astype(BF16), ones) * (1.0 / B_DIM)
    gate = gate_ref[...]
    rec_ref[...] = ((o * lax.rsqrt(ms + NORM_EPS) * og_ref[...]) * _silu(gate)).astype(rec_ref.dtype)


def _hgrn(qb, kb, ib, lf, gb, s0, out_gain, batch, seq, tile, n_chunks, heads_per_step):
    nt = seq // tile
    width = heads_per_step * B_DIM
    tok = pl.BlockSpec((tile, width), lambda b, h, t: (b * nt + t, h))
    state = pl.BlockSpec((1, heads_per_step, B_DIM, B_DIM), lambda b, h, t: (b, h, 0, 0))
    idx = jnp.arange(HG_BLOCK)
    tril = (idx[:, None] >= idx[None, :]).astype(BF16)
    pick = (jnp.repeat(jnp.arange(HG_CHUNK), B_DIM)[:, None] == (idx % HG_CHUNK)[None, :]).astype(BF16)
    return pl.pallas_call(
        functools.partial(_hgrn_kernel, n_chunks=n_chunks),
        out_shape=(jax.ShapeDtypeStruct((batch * seq, B_WIDTH), BF16),
                   jax.ShapeDtypeStruct(s0.shape, F32)),
        grid=(batch, B_HEADS // heads_per_step, nt),
        in_specs=[tok, tok, tok, tok, tok, state, pl.BlockSpec((1, width), lambda b, h, t: (0, h)),
                  _resident(tril.shape), _resident(pick.shape)],
        out_specs=(tok, state),
        scratch_shapes=[pltpu.VMEM((heads_per_step, B_DIM, B_DIM), F32),
                        pltpu.VMEM((tile, width), F32)],
        compiler_params=_params(("arbitrary", "arbitrary", "arbitrary")),
        name="hgrn",
    )(qb, kb, ib, lf, gb, s0, out_gain, tril, pick)


def kernel(x_prompt, x_sample, cache_k, cache_v, state_hgrn, page_table, rel_bias_table, lb_logits,
           ffn1_norm, ffn1_gate, ffn1_up, ffn1_down, mix_norm, w_in, q_norm, k_norm, hgrn_out_norm,
           w_out, ffn2_norm, ffn2_gate, ffn2_up, ffn2_down):
    batch, seq, _ = x_prompt.shape
    db, dec_seq, _ = x_sample.shape
    depth, n_phys = cache_k.shape[:2]
    assert depth == 1 and lb_logits.shape[0] == 2
    past = page_table.shape[1] * PAGE_SIZE
    assert seq % MOBA_BLOCK == 0 and past % MOBA_BLOCK == 0 and dec_seq <= HG_CHUNK
    n_p, n_s = batch * seq, db * dec_seq

    bf = lambda w: w[0].astype(BF16)
    gain = lambda g: g[0].reshape(1, -1)
    ffn1 = (gain(ffn1_norm), bf(ffn1_gate), bf(ffn1_up), bf(ffn1_down))
    ffn2 = (gain(ffn2_norm), bf(ffn2_gate), bf(ffn2_up), bf(ffn2_down))
    wo = bf(w_out)
    w_in16 = bf(w_in)
    per_lane = lambda g: jnp.broadcast_to(jnp.tile(g[0], A_HEADS)[:, None], (A_WIDTH, 128))
    proj_w = (gain(mix_norm), w_in16[:, :3 * GROUP_W].T, w_in16[:, 3 * GROUP_W:],
              per_lane(q_norm), per_lane(k_norm), lb_logits)
    out_gain = hgrn_out_norm[0].reshape(1, -1)
    w_bias, sbt, ob = _bias_tables(rel_bias_table.T, seq, past, dec_seq)

    ys = _ffn(x_sample.reshape(n_s, D_MODEL), *ffn1, n_s)
    q_t, k_t, v_t, qb_s, kb_s, ib_s, lf_s, gb_s = _proj(ys, *proj_w, n_s, n_s)
    q_s, k_s, v_s = (a.T.reshape(db, dec_seq, A_WIDTH) for a in (q_t, k_t, v_t))
    pad_rows = lambda a, r: jnp.pad(a.reshape(db, dec_seq, -1), ((0, 0), (0, r - dec_seq), (0, 0)))
    pages = lambda c: c[0].transpose(0, 2, 3, 1).reshape(n_phys, A_WIDTH, PAGE_SIZE)
    sample_attn = functools.partial(
        _moba_sample, q_s, pad_rows(k_s, 8), pad_rows(v_s, 8), pages(cache_k), pages(cache_v), page_table, sbt, ob,
        bpg=math.gcd(past // MOBA_BLOCK, SAMPLE_BLOCKS_PER_STEP))

    tm = 512
    x_p = x_prompt.reshape(n_p, D_MODEL)
    if n_p // tm == db:
        y, attn_s = sample_attn(ffn=(x_p, *ffn1, tm))
    else:
        y, attn_s = _ffn(x_p, *ffn1, tm), sample_attn()
    q_t, kp_t, vp_t, qb, kb, ib, lf, gb = _proj(y, *proj_w, seq, tm)
    attn = _moba_prompt(q_t, kp_t, vp_t, w_bias.reshape(A_HEADS // 2, 2, -1), batch, seq)
    s0 = jnp.zeros((batch, B_HEADS, B_DIM, B_DIM), F32)
    rec, s_p = _hgrn(qb, kb, ib, lf, gb, s0, out_gain, batch, seq, math.gcd(seq, HGRN_TILE), HG_BLOCK // HG_CHUNK, 1)
    y_p = _ffn(y, *ffn2, tm, mix=(attn, rec, wo))
    heads_p = lambda a: a.reshape(batch, A_HEADS, A_HEAD_DIM, seq).transpose(0, 3, 1, 2)[None]

    blocked = lambda a: pad_rows(a, HG_BLOCK).reshape(db * HG_BLOCK, -1)
    rec, s_s = _hgrn(blocked(qb_s), blocked(kb_s), blocked(ib_s), blocked(lf_s), blocked(gb_s), state_hgrn[0],
                     out_gain, db, HG_BLOCK, HG_BLOCK, 1, B_HEADS)
    rec = rec.reshape(db, HG_BLOCK, B_WIDTH)[:, :dec_seq].reshape(n_s, B_WIDTH)
    y_s = _ffn(ys, *ffn2, n_s, mix=(attn_s.reshape(n_s, A_WIDTH), rec, wo))
    heads_s = lambda a: a.reshape(1, db, dec_seq, A_HEADS, A_HEAD_DIM)

    return (y_p.reshape(batch, seq, D_MODEL), y_s.reshape(db, dec_seq, D_MODEL),
            heads_p(kp_t), heads_p(vp_t), s_p[None],
            heads_s(k_s), heads_s(v_s), s_s[None])
```

```python
import functools
import math

import jax
import jax.numpy as jnp
from jax import lax
from jax.experimental import pallas as pl
from jax.experimental.pallas import tpu as pltpu

F32 = jnp.float32
BF16 = jnp.bfloat16

D_MODEL = 1024
A_HEADS = 8
A_HEAD_DIM = 64
A_WIDTH = A_HEADS * A_HEAD_DIM
B_HEADS = 4
B_DIM = 128
B_WIDTH = B_HEADS * B_DIM
GROUP_W = 512
N_GROUPS = 7
D_FF = 2816
MOBA_BLOCK = 256
MOBA_TOPK = 3
PAGE_SIZE = 128
N_BUCKETS = 32
MAX_EXACT = N_BUCKETS // 2
REL_MAX_DIST = 4096
HG_CHUNK = 16
HG_BLOCK = 128
NORM_EPS = 1e-6
NEG_INF = -1e30
ATTN_SCALE = A_HEAD_DIM ** -0.5
LOG2_E = math.log2(math.e)

FF_CHUNK = 256
SAMPLE_BLOCKS_PER_STEP = 8
MOBA_BLOCKS_PER_TRIP = 4
HGRN_TILE = 2048
VMEM_LIMIT = 48 * 1024 * 1024
VMEM_LIMIT_FUSED = 56 * 1024 * 1024


def _resident(shape):
    nd = len(shape)
    return pl.BlockSpec(shape, lambda *_: (0,) * nd, pipeline_mode=pl.Buffered(1))


def _params(semantics):
    return pltpu.CompilerParams(dimension_semantics=semantics, vmem_limit_bytes=VMEM_LIMIT)


def _dot(a, b):
    return jnp.dot(a, b, preferred_element_type=F32)


def _dot_nt(a, b):
    return lax.dot_general(a, b, (((1,), (1,)), ((), ())), preferred_element_type=F32)


def _split3(a):
    a1 = a.astype(BF16)
    r1 = a - a1.astype(F32)
    a2 = r1.astype(BF16)
    a3 = (r1 - a2.astype(F32)).astype(BF16)
    return a1, a2, a3


def _div_pow2(x, d):
    return lax.shift_right_logical(x, int(math.log2(d)))


def _rms_rows(x, gain):
    ms = jnp.mean(x * x, axis=-1, keepdims=True)
    return x * lax.rsqrt(ms + NORM_EPS) * gain


def _silu(x):
    return x * jax.nn.sigmoid(x)


def _ffn_chunks(h, wg_ref, wu_ref, wd_ref, acc, chunks):
    for j in chunks:
        cols = slice(j * FF_CHUNK, (j + 1) * FF_CHUNK)
        g = _dot(h, wg_ref[:, cols])
        u = _dot(h, wu_ref[:, cols])
        acc = acc + _dot((_silu(g) * u).astype(BF16), wd_ref[cols, :])
    return acc


def _ffn_kernel(*refs, with_mix):
    if with_mix:
        x_ref, attn_ref, rec_ref, wo_ref, gain_ref, wg_ref, wu_ref, wd_ref, o_ref = refs
        y = (x_ref[...] + _dot(attn_ref[...], wo_ref[:A_WIDTH, :])
             + _dot(rec_ref[...], wo_ref[A_WIDTH:, :]))
    else:
        x_ref, gain_ref, wg_ref, wu_ref, wd_ref, o_ref = refs
        y = x_ref[...]
    h = _rms_rows(y, gain_ref[...]).astype(BF16)
    acc = _ffn_chunks(h, wg_ref, wu_ref, wd_ref, jnp.zeros(y.shape, F32), range(D_FF // FF_CHUNK))
    o_ref[...] = y + 0.5 * acc


def _ffn_sample_kernel(pt_ref, x_ref, gain_ref, wg_ref, wu_ref, wd_ref, q_ref, kn_ref, vn_ref, sbt_ref, ob_ref,
                       ck_ref, cv_ref, y_ref, attn_ref, *scratch, nbk, dec_seq, bpg):
    stages = _sample_stages(pl.program_id(0), pl.num_programs(0), pt_ref, q_ref, kn_ref, vn_ref, sbt_ref, ob_ref,
                            ck_ref, cv_ref, attn_ref, *scratch, nbk=nbk, dec_seq=dec_seq, bpg=bpg)
    y = x_ref[...]
    h = _rms_rows(y, gain_ref[...]).astype(BF16)
    acc = jnp.zeros(y.shape, F32)
    n_chunks = D_FF // FF_CHUNK
    cuts = [k * n_chunks // len(stages) for k in range(len(stages) + 1)]
    for k, stage in enumerate(stages):
        stage()
        acc = _ffn_chunks(h, wg_ref, wu_ref, wd_ref, acc, range(cuts[k], cuts[k + 1]))
    y_ref[...] = y + 0.5 * acc


def _ffn(x, gain, wg, wu, wd, tm, mix=None):
    n = x.shape[0]
    row = lambda w: pl.BlockSpec((tm, w), lambda i: (i, 0))
    args, specs = [x], [row(D_MODEL)]
    if mix is not None:
        attn, rec, wo = mix
        args += [attn, rec, wo]
        specs += [row(A_WIDTH), row(B_WIDTH), _resident(wo.shape)]
    args += [gain, wg, wu, wd]
    specs += [_resident(gain.shape), _resident(wg.shape), _resident(wu.shape), _resident(wd.shape)]
    return pl.pallas_call(
        functools.partial(_ffn_kernel, with_mix=mix is not None),
        out_shape=jax.ShapeDtypeStruct((n, D_MODEL), F32),
        grid=(n // tm,),
        in_specs=specs,
        out_specs=row(D_MODEL),
        compiler_params=_params(("parallel",)),
        name="ffn_mix" if mix is not None else "ffn",
    )(*args)


def _proj_kernel(y_ref, gain_ref, wat_ref, wb_ref, qg_ref, kg_ref, lbl_ref,
                 q_ref, k_ref, v_ref, qb_ref, kb_ref, ib_ref, lf_ref, gb_ref):
    h = _rms_rows(y_ref[...], gain_ref[...]).astype(BF16)
    tm = h.shape[0]

    def group_t(i):
        return _dot_nt(wat_ref[i * GROUP_W:(i + 1) * GROUP_W, :], h)

    def group(i):
        return _dot(h, wb_ref[:, i * GROUP_W:(i + 1) * GROUP_W])

    def head_norm(z, g):
        out = []
        for hd in range(A_HEADS):
            zh = z[hd * A_HEAD_DIM:(hd + 1) * A_HEAD_DIM, :]
            out.append(zh * lax.rsqrt(jnp.mean(zh * zh, axis=0, keepdims=True) + NORM_EPS))
        return jnp.concatenate(out, axis=0) * jnp.concatenate([g] * pl.cdiv(tm, 128), axis=1)[:, :tm]

    q_ref[...] = head_norm(group_t(0), qg_ref[...]) * ATTN_SCALE
    k_ref[...] = head_norm(group_t(1), kg_ref[...])
    v_ref[...] = group_t(2)
    qb_ref[...] = _silu(group(0))
    lbl = lbl_ref[...]
    e = jnp.exp(lbl - jnp.max(lbl, axis=0, keepdims=True))
    lb = e[0:1, :] / jnp.sum(e, axis=0, keepdims=True)
    forget = lb + (1.0 - lb) * jax.nn.sigmoid(group(1))
    lf_ref[...] = jnp.log(forget)
    kb_ref[...] = 1.0 - forget
    ib_ref[...] = group(2)
    gb_ref[...] = group(3)


def _proj(y, gain, wat, wb, qg, kg, lbl, seq, tm):
    n = y.shape[0]
    nt = seq // tm
    row = lambda w: pl.BlockSpec((tm, w), lambda i: (i, 0))
    col = pl.BlockSpec((GROUP_W, tm), lambda i: (i // nt, i % nt))
    out_t = jax.ShapeDtypeStruct((n // seq * GROUP_W, seq), F32)
    out = jax.ShapeDtypeStruct((n, GROUP_W), F32)
    return pl.pallas_call(
        _proj_kernel,
        out_shape=(out_t,) * 3 + (out,) * 5,
        grid=(n // tm,),
        in_specs=[row(D_MODEL)] + [_resident(a.shape) for a in (gain, wat, wb, qg, kg, lbl)],
        out_specs=(col,) * 3 + (row(GROUP_W),) * 5,
        compiler_params=_params(("parallel",)),
        name="proj",
    )(y, gain, wat, wb, qg, kg, lbl)


def _rel_bucket(n):
    nf = jnp.maximum(n, 1).astype(F32)
    large = MAX_EXACT + (jnp.log(nf / MAX_EXACT) / math.log(REL_MAX_DIST / MAX_EXACT)
                         * (N_BUCKETS - MAX_EXACT)).astype(jnp.int32)
    large = jnp.minimum(large, N_BUCKETS - 1)
    return jnp.where(n < MAX_EXACT, n, large)


def _bias_kernel(tab_ref, w_ref, sbt_ref, ob_ref, *, past, dec_seq):
    tab = tab_ref[...]
    tab_rows = jnp.concatenate([tab] * dec_seq, axis=0)

    def lookup(bucket, table):
        out = jnp.zeros(bucket.shape, F32)
        for b in range(N_BUCKETS):
            out = jnp.where(bucket == b, table[:, b:b + 1], out)
        return out

    c = lax.broadcasted_iota(jnp.int32, w_ref.shape, 1)
    w_ref[...] = lookup(_rel_bucket(jnp.maximum(c - MOBA_BLOCK, 0)), tab)
    t = _div_pow2(lax.broadcasted_iota(jnp.int32, sbt_ref.shape, 0), A_HEADS)
    kpos = lax.broadcasted_iota(jnp.int32, sbt_ref.shape, 1)
    sbt_ref[...] = lookup(_rel_bucket(jnp.maximum(past + t - kpos, 0)), tab_rows)
    t = _div_pow2(lax.broadcasted_iota(jnp.int32, ob_ref.shape, 0), A_HEADS)
    s = lax.broadcasted_iota(jnp.int32, ob_ref.shape, 1)
    ob_ref[...] = lookup(_rel_bucket(jnp.maximum(t - s, 0)), tab_rows)


def _bias_tables(tab_t, seq, past, dec_seq):
    rows = dec_seq * A_HEADS
    return pl.pallas_call(
        functools.partial(_bias_kernel, past=past, dec_seq=dec_seq),
        out_shape=(jax.ShapeDtypeStruct((A_HEADS, seq + MOBA_BLOCK), F32),
                   jax.ShapeDtypeStruct((rows, past), F32),
                   jax.ShapeDtypeStruct((rows, 128), F32)),
        name="rel_bias",
    )(tab_t)


def _top_k_mask(gate, k, axis):
    index = lax.broadcasted_iota(jnp.int32, gate.shape, axis).astype(F32)
    sel = jnp.zeros(gate.shape, F32)
    for _ in range(k):
        mx = jnp.max(gate, axis=axis, keepdims=True)
        first = jnp.min(jnp.where(gate == mx, index, float(gate.shape[axis])), axis=axis, keepdims=True)
        chosen = index == first
        sel = jnp.where(chosen & (mx > -jnp.inf), 1.0, sel)
        gate = jnp.where(chosen, -jnp.inf, gate)
    return sel


def _moba_prompt_kernel(q_ref, k_ref, v_ref, w_ref, o_ref, kf_ref, vw_ref, km_ref, tt_ref, s_ref, qp_ref,
                        mx_ref, acc_ref, *, nb):
    b = pl.program_id(1)
    i = pl.program_id(2)
    blk = MOBA_BLOCK

    @pl.when((b == 0) & (i == 0))
    def _build_bias_tiles():
        qry = lax.broadcasted_iota(jnp.int32, (blk, blk), 0)
        key = lax.broadcasted_iota(jnp.int32, (blk, blk), 1)
        for hh in range(2):
            for d in range(nb):
                win = jnp.broadcast_to(w_ref[0, hh:hh + 1, d * blk:(d + 2) * blk], (blk, 2 * blk))
                tile = (pltpu.roll(win, 0, 1, stride=1, stride_axis=0)[:, blk:] * LOG2_E).T
                tt_ref[hh, d] = jnp.where(qry >= key, tile, NEG_INF) if d == 0 else tile

    @pl.when(i == 0)
    def _prepare_sequence():
        slot = lax.broadcasted_iota(jnp.int32, (128, blk), 0)
        ones = jnp.ones((blk, 128), F32)
        for n in range(nb):
            cols = slice(n * blk, (n + 1) * blk)
            k_t = k_ref[:, cols]
            kf_ref[n] = jnp.concatenate([k_t, jnp.where(slot == n, 1.0, 0.0)], axis=0).astype(BF16)
            km_ref[n:n + 1, :] = jnp.mean(k_t.T, axis=0, keepdims=True)
            vw_ref[n] = jnp.concatenate([v_ref[:, cols].T, ones], axis=1).astype(BF16)
        km = km_ref[...]
        km_lane = lax.broadcasted_iota(jnp.int32, km.shape, 1)
        feat = lax.broadcasted_iota(jnp.int32, (128, blk), 0)
        blk_id = lax.broadcasted_iota(jnp.int32, (nb, blk), 0)
        for t in range(nb):
            q_t = q_ref[:, t * blk:(t + 1) * blk] * LOG2_E
            for hh in range(2):
                lo = hh * A_HEAD_DIM
                q_h = jnp.where((feat >= lo) & (feat < lo + A_HEAD_DIM), q_t, 0.0)
                km_h = jnp.where((km_lane >= lo) & (km_lane < lo + A_HEAD_DIM), km, 0.0)
                k1, k2, _ = _split3(km_h)
                q1, q2, _ = _split3(q_h)
                gate = _dot(k1, q1) + _dot(k1, q2) + _dot(k2, q1)
                gate = jnp.where(blk_id < t, gate, -jnp.inf)
                attends = jnp.where(blk_id == t, 1.0, _top_k_mask(gate, MOBA_TOPK, 0))
                penalty = jnp.concatenate([jnp.where(attends > 0.0, 0.0, NEG_INF),
                                           jnp.zeros((128 - nb, blk), F32)], axis=0)
                qp_ref[t, hh] = jnp.concatenate([q_h.T, penalty.T], axis=1).astype(BF16)

    qp = jnp.concatenate([qp_ref[i, 0], qp_ref[i, 1]], axis=0)
    mx_ref[...] = jnp.full(mx_ref.shape, -jnp.inf, F32)
    acc_ref[...] = jnp.zeros(acc_ref.shape, F32)

    per_trip = math.gcd(nb, MOBA_BLOCKS_PER_TRIP)
    full_trips = _div_pow2(i + 1, per_trip)
    left_over = (i + 1) & (per_trip - 1)

    def walk(body):
        def trip(j, carry):
            body(per_trip * j, per_trip)
            return carry
        lax.fori_loop(0, full_trips, trip, 0)
        for left in range(1, per_trip):
            count = left if 2 * left <= per_trip else per_trip
            pl.when(left_over == left)(functools.partial(body, per_trip * full_trips, count))

    def logits(first, count):
        raw = [_dot(qp, kf_ref[first + u]) for u in range(count)]
        mx = mx_ref[...]
        for u, r in enumerate(raw):
            d = jnp.maximum(i - (first + u), 0)
            s = r + jnp.concatenate([tt_ref[0, d], tt_ref[1, d]], axis=0)
            s_ref[first + u] = s
            mx = jnp.maximum(mx, jnp.maximum(s[:, :128], s[:, 128:]))
        mx_ref[...] = mx

    walk(logits)
    m = jnp.broadcast_to(jnp.max(mx_ref[...], axis=1, keepdims=True), (2 * blk, blk))

    def values(first, count):
        p = jnp.concatenate([jnp.exp2(s_ref[first + u] - m).astype(BF16) for u in range(count)], axis=1)
        w = vw_ref[pl.ds(first, count)].reshape(count * blk, 256)
        acc_ref[...] = acc_ref[...] + _dot(p, w)

    walk(values)
    acc = acc_ref[...]
    first_head = lax.broadcasted_iota(jnp.int32, (blk, 128), 1) < A_HEAD_DIM
    num = jnp.where(first_head, acc[:blk, :128], acc[blk:, :128])
    den = jnp.where(first_head, acc[:blk, 128:], acc[blk:, 128:])
    o_ref[...] = (num / den).astype(o_ref.dtype)


def _moba_prompt(q_t, k_t, v_t, w_pairs, batch, seq):
    nb = seq // MOBA_BLOCK
    pairs = A_HEADS // 2
    whole = pl.BlockSpec((128, seq), lambda p, b, i: (b * pairs + p, 0))
    return pl.pallas_call(
        functools.partial(_moba_prompt_kernel, nb=nb),
        out_shape=jax.ShapeDtypeStruct((batch * seq, A_WIDTH), BF16),
        grid=(pairs, batch, nb),
        in_specs=[whole, whole, whole,
                  pl.BlockSpec((1, 2, seq + MOBA_BLOCK), lambda p, b, i: (p, 0, 0))],
        out_specs=pl.BlockSpec((MOBA_BLOCK, 128), lambda p, b, i: (b * nb + i, p)),
        scratch_shapes=[pltpu.VMEM((nb, 256, MOBA_BLOCK), BF16),
                        pltpu.VMEM((nb, MOBA_BLOCK, 256), BF16),
                        pltpu.VMEM((nb, 128), F32),
                        pltpu.VMEM((2, nb, MOBA_BLOCK, MOBA_BLOCK), F32),
                        pltpu.VMEM((nb, 2 * MOBA_BLOCK, MOBA_BLOCK), F32),
                        pltpu.VMEM((nb, 2, MOBA_BLOCK, 256), BF16),
                        pltpu.VMEM((2 * MOBA_BLOCK, 128), F32),
                        pltpu.VMEM((2 * MOBA_BLOCK, 256), F32)],
        compiler_params=_params(("arbitrary", "arbitrary", "arbitrary")),
        name="moba_prompt",
    )(q_t, k_t, v_t, w_pairs)


def _moba_sample_kernel(*refs, nbk, dec_seq, bpg):
    for stage in _sample_stages(pl.program_id(0), pl.num_programs(0), *refs, nbk=nbk, dec_seq=dec_seq, bpg=bpg):
        stage()


def _sample_stages(b, n_seq, pt_ref, q_ref, kn_ref, vn_ref, sbt_ref, ob_ref, ck_ref, cv_ref, o_ref,
                   kbuf, vbuf, sem, m_sc, l_sc, acc_sc, g_sc, *, nbk, dec_seq, bpg):
    n_groups = nbk // bpg
    ppg = 2 * bpg
    rows = dec_seq * A_HEADS
    head_of_row = lax.broadcasted_iota(jnp.int32, (A_HEADS, A_WIDTH), 0)
    head_of_lane = _div_pow2(lax.broadcasted_iota(jnp.int32, (A_HEADS, A_WIDTH), 1), A_HEAD_DIM)
    head_mask = head_of_row == head_of_lane
    wide = lambda a: jnp.broadcast_to(a, (rows, 128))

    def group_copies(seq, g, slot):
        out = []
        for r in range(ppg):
            page = pt_ref[seq * (n_groups * ppg) + g * ppg + r]
            out.append(pltpu.make_async_copy(ck_ref.at[page], kbuf.at[slot, r], sem.at[slot, 0]))
            out.append(pltpu.make_async_copy(cv_ref.at[page], vbuf.at[slot, r], sem.at[slot, 1]))
        return out

    def queries():
        return jnp.concatenate(
            [jnp.where(head_mask, jnp.broadcast_to(q_ref[0, t:t + 1, :], head_mask.shape), 0.0)
             for t in range(dec_seq)], axis=0).astype(BF16)

    live = {}

    def group_keys(g):
        slot = (b * n_groups + g) & 1
        if g == 0:
            @pl.when(b == 0)
            def _first_group():
                for cp in group_copies(0, 0, 0):
                    cp.start()
        if g + 1 < n_groups:
            for cp in group_copies(b, g + 1, 1 - slot):
                cp.start()
        else:
            @pl.when(b + 1 < n_seq)
            def _next_sequence():
                for cp in group_copies(b + 1, 0, 1 - slot):
                    cp.start()
        for cp in group_copies(b, g, slot):
            cp.wait()
        q16 = queries()
        live["raw"] = jnp.concatenate(
            [_dot(q16, jnp.concatenate([kbuf[slot, 2 * r], kbuf[slot, 2 * r + 1]], axis=1).astype(BF16))
             for r in range(bpg)], axis=1)

    per_block = lambda a: [a[:, r * MOBA_BLOCK:(r + 1) * MOBA_BLOCK] for r in range(bpg)]

    def group_softmax(g):
        width = bpg * MOBA_BLOCK
        s = per_block(live["raw"] + sbt_ref[:, g * width:(g + 1) * width])
        live["m"] = [jnp.max(a, axis=1, keepdims=True) for a in s]
        live["p"] = [jnp.exp(a - mr) for a, mr in zip(s, live["m"])]

    def group_values(g):
        slot = (b * n_groups + g) & 1
        raw, m, p = per_block(live.pop("raw")), live.pop("m"), live.pop("p")
        for r in range(bpg):
            n = g * bpg + r
            vblk = jnp.concatenate([vbuf[slot, 2 * r], vbuf[slot, 2 * r + 1]], axis=1).astype(BF16)
            m_sc[n] = wide(m[r])
            l_sc[n] = wide(jnp.sum(p[r], axis=1, keepdims=True))
            acc_sc[n] = _dot_nt(p[r].astype(BF16), vblk)
            g_sc[n] = wide(jnp.sum(raw[r], axis=1, keepdims=True) * (1.0 / MOBA_BLOCK))

    def merge():
        q16 = queries()
        gates = [g_sc[n] for n in range(nbk)]
        picked = [jnp.zeros((rows, 128), jnp.bool_)] * nbk
        for _ in range(min(MOBA_TOPK, nbk)):
            mx = functools.reduce(jnp.maximum, gates)
            first = functools.reduce(
                jnp.minimum, [jnp.where(g == mx, float(n), float(nbk)) for n, g in enumerate(gates)])
            for n in range(nbk):
                chosen = first == float(n)
                picked[n] = picked[n] | (chosen & (mx > -jnp.inf))
                gates[n] = jnp.where(chosen, -jnp.inf, gates[n])
        zeros = jnp.zeros((128 - kn_ref.shape[1], A_WIDTH), F32)
        k_own = jnp.concatenate([kn_ref[0], zeros], axis=0).astype(BF16)
        v_own = jnp.concatenate([vn_ref[0], zeros], axis=0).astype(BF16)
        s = _dot_nt(q16, k_own) + ob_ref[...]
        t_of_row = _div_pow2(lax.broadcasted_iota(jnp.int32, s.shape, 0), A_HEADS)
        s_idx = lax.broadcasted_iota(jnp.int32, s.shape, 1)
        s = jnp.where(s_idx <= t_of_row, s, NEG_INF)
        m_own = wide(jnp.max(s, axis=1, keepdims=True))
        p = jnp.exp(s - m_own)
        l_own = wide(jnp.sum(p, axis=1, keepdims=True))
        acc_own = _dot(p.astype(BF16), v_own)
        m_all = m_own
        for n in range(nbk):
            m_all = jnp.maximum(m_all, jnp.where(picked[n], m_sc[n], -jnp.inf))
        lanes4 = lambda a: jnp.concatenate([a] * (A_WIDTH // 128), axis=1)
        w = jnp.exp(m_own - m_all)
        l_all = w * l_own
        acc = lanes4(w) * acc_own
        for n in range(nbk):
            w = jnp.where(picked[n], jnp.exp(m_sc[n] - m_all), 0.0)
            l_all = l_all + w * l_sc[n]
            acc = acc + lanes4(w) * acc_sc[n]
        res = acc / lanes4(l_all)
        out = [jnp.sum(jnp.where(head_mask, res[t * A_HEADS:(t + 1) * A_HEADS, :], 0.0),
                       axis=0, keepdims=True) for t in range(dec_seq)]
        o_ref[0] = jnp.concatenate(out, axis=0).astype(o_ref.dtype)

    return [functools.partial(stage, g) for g in range(n_groups)
            for stage in (group_keys, group_softmax, group_values)] + [merge]


def _moba_sample(q, k_new, v_new, cache_k, cache_v, page_table, sbt, ob, bpg, ffn=None):
    db, dec_seq, _ = q.shape
    n_pages = page_table.shape[1]
    nbk = n_pages * PAGE_SIZE // MOBA_BLOCK
    rows = dec_seq * A_HEADS
    assert MOBA_BLOCK == 2 * PAGE_SIZE and nbk % bpg == 0
    per_seq = lambda b, pt: (b, 0, 0)
    whole = lambda a: pl.BlockSpec(a.shape, lambda b, pt: (0,) * a.ndim, pipeline_mode=pl.Buffered(1))
    sample_specs = [pl.BlockSpec((1, dec_seq, A_WIDTH), per_seq),
                    pl.BlockSpec((1, k_new.shape[1], A_WIDTH), per_seq),
                    pl.BlockSpec((1, v_new.shape[1], A_WIDTH), per_seq),
                    whole(sbt), whole(ob),
                    pl.BlockSpec(memory_space=pl.ANY), pl.BlockSpec(memory_space=pl.ANY)]
    sample_args = (q, k_new, v_new, sbt, ob, cache_k, cache_v)
    attn_spec = pl.BlockSpec((1, dec_seq, A_WIDTH), per_seq)
    attn_shape = jax.ShapeDtypeStruct((db, dec_seq, A_WIDTH), BF16)
    scratch = [pltpu.VMEM((2, 2 * bpg, A_WIDTH, PAGE_SIZE), F32),
               pltpu.VMEM((2, 2 * bpg, A_WIDTH, PAGE_SIZE), F32),
               pltpu.SemaphoreType.DMA((2, 2)),
               pltpu.VMEM((nbk, rows, 128), F32),
               pltpu.VMEM((nbk, rows, 128), F32),
               pltpu.VMEM((nbk, rows, A_WIDTH), F32),
               pltpu.VMEM((nbk, rows, 128), F32)]
    static = dict(nbk=nbk, dec_seq=dec_seq, bpg=bpg)
    if ffn is None:
        return pl.pallas_call(
            functools.partial(_moba_sample_kernel, **static),
            out_shape=attn_shape,
            grid_spec=pltpu.PrefetchScalarGridSpec(num_scalar_prefetch=1, grid=(db,), in_specs=sample_specs,
                                                   out_specs=attn_spec, scratch_shapes=scratch),
            compiler_params=_params(("arbitrary",)),
            name="moba_sample",
        )(page_table.reshape(-1), *sample_args)
    x, gain, wg, wu, wd, tm = ffn
    assert x.shape[0] == db * tm
    row = pl.BlockSpec((tm, D_MODEL), lambda b, pt: (b, 0))
    return pl.pallas_call(
        functools.partial(_ffn_sample_kernel, **static),
        out_shape=(jax.ShapeDtypeStruct(x.shape, F32), attn_shape),
        grid_spec=pltpu.PrefetchScalarGridSpec(
            num_scalar_prefetch=1, grid=(db,),
            in_specs=[row] + [whole(a) for a in (gain, wg, wu, wd)] + sample_specs,
            out_specs=(row, attn_spec), scratch_shapes=scratch),
        compiler_params=pltpu.CompilerParams(dimension_semantics=("arbitrary",),
                                             vmem_limit_bytes=VMEM_LIMIT_FUSED),
        name="ffn_sample",
    )(page_table.reshape(-1), x, gain, wg, wu, wd, *sample_args)


def _hgrn_kernel(q_ref, k_ref, v_ref, g_ref, gate_ref, s0_ref, og_ref, tril_ref, pick_ref, rec_ref, sfin_ref,
                 st_ref, cum_ref, *, n_chunks):
    tb = pl.program_id(2)
    heads = range(q_ref.shape[1] // B_DIM)

    @pl.when(tb == 0)
    def _load_state():
        for h in heads:
            st_ref[h] = s0_ref[0, h].T

    head = lambda ref, h: ref.at[:, h * B_DIM:(h + 1) * B_DIM]
    used = n_chunks * HG_CHUNK
    work, terms = [], []
    for h in heads:
        o_blocks, off_blocks, term_blocks, v16_blocks, st_ref[h] = _hgrn_scores(
            head(q_ref, h), head(k_ref, h), head(v_ref, h), head(g_ref, h), tril_ref, st_ref[h], head(cum_ref, h),
            n_chunks)
        work.append((o_blocks, off_blocks, v16_blocks))
        terms += term_blocks
    half = (len(terms) + 1) // 2
    inside = [_dot(jnp.concatenate(part, axis=0), pick_ref[...]) for part in (terms[:half], terms[half:]) if part]
    inside = [a[at * used:(at + 1) * used] for a in inside for at in range(a.shape[0] // used)]
    for h in heads:
        n_blocks = len(work[h][0])
        _hgrn_finish(*work[h], inside[h * n_blocks:(h + 1) * n_blocks], head(gate_ref, h), head(og_ref, h),
                     head(rec_ref, h), n_chunks)

    @pl.when(tb == pl.num_programs(2) - 1)
    def _store_state():
        for h in heads:
            sfin_ref[0, h] = st_ref[h].T


def _hgrn_scores(q_ref, k_ref, v_ref, g_ref, tril_ref, st, cum_ref, n_chunks):
    n_blocks = q_ref.shape[0] // HG_BLOCK
    used = n_chunks * HG_CHUNK
    blocks = [slice(bi * HG_BLOCK, (bi + 1) * HG_BLOCK) for bi in range(n_blocks)]

    tril = tril_ref[...]
    g_all = jnp.concatenate([g_ref[rows, :] for rows in blocks], axis=1) * LOG2_E
    g1, g2, g3 = _split3(g_all)
    cum_all = _dot(tril, g1) + _dot(tril, g2) + _dot(tril, g3)

    o_blocks, off_blocks, term_blocks, v16_blocks = [], [], [], []
    for bi, rows in enumerate(blocks):
        q, k, v = q_ref[rows, :], k_ref[rows, :], v_ref[rows, :]
        cum = cum_all[:, bi * B_DIM:(bi + 1) * B_DIM]
        last = cum[HG_BLOCK - 1:HG_BLOCK, :]
        v16_blocks.append(v.astype(BF16))
        o_blocks.append(_dot_nt((q * jnp.exp2(cum)).astype(BF16), st.astype(BF16)))
        st = st * jnp.exp2(last) + _dot(v.T.astype(BF16), (k * jnp.exp2(last - cum)).astype(BF16))
        parts = [jnp.zeros((HG_CHUNK, B_DIM), F32)]
        for c in range(1, n_chunks):
            lo = c * HG_CHUNK
            r = cum[lo - 1:lo, :]
            k_c = (k[:lo] * jnp.exp2(r - cum[:lo])).astype(BF16)
            k_c = jnp.concatenate([k_c, jnp.zeros((HG_BLOCK - lo, B_DIM), BF16)], axis=0)
            q_c = (q[lo:lo + HG_CHUNK] * jnp.exp2(cum[lo:lo + HG_CHUNK] - r)).astype(BF16)
            parts.append(_dot_nt(q_c, k_c))
        off_blocks.append(jnp.concatenate(parts, axis=0))
        cum_ref[rows, :] = cum

    assert used == HG_BLOCK or n_blocks == 1
    n_rows = n_blocks * used
    chunk_rows = lambda ref, s: jnp.concatenate(
        [jnp.broadcast_to(ref[c * HG_CHUNK + s:c * HG_CHUNK + s + 1, :], (HG_CHUNK, B_DIM))
         for c in range(n_rows // HG_CHUNK)], axis=0)
    row_in_chunk = lax.broadcasted_iota(jnp.int32, (n_rows, B_DIM), 0) & (HG_CHUNK - 1)
    q_rows, cum_rows = q_ref[:n_rows, :], cum_ref[:n_rows, :]
    terms = []
    for s in range(HG_CHUNK):
        decay = jnp.exp2(jnp.where(row_in_chunk >= s, cum_rows - chunk_rows(cum_ref, s), -jnp.inf))
        terms.append((q_rows * decay * chunk_rows(k_ref, s)).astype(BF16))
    terms = jnp.concatenate(terms, axis=1)
    term_blocks = [terms[bi * used:(bi + 1) * used] for bi in range(n_blocks)]
    return o_blocks, off_blocks, term_blocks, v16_blocks, st


def _hgrn_finish(o_blocks, off_blocks, v16_blocks, inside_blocks, gate_ref, og_ref, rec_ref, n_chunks):
    used = n_chunks * HG_CHUNK
    row = lax.broadcasted_iota(jnp.int32, (used, B_DIM), 0)
    col = lax.broadcasted_iota(jnp.int32, (used, B_DIM), 1)
    same_chunk = _div_pow2(row, HG_CHUNK) == _div_pow2(col, HG_CHUNK)
    ones = jnp.ones((B_DIM, B_DIM), BF16)
    out = []
    for o, off, v16, inside in zip(o_blocks, off_blocks, v16_blocks, inside_blocks):
        score = off + jnp.where(same_chunk, inside, 0.0)
        intra = _dot(score.astype(BF16), v16)
        if used < HG_BLOCK:
            intra = jnp.concatenate([intra, jnp.zeros((HG_BLOCK - used, B_DIM), F32)], axis=0)
        out.append(o + intra)
    o = jnp.concatenate(out, axis=0)
    ms = _dot((o * o).astype(BF16), ones) * (1.0 / B_DIM)
    gate = gate_ref[...]
    rec_ref[...] = ((o * lax.rsqrt(ms + NORM_EPS) * og_ref[...]) * _silu(gate)).astype(rec_ref.dtype)


def _hgrn(qb, kb, ib, lf, gb, s0, out_gain, batch, seq, tile, n_chunks, heads_per_step):
    nt = seq // tile
    width = heads_per_step * B_DIM
    tok = pl.BlockSpec((tile, width), lambda b, h, t: (b * nt + t, h))
    state = pl.BlockSpec((1, heads_per_step, B_DIM, B_DIM), lambda b, h, t: (b, h, 0, 0))
    idx = jnp.arange(HG_BLOCK)
    tril = (idx[:, None] >= idx[None, :]).astype(BF16)
    pick = (jnp.repeat(jnp.arange(HG_CHUNK), B_DIM)[:, None] == (idx % HG_CHUNK)[None, :]).astype(BF16)
    return pl.pallas_call(
        functools.partial(_hgrn_kernel, n_chunks=n_chunks),
        out_shape=(jax.ShapeDtypeStruct((batch * seq, B_WIDTH), BF16),
                   jax.ShapeDtypeStruct(s0.shape, F32)),
        grid=(batch, B_HEADS // heads_per_step, nt),
        in_specs=[tok, tok, tok, tok, tok, state, pl.BlockSpec((1, width), lambda b, h, t: (0, h)),
                  _resident(tril.shape), _resident(pick.shape)],
        out_specs=(tok, state),
        scratch_shapes=[pltpu.VMEM((heads_per_step, B_DIM, B_DIM), F32),
                        pltpu.VMEM((tile, width), F32)],
        compiler_params=_params(("arbitrary", "arbitrary", "arbitrary")),
        name="hgrn",
    )(qb, kb, ib, lf, gb, s0, out_gain, tril, pick)


def kernel(x_prompt, x_sample, cache_k, cache_v, state_hgrn, page_table, rel_bias_table, lb_logits,
           ffn1_norm, ffn1_gate, ffn1_up, ffn1_down, mix_norm, w_in, q_norm, k_norm, hgrn_out_norm,
           w_out, ffn2_norm, ffn2_gate, ffn2_up, ffn2_down):
    batch, seq, _ = x_prompt.shape
    db, dec_seq, _ = x_sample.shape
    depth, n_phys = cache_k.shape[:2]
    assert depth == 1 and lb_logits.shape[0] == 2
    past = page_table.shape[1] * PAGE_SIZE
    assert seq % MOBA_BLOCK == 0 and past % MOBA_BLOCK == 0 and dec_seq <= HG_CHUNK
    n_p, n_s = batch * seq, db * dec_seq

    bf = lambda w: w[0].astype(BF16)
    gain = lambda g: g[0].reshape(1, -1)
    ffn1 = (gain(ffn1_norm), bf(ffn1_gate), bf(ffn1_up), bf(ffn1_down))
    ffn2 = (gain(ffn2_norm), bf(ffn2_gate), bf(ffn2_up), bf(ffn2_down))
    wo = bf(w_out)
    w_in16 = bf(w_in)
    per_lane = lambda g: jnp.broadcast_to(jnp.tile(g[0], A_HEADS)[:, None], (A_WIDTH, 128))
    proj_w = (gain(mix_norm), w_in16[:, :3 * GROUP_W].T, w_in16[:, 3 * GROUP_W:],
              per_lane(q_norm), per_lane(k_norm), lb_logits)
    out_gain = hgrn_out_norm[0].reshape(1, -1)
    w_bias, sbt, ob = _bias_tables(rel_bias_table.T, seq, past, dec_seq)

    ys = _ffn(x_sample.reshape(n_s, D_MODEL), *ffn1, n_s)
    q_t, k_t, v_t, qb_s, kb_s, ib_s, lf_s, gb_s = _proj(ys, *proj_w, n_s, n_s)
    q_s, k_s, v_s = (a.T.reshape(db, dec_seq, A_WIDTH) for a in (q_t, k_t, v_t))
    pad_rows = lambda a, r: jnp.pad(a.reshape(db, dec_seq, -1), ((0, 0), (0, r - dec_seq), (0, 0)))
    pages = lambda c: c[0].transpose(0, 2, 3, 1).reshape(n_phys, A_WIDTH, PAGE_SIZE)
    sample_attn = functools.partial(
        _moba_sample, q_s, pad_rows(k_s, 8), pad_rows(v_s, 8), pages(cache_k), pages(cache_v), page_table, sbt, ob,
        bpg=math.gcd(past // MOBA_BLOCK, SAMPLE_BLOCKS_PER_STEP))

    tm = 512
    x_p = x_prompt.reshape(n_p, D_MODEL)
    if n_p // tm == db:
        y, attn_s = sample_attn(ffn=(x_p, *ffn1, tm))
    else:
        y, attn_s = _ffn(x_p, *ffn1, tm), sample_attn()
    q_t, kp_t, vp_t, qb, kb, ib, lf, gb = _proj(y, *proj_w, seq, tm)
    attn = _moba_prompt(q_t, kp_t, vp_t, w_bias.reshape(A_HEADS // 2, 2, -1), batch, seq)
    s0 = jnp.zeros((batch, B_HEADS, B_DIM, B_DIM), F32)
    rec, s_p = _hgrn(qb, kb, ib, lf, gb, s0, out_gain, batch, seq, math.gcd(seq, HGRN_TILE), HG_BLOCK // HG_CHUNK, 1)
    y_p = _ffn(y, *ffn2, tm, mix=(attn, rec, wo))
    heads_p = lambda a: a.reshape(batch, A_HEADS, A_HEAD_DIM, seq).transpose(0, 3, 1, 2)[None]

    blocked = lambda a: pad_rows(a, HG_BLOCK).reshape(db * HG_BLOCK, -1)
    rec, s_s = _hgrn(blocked(qb_s), blocked(kb_s), blocked(ib_s), blocked(lf_s), blocked(gb_s), state_hgrn[0],
                     out_gain, db, HG_BLOCK, HG_BLOCK, 1, B_HEADS)
    rec = rec.reshape(db, HG_BLOCK, B_WIDTH)[:, :dec_seq].reshape(n_s, B_WIDTH)
    y_s = _ffn(ys, *ffn2, n_s, mix=(attn_s.reshape(n_s, A_WIDTH), rec, wo))
    heads_s = lambda a: a.reshape(1, db, dec_seq, A_HEADS, A_HEAD_DIM)

    return (y_p.reshape(batch, seq, D_MODEL), y_s.reshape(db, dec_seq, D_MODEL),
            heads_p(kp_t), heads_p(vp_t), s_p[None],
            heads_s(k_s), heads_s(v_s), s_s[None])
```

```python
import functools
import math

import jax
import jax.numpy as jnp
from jax import lax
from jax.experimental import pallas as pl
from jax.experimental.pallas import tpu as pltpu

F32 = jnp.float32
BF16 = jnp.bfloat16

D_MODEL = 1024
A_HEADS = 8
A_HEAD_DIM = 64
A_WIDTH = A_HEADS * A_HEAD_DIM
B_HEADS = 4
B_DIM = 128
B_WIDTH = B_HEADS * B_DIM
GROUP_W = 512
N_GROUPS = 7
D_FF = 2816
MOBA_BLOCK = 256
MOBA_TOPK = 3
PAGE_SIZE = 128
N_BUCKETS = 32
MAX_EXACT = N_BUCKETS // 2
REL_MAX_DIST = 4096
HG_CHUNK = 16
HG_BLOCK = 128
NORM_EPS = 1e-6
NEG_INF = -1e30
ATTN_SCALE = A_HEAD_DIM ** -0.5
LOG2_E = math.log2(math.e)

FF_CHUNK = 256
SAMPLE_BLOCKS_PER_STEP = 8
MOBA_BLOCKS_PER_TRIP = 4
HGRN_TILE = 2048
MOBA_ONES_ROWS = 16
CAST_ROWS = 256
VMEM_LIMIT = 48 * 1024 * 1024
VMEM_LIMIT_FUSED = 56 * 1024 * 1024


def _resident(shape):
    nd = len(shape)
    return pl.BlockSpec(shape, lambda *_: (0,) * nd, pipeline_mode=pl.Buffered(1))


def _params(semantics):
    return pltpu.CompilerParams(dimension_semantics=semantics, vmem_limit_bytes=VMEM_LIMIT)


def _dot(a, b):
    return jnp.dot(a, b, preferred_element_type=F32)


def _dot_nt(a, b):
    return lax.dot_general(a, b, (((1,), (1,)), ((), ())), preferred_element_type=F32)


def _split3(a):
    a1 = a.astype(BF16)
    r1 = a - a1.astype(F32)
    a2 = r1.astype(BF16)
    a3 = (r1 - a2.astype(F32)).astype(BF16)
    return a1, a2, a3


def _div_pow2(x, d):
    return lax.shift_right_logical(x, int(math.log2(d)))


def _cast_kernel(x_ref, o_ref):
    o_ref[...] = x_ref[...].astype(o_ref.dtype)


def _to_bf16(w):
    rows, cols = w.shape
    block = pl.BlockSpec((math.gcd(rows, CAST_ROWS), cols), lambda i: (i, 0))
    return pl.pallas_call(
        _cast_kernel,
        out_shape=jax.ShapeDtypeStruct(w.shape, BF16),
        grid=(rows // block.block_shape[0],),
        in_specs=[block],
        out_specs=block,
        compiler_params=_params(("parallel",)),
        name="to_bf16",
    )(w)


def _rms_rows(x, gain):
    ms = jnp.mean(x * x, axis=-1, keepdims=True)
    return x * lax.rsqrt(ms + NORM_EPS) * gain


def _silu(x):
    return x * jax.nn.sigmoid(x)


def _ffn_chunks(h, wg_ref, wu_ref, wd_ref, acc, chunks):
    for j in chunks:
        cols = slice(j * FF_CHUNK, (j + 1) * FF_CHUNK)
        g = _dot(h, wg_ref[:, cols])
        u = _dot(h, wu_ref[:, cols])
        acc = acc + _dot((_silu(g) * u).astype(BF16), wd_ref[cols, :])
    return acc


def _ffn_kernel(*refs, with_mix):
    if with_mix:
        x_ref, attn_ref, rec_ref, wo_ref, gain_ref, wg_ref, wu_ref, wd_ref, o_ref = refs
        y = (x_ref[...] + _dot(attn_ref[...], wo_ref[:A_WIDTH, :])
             + _dot(rec_ref[...], wo_ref[A_WIDTH:, :]))
    else:
        x_ref, gain_ref, wg_ref, wu_ref, wd_ref, o_ref = refs
        y = x_ref[...]
    h = _rms_rows(y, gain_ref[...]).astype(BF16)
    acc = _ffn_chunks(h, wg_ref, wu_ref, wd_ref, jnp.zeros(y.shape, F32), range(D_FF // FF_CHUNK))
    o_ref[...] = y + 0.5 * acc


def _ffn_sample_kernel(pt_ref, x_ref, gain_ref, wg_ref, wu_ref, wd_ref, q_ref, kn_ref, vn_ref, sbt_ref, ob_ref,
                       ck_ref, cv_ref, y_ref, attn_ref, *scratch, nbk, dec_seq, bpg):
    stages = _sample_stages(pl.program_id(0), pl.num_programs(0), pt_ref, q_ref, kn_ref, vn_ref, sbt_ref, ob_ref,
                            ck_ref, cv_ref, attn_ref, *scratch, nbk=nbk, dec_seq=dec_seq, bpg=bpg)
    y = x_ref[...]
    h = _rms_rows(y, gain_ref[...]).astype(BF16)
    acc = jnp.zeros(y.shape, F32)
    n_chunks = D_FF // FF_CHUNK
    cuts = [k * n_chunks // len(stages) for k in range(len(stages) + 1)]
    for k, stage in enumerate(stages):
        stage()
        acc = _ffn_chunks(h, wg_ref, wu_ref, wd_ref, acc, range(cuts[k], cuts[k + 1]))
    y_ref[...] = y + 0.5 * acc


def _ffn(x, gain, wg, wu, wd, tm, mix=None):
    n = x.shape[0]
    row = lambda w: pl.BlockSpec((tm, w), lambda i: (i, 0))
    args, specs = [x], [row(D_MODEL)]
    if mix is not None:
        attn, rec, wo = mix
        args += [attn, rec, wo]
        specs += [row(A_WIDTH), row(B_WIDTH), _resident(wo.shape)]
    args += [gain, wg, wu, wd]
    specs += [_resident(gain.shape), _resident(wg.shape), _resident(wu.shape), _resident(wd.shape)]
    return pl.pallas_call(
        functools.partial(_ffn_kernel, with_mix=mix is not None),
        out_shape=jax.ShapeDtypeStruct((n, D_MODEL), F32),
        grid=(n // tm,),
        in_specs=specs,
        out_specs=row(D_MODEL),
        compiler_params=_params(("parallel",)),
        name="ffn_mix" if mix is not None else "ffn",
    )(*args)


def _proj_kernel(y_ref, gain_ref, wat_ref, wb_ref, qg_ref, kg_ref, lbl_ref,
                 q_ref, k_ref, v_ref, qb_ref, kb_ref, ib_ref, lf_ref, gb_ref):
    h = _rms_rows(y_ref[...], gain_ref[...]).astype(BF16)
    tm = h.shape[0]

    def group_t(i):
        return _dot_nt(wat_ref[i * GROUP_W:(i + 1) * GROUP_W, :], h)

    def group(i):
        return _dot(h, wb_ref[:, i * GROUP_W:(i + 1) * GROUP_W])

    def head_norm(z, g):
        out = []
        for hd in range(A_HEADS):
            zh = z[hd * A_HEAD_DIM:(hd + 1) * A_HEAD_DIM, :]
            out.append(zh * lax.rsqrt(jnp.mean(zh * zh, axis=0, keepdims=True) + NORM_EPS))
        return jnp.concatenate(out, axis=0) * jnp.concatenate([g] * pl.cdiv(tm, 128), axis=1)[:, :tm]

    q_ref[...] = head_norm(group_t(0), qg_ref[...]) * ATTN_SCALE
    k_ref[...] = head_norm(group_t(1), kg_ref[...])
    v_ref[...] = group_t(2)
    qb_ref[...] = _silu(group(0))
    lbl = lbl_ref[...]
    e = jnp.exp(lbl - jnp.max(lbl, axis=0, keepdims=True))
    lb = e[0:1, :] / jnp.sum(e, axis=0, keepdims=True)
    forget = lb + (1.0 - lb) * jax.nn.sigmoid(group(1))
    lf_ref[...] = jnp.log(forget)
    kb_ref[...] = 1.0 - forget
    ib_ref[...] = group(2)
    gb_ref[...] = group(3)


def _proj(y, gain, wat, wb, qg, kg, lbl, seq, tm):
    n = y.shape[0]
    nt = seq // tm
    row = lambda w: pl.BlockSpec((tm, w), lambda i: (i, 0))
    col = pl.BlockSpec((GROUP_W, tm), lambda i: (i // nt, i % nt))
    out_t = jax.ShapeDtypeStruct((n // seq * GROUP_W, seq), F32)
    out = jax.ShapeDtypeStruct((n, GROUP_W), F32)
    return pl.pallas_call(
        _proj_kernel,
        out_shape=(out_t,) * 3 + (out,) * 5,
        grid=(n // tm,),
        in_specs=[row(D_MODEL)] + [_resident(a.shape) for a in (gain, wat, wb, qg, kg, lbl)],
        out_specs=(col,) * 3 + (row(GROUP_W),) * 5,
        compiler_params=_params(("parallel",)),
        name="proj",
    )(y, gain, wat, wb, qg, kg, lbl)


def _rel_bucket(n):
    nf = jnp.maximum(n, 1).astype(F32)
    large = MAX_EXACT + (jnp.log(nf / MAX_EXACT) / math.log(REL_MAX_DIST / MAX_EXACT)
                         * (N_BUCKETS - MAX_EXACT)).astype(jnp.int32)
    large = jnp.minimum(large, N_BUCKETS - 1)
    return jnp.where(n < MAX_EXACT, n, large)


def _bias_kernel(tab_ref, w_ref, sbt_ref, ob_ref, *, past, dec_seq):
    tab = tab_ref[...]
    tab_rows = jnp.concatenate([tab] * dec_seq, axis=0)

    def lookup(bucket, table):
        out = jnp.zeros(bucket.shape, F32)
        for b in range(N_BUCKETS):
            out = jnp.where(bucket == b, table[:, b:b + 1], out)
        return out

    c = lax.broadcasted_iota(jnp.int32, w_ref.shape, 1)
    w_ref[...] = lookup(_rel_bucket(jnp.maximum(c - MOBA_BLOCK, 0)), tab)
    t = _div_pow2(lax.broadcasted_iota(jnp.int32, sbt_ref.shape, 0), A_HEADS)
    kpos = lax.broadcasted_iota(jnp.int32, sbt_ref.shape, 1)
    sbt_ref[...] = lookup(_rel_bucket(jnp.maximum(past + t - kpos, 0)), tab_rows)
    t = _div_pow2(lax.broadcasted_iota(jnp.int32, ob_ref.shape, 0), A_HEADS)
    s = lax.broadcasted_iota(jnp.int32, ob_ref.shape, 1)
    ob_ref[...] = lookup(_rel_bucket(jnp.maximum(t - s, 0)), tab_rows)


def _bias_tables(tab_t, seq, past, dec_seq):
    rows = dec_seq * A_HEADS
    return pl.pallas_call(
        functools.partial(_bias_kernel, past=past, dec_seq=dec_seq),
        out_shape=(jax.ShapeDtypeStruct((A_HEADS, seq + MOBA_BLOCK), F32),
                   jax.ShapeDtypeStruct((rows, past), F32),
                   jax.ShapeDtypeStruct((rows, 128), F32)),
        name="rel_bias",
    )(tab_t)


def _top_k_mask(gate, k, axis):
    index = lax.broadcasted_iota(jnp.int32, gate.shape, axis).astype(F32)
    sel = jnp.zeros(gate.shape, F32)
    for _ in range(k):
        mx = jnp.max(gate, axis=axis, keepdims=True)
        first = jnp.min(jnp.where(gate == mx, index, float(gate.shape[axis])), axis=axis, keepdims=True)
        chosen = index == first
        sel = jnp.where(chosen & (mx > -jnp.inf), 1.0, sel)
        gate = jnp.where(chosen, -jnp.inf, gate)
    return sel


def _moba_prompt_kernel(q_ref, k_ref, v_ref, w_ref, o_ref, kb_ref, vt_ref, km_ref, tt_ref, s_ref, *, nb):
    b = pl.program_id(1)
    i = pl.program_id(2)
    blk = MOBA_BLOCK

    key_j = lax.broadcasted_iota(jnp.int32, (blk, blk), 0)
    qry_c = lax.broadcasted_iota(jnp.int32, (blk, blk), 1)

    @pl.when((b == 0) & (i == 0))
    def _build_bias_tiles():
        for hh in range(2):
            for d in range(nb):
                win = jnp.broadcast_to(w_ref[0, hh:hh + 1, d * blk:(d + 2) * blk], (blk, 2 * blk))
                tile = pltpu.roll(win, 0, 1, stride=1, stride_axis=0)[:, blk:] * LOG2_E
                tt_ref[hh, d] = jnp.where(qry_c >= key_j, tile, NEG_INF) if d == 0 else tile

    @pl.when(i == 0)
    def _prepare_sequence():
        lane = lax.broadcasted_iota(jnp.int32, (blk, 128), 1)
        ones = jnp.ones((MOBA_ONES_ROWS, blk), F32)
        for n in range(nb):
            cols = slice(n * blk, (n + 1) * blk)
            kblk = k_ref[:, cols].T
            kb_ref[n] = jnp.concatenate([kblk, jnp.where(lane == n, 1.0, 0.0)], axis=1).astype(BF16)
            km_ref[n:n + 1, :] = jnp.mean(kblk, axis=0, keepdims=True)
            for hh in range(2):
                v_h = v_ref[hh * A_HEAD_DIM:(hh + 1) * A_HEAD_DIM, cols]
                vt_ref[n, hh] = jnp.concatenate([v_h, ones], axis=0).astype(BF16)

    q_t = q_ref[...] * LOG2_E
    feat = lax.broadcasted_iota(jnp.int32, q_t.shape, 0)
    km = km_ref[...]
    km_lane = lax.broadcasted_iota(jnp.int32, km.shape, 1)
    blk_id = lax.broadcasted_iota(jnp.int32, (nb, blk), 0)

    fold = lambda a, op: op(a.reshape(blk // 8, 8, blk), axis=0)

    qp = []
    for hh in range(2):
        lo = hh * A_HEAD_DIM
        q_h = jnp.where((feat >= lo) & (feat < lo + A_HEAD_DIM), q_t, 0.0)
        km_h = jnp.where((km_lane >= lo) & (km_lane < lo + A_HEAD_DIM), km, 0.0)
        k1, k2, _ = _split3(km_h)
        q1, q2, _ = _split3(q_h)
        gate = _dot(k1, q1) + _dot(k1, q2) + _dot(k2, q1)
        gate = jnp.where(blk_id < i, gate, -jnp.inf)
        attends = jnp.where(blk_id == i, 1.0, _top_k_mask(gate, MOBA_TOPK, 0))
        penalty = jnp.concatenate([jnp.where(attends > 0.0, 0.0, NEG_INF),
                                   jnp.zeros((128 - nb, blk), F32)], axis=0)
        qp.append(jnp.concatenate([q1, penalty.astype(BF16)], axis=0))

    per_trip = math.gcd(nb, MOBA_BLOCKS_PER_TRIP)
    full_trips = _div_pow2(i + 1, per_trip)
    left_over = (i + 1) & (per_trip - 1)

    def walk(body, carry):
        carry = lax.fori_loop(0, full_trips, lambda j, c: body(per_trip * j, per_trip, c), carry)
        for left in range(1, per_trip):
            count = left if 2 * left <= per_trip else per_trip
            carry = lax.cond(left_over == left,
                             lambda c, count=count: body(per_trip * full_trips, count, c), lambda c: c, carry)
        return carry

    pairs = lambda first, count: [(first + u, hh) for u in range(count) for hh in range(2)]

    def logits(first, count, m8):
        raw = [_dot(kb_ref[n], qp[hh]) for n, hh in pairs(first, count)]
        out = list(m8)
        for (n, hh), r in zip(pairs(first, count), raw):
            s = r + tt_ref[hh, jnp.maximum(i - n, 0)]
            s_ref[hh, n] = s
            out[hh] = jnp.maximum(out[hh], fold(s, jnp.max))
        return tuple(out)

    m8 = walk(logits, (jnp.full((8, blk), -jnp.inf, F32),) * 2)
    m = [jnp.max(a, axis=0, keepdims=True) for a in m8]

    def values(first, count, acc):
        p = [jnp.exp2(s_ref[hh, n] - m[hh]).astype(BF16) for n, hh in pairs(first, count)]
        out = list(acc)
        for (n, hh), ph in zip(pairs(first, count), p):
            out[hh] = out[hh] + _dot(vt_ref[n, hh], ph)
        return tuple(out)

    acc = walk(values, (jnp.zeros((A_HEAD_DIM + MOBA_ONES_ROWS, blk), F32),) * 2)
    out_t = jnp.concatenate([a[:A_HEAD_DIM] / a[A_HEAD_DIM:A_HEAD_DIM + 1] for a in acc], axis=0)
    o_ref[...] = out_t.T.astype(o_ref.dtype)


def _moba_prompt(q_t, k_t, v_t, w_pairs, batch, seq):
    nb = seq // MOBA_BLOCK
    pairs = A_HEADS // 2
    q_tile = pl.BlockSpec((128, MOBA_BLOCK), lambda p, b, i: (b * pairs + p, i))
    whole = pl.BlockSpec((128, seq), lambda p, b, i: (b * pairs + p, 0))
    return pl.pallas_call(
        functools.partial(_moba_prompt_kernel, nb=nb),
        out_shape=jax.ShapeDtypeStruct((batch * seq, A_WIDTH), BF16),
        grid=(pairs, batch, nb),
        in_specs=[q_tile, whole, whole,
                  pl.BlockSpec((1, 2, seq + MOBA_BLOCK), lambda p, b, i: (p, 0, 0))],
        out_specs=pl.BlockSpec((MOBA_BLOCK, 128), lambda p, b, i: (b * nb + i, p)),
        scratch_shapes=[pltpu.VMEM((nb, MOBA_BLOCK, 256), BF16),
                        pltpu.VMEM((nb, 2, A_HEAD_DIM + MOBA_ONES_ROWS, MOBA_BLOCK), BF16),
                        pltpu.VMEM((nb, 128), F32),
                        pltpu.VMEM((2, nb, MOBA_BLOCK, MOBA_BLOCK), F32),
                        pltpu.VMEM((2, nb, MOBA_BLOCK, MOBA_BLOCK), F32)],
        compiler_params=_params(("arbitrary", "arbitrary", "arbitrary")),
        name="moba_prompt",
    )(q_t, k_t, v_t, w_pairs)


def _moba_sample_kernel(*refs, nbk, dec_seq, bpg):
    for stage in _sample_stages(pl.program_id(0), pl.num_programs(0), *refs, nbk=nbk, dec_seq=dec_seq, bpg=bpg):
        stage()


def _sample_stages(b, n_seq, pt_ref, q_ref, kn_ref, vn_ref, sbt_ref, ob_ref, ck_ref, cv_ref, o_ref,
                   kbuf, vbuf, sem, m_sc, l_sc, acc_sc, g_sc, *, nbk, dec_seq, bpg):
    n_groups = nbk // bpg
    ppg = 2 * bpg
    rows = dec_seq * A_HEADS
    head_of_row = lax.broadcasted_iota(jnp.int32, (A_HEADS, A_WIDTH), 0)
    head_of_lane = _div_pow2(lax.broadcasted_iota(jnp.int32, (A_HEADS, A_WIDTH), 1), A_HEAD_DIM)
    head_mask = head_of_row == head_of_lane
    wide = lambda a: jnp.broadcast_to(a, (rows, 128))

    def group_copies(seq, g, slot):
        out = []
        for r in range(ppg):
            page = pt_ref[seq * (n_groups * ppg) + g * ppg + r]
            out.append(pltpu.make_async_copy(ck_ref.at[page], kbuf.at[slot, r], sem.at[slot, 0]))
            out.append(pltpu.make_async_copy(cv_ref.at[page], vbuf.at[slot, r], sem.at[slot, 1]))
        return out

    def queries():
        return jnp.concatenate(
            [jnp.where(head_mask, jnp.broadcast_to(q_ref[0, t:t + 1, :], head_mask.shape), 0.0)
             for t in range(dec_seq)], axis=0).astype(BF16)

    live = {}

    def group_keys(g):
        slot = (b * n_groups + g) & 1
        if g == 0:
            @pl.when(b == 0)
            def _first_group():
                for cp in group_copies(0, 0, 0):
                    cp.start()
        if g + 1 < n_groups:
            for cp in group_copies(b, g + 1, 1 - slot):
                cp.start()
        else:
            @pl.when(b + 1 < n_seq)
            def _next_sequence():
                for cp in group_copies(b + 1, 0, 1 - slot):
                    cp.start()
        for cp in group_copies(b, g, slot):
            cp.wait()
        q16 = queries()
        live["raw"] = jnp.concatenate(
            [_dot(q16, jnp.concatenate([kbuf[slot, 2 * r], kbuf[slot, 2 * r + 1]], axis=1).astype(BF16))
             for r in range(bpg)], axis=1)

    per_block = lambda a: [a[:, r * MOBA_BLOCK:(r + 1) * MOBA_BLOCK] for r in range(bpg)]

    def group_softmax(g):
        width = bpg * MOBA_BLOCK
        s = per_block(live["raw"] + sbt_ref[:, g * width:(g + 1) * width])
        live["m"] = [jnp.max(a, axis=1, keepdims=True) for a in s]
        live["p"] = [jnp.exp(a - mr) for a, mr in zip(s, live["m"])]

    def group_values(g):
        slot = (b * n_groups + g) & 1
        raw, m, p = per_block(live.pop("raw")), live.pop("m"), live.pop("p")
        for r in range(bpg):
            n = g * bpg + r
            vblk = jnp.concatenate([vbuf[slot, 2 * r], vbuf[slot, 2 * r + 1]], axis=1).astype(BF16)
            m_sc[n] = wide(m[r])
            l_sc[n] = wide(jnp.sum(p[r], axis=1, keepdims=True))
            acc_sc[n] = _dot_nt(p[r].astype(BF16), vblk)
            g_sc[n] = wide(jnp.sum(raw[r], axis=1, keepdims=True) * (1.0 / MOBA_BLOCK))

    def merge():
        q16 = queries()
        gates = [g_sc[n] for n in range(nbk)]
        picked = [jnp.zeros((rows, 128), jnp.bool_)] * nbk
        for _ in range(min(MOBA_TOPK, nbk)):
            mx = functools.reduce(jnp.maximum, gates)
            first = functools.reduce(
                jnp.minimum, [jnp.where(g == mx, float(n), float(nbk)) for n, g in enumerate(gates)])
            for n in range(nbk):
                chosen = first == float(n)
                picked[n] = picked[n] | (chosen & (mx > -jnp.inf))
                gates[n] = jnp.where(chosen, -jnp.inf, gates[n])
        zeros = jnp.zeros((128 - kn_ref.shape[1], A_WIDTH), F32)
        k_own = jnp.concatenate([kn_ref[0], zeros], axis=0).astype(BF16)
        v_own = jnp.concatenate([vn_ref[0], zeros], axis=0).astype(BF16)
        s = _dot_nt(q16, k_own) + ob_ref[...]
        t_of_row = _div_pow2(lax.broadcasted_iota(jnp.int32, s.shape, 0), A_HEADS)
        s_idx = lax.broadcasted_iota(jnp.int32, s.shape, 1)
        s = jnp.where(s_idx <= t_of_row, s, NEG_INF)
        m_own = wide(jnp.max(s, axis=1, keepdims=True))
        p = jnp.exp(s - m_own)
        l_own = wide(jnp.sum(p, axis=1, keepdims=True))
        acc_own = _dot(p.astype(BF16), v_own)
        m_all = m_own
        for n in range(nbk):
            m_all = jnp.maximum(m_all, jnp.where(picked[n], m_sc[n], -jnp.inf))
        lanes4 = lambda a: jnp.concatenate([a] * (A_WIDTH // 128), axis=1)
        w = jnp.exp(m_own - m_all)
        l_all = w * l_own
        acc = lanes4(w) * acc_own
        for n in range(nbk):
            w = jnp.where(picked[n], jnp.exp(m_sc[n] - m_all), 0.0)
            l_all = l_all + w * l_sc[n]
            acc = acc + lanes4(w) * acc_sc[n]
        res = acc / lanes4(l_all)
        out = [jnp.sum(jnp.where(head_mask, res[t * A_HEADS:(t + 1) * A_HEADS, :], 0.0),
                       axis=0, keepdims=True) for t in range(dec_seq)]
        o_ref[0] = jnp.concatenate(out, axis=0).astype(o_ref.dtype)

    return [functools.partial(stage, g) for g in range(n_groups)
            for stage in (group_keys, group_softmax, group_values)] + [merge]


def _moba_sample(q, k_new, v_new, cache_k, cache_v, page_table, sbt, ob, bpg, ffn=None):
    db, dec_seq, _ = q.shape
    n_pages = page_table.shape[1]
    nbk = n_pages * PAGE_SIZE // MOBA_BLOCK
    rows = dec_seq * A_HEADS
    assert MOBA_BLOCK == 2 * PAGE_SIZE and nbk % bpg == 0
    per_seq = lambda b, pt: (b, 0, 0)
    whole = lambda a: pl.BlockSpec(a.shape, lambda b, pt: (0,) * a.ndim, pipeline_mode=pl.Buffered(1))
    sample_specs = [pl.BlockSpec((1, dec_seq, A_WIDTH), per_seq),
                    pl.BlockSpec((1, k_new.shape[1], A_WIDTH), per_seq),
                    pl.BlockSpec((1, v_new.shape[1], A_WIDTH), per_seq),
                    whole(sbt), whole(ob),
                    pl.BlockSpec(memory_space=pl.ANY), pl.BlockSpec(memory_space=pl.ANY)]
    sample_args = (q, k_new, v_new, sbt, ob, cache_k, cache_v)
    attn_spec = pl.BlockSpec((1, dec_seq, A_WIDTH), per_seq)
    attn_shape = jax.ShapeDtypeStruct((db, dec_seq, A_WIDTH), BF16)
    scratch = [pltpu.VMEM((2, 2 * bpg, A_WIDTH, PAGE_SIZE), F32),
               pltpu.VMEM((2, 2 * bpg, A_WIDTH, PAGE_SIZE), F32),
               pltpu.SemaphoreType.DMA((2, 2)),
               pltpu.VMEM((nbk, rows, 128), F32),
               pltpu.VMEM((nbk, rows, 128), F32),
               pltpu.VMEM((nbk, rows, A_WIDTH), F32),
               pltpu.VMEM((nbk, rows, 128), F32)]
    static = dict(nbk=nbk, dec_seq=dec_seq, bpg=bpg)
    if ffn is None:
        return pl.pallas_call(
            functools.partial(_moba_sample_kernel, **static),
            out_shape=attn_shape,
            grid_spec=pltpu.PrefetchScalarGridSpec(num_scalar_prefetch=1, grid=(db,), in_specs=sample_specs,
                                                   out_specs=attn_spec, scratch_shapes=scratch),
            compiler_params=_params(("arbitrary",)),
            name="moba_sample",
        )(page_table.reshape(-1), *sample_args)
    x, gain, wg, wu, wd, tm = ffn
    assert x.shape[0] == db * tm
    row = pl.BlockSpec((tm, D_MODEL), lambda b, pt: (b, 0))
    return pl.pallas_call(
        functools.partial(_ffn_sample_kernel, **static),
        out_shape=(jax.ShapeDtypeStruct(x.shape, F32), attn_shape),
        grid_spec=pltpu.PrefetchScalarGridSpec(
            num_scalar_prefetch=1, grid=(db,),
            in_specs=[row] + [whole(a) for a in (gain, wg, wu, wd)] + sample_specs,
            out_specs=(row, attn_spec), scratch_shapes=scratch),
        compiler_params=pltpu.CompilerParams(dimension_semantics=("arbitrary",),
                                             vmem_limit_bytes=VMEM_LIMIT_FUSED),
        name="ffn_sample",
    )(page_table.reshape(-1), x, gain, wg, wu, wd, *sample_args)


def _hgrn_kernel(q_ref, k_ref, v_ref, g_ref, gate_ref, s0_ref, og_ref, tril_ref, pick_ref, rec_ref, sfin_ref,
                 st_ref, cum_ref, *, n_chunks):
    tb = pl.program_id(2)
    heads = range(q_ref.shape[1] // B_DIM)

    @pl.when(tb == 0)
    def _load_state():
        for h in heads:
            st_ref[h] = s0_ref[0, h].T

    head = lambda ref, h: ref.at[:, h * B_DIM:(h + 1) * B_DIM]
    used = n_chunks * HG_CHUNK
    work, terms = [], []
    for h in heads:
        o_blocks, off_blocks, term_blocks, v16_blocks, st_ref[h] = _hgrn_scores(
            head(q_ref, h), head(k_ref, h), head(v_ref, h), head(g_ref, h), tril_ref, st_ref[h], head(cum_ref, h),
            n_chunks)
        work.append((o_blocks, off_blocks, v16_blocks))
        terms += term_blocks
    half = (len(terms) + 1) // 2
    inside = [_dot(jnp.concatenate(part, axis=0), pick_ref[...]) for part in (terms[:half], terms[half:]) if part]
    inside = [a[at * used:(at + 1) * used] for a in inside for at in range(a.shape[0] // used)]
    for h in heads:
        n_blocks = len(work[h][0])
        _hgrn_finish(*work[h], inside[h * n_blocks:(h + 1) * n_blocks], head(gate_ref, h), head(og_ref, h),
                     head(rec_ref, h), n_chunks)

    @pl.when(tb == pl.num_programs(2) - 1)
    def _store_state():
        for h in heads:
            sfin_ref[0, h] = st_ref[h].T


def _hgrn_scores(q_ref, k_ref, v_ref, g_ref, tril_ref, st, cum_ref, n_chunks):
    n_blocks = q_ref.shape[0] // HG_BLOCK
    used = n_chunks * HG_CHUNK
    blocks = [slice(bi * HG_BLOCK, (bi + 1) * HG_BLOCK) for bi in range(n_blocks)]

    tril = tril_ref[...]
    g_all = jnp.concatenate([g_ref[rows, :] for rows in blocks], axis=1) * LOG2_E
    g1, g2, g3 = _split3(g_all)
    cum_all = _dot(tril, g1) + _dot(tril, g2) + _dot(tril, g3)

    o_blocks, off_blocks, term_blocks, v16_blocks = [], [], [], []
    for bi, rows in enumerate(blocks):
        q, k, v = q_ref[rows, :], k_ref[rows, :], v_ref[rows, :]
        cum = cum_all[:, bi * B_DIM:(bi + 1) * B_DIM]
        last = cum[HG_BLOCK - 1:HG_BLOCK, :]
        v16_blocks.append(v.astype(BF16))
        o_blocks.append(_dot_nt((q * jnp.exp2(cum)).astype(BF16), st.astype(BF16)))
        st = st * jnp.exp2(last) + _dot(v.T.astype(BF16), (k * jnp.exp2(last - cum)).astype(BF16))
        parts = [jnp.zeros((HG_CHUNK, B_DIM), F32)]
        for c in range(1, n_chunks):
            lo = c * HG_CHUNK
            r = cum[lo - 1:lo, :]
            k_c = (k[:lo] * jnp.exp2(r - cum[:lo])).astype(BF16)
            k_c = jnp.concatenate([k_c, jnp.zeros((HG_BLOCK - lo, B_DIM), BF16)], axis=0)
            q_c = (q[lo:lo + HG_CHUNK] * jnp.exp2(cum[lo:lo + HG_CHUNK] - r)).astype(BF16)
            parts.append(_dot_nt(q_c, k_c))
        off_blocks.append(jnp.concatenate(parts, axis=0))
        cum_ref[rows, :] = cum

    assert used == HG_BLOCK or n_blocks == 1
    n_rows = n_blocks * used
    chunk_rows = lambda ref, s: jnp.concatenate(
        [jnp.broadcast_to(ref[c * HG_CHUNK + s:c * HG_CHUNK + s + 1, :], (HG_CHUNK, B_DIM))
         for c in range(n_rows // HG_CHUNK)], axis=0)
    row_in_chunk = lax.broadcasted_iota(jnp.int32, (n_rows, B_DIM), 0) & (HG_CHUNK - 1)
    q_rows, cum_rows = q_ref[:n_rows, :], cum_ref[:n_rows, :]
    terms = []
    for s in range(HG_CHUNK):
        decay = jnp.exp2(jnp.where(row_in_chunk >= s, cum_rows - chunk_rows(cum_ref, s), -jnp.inf))
        terms.append((q_rows * decay * chunk_rows(k_ref, s)).astype(BF16))
    terms = jnp.concatenate(terms, axis=1)
    term_blocks = [terms[bi * used:(bi + 1) * used] for bi in range(n_blocks)]
    return o_blocks, off_blocks, term_blocks, v16_blocks, st


def _hgrn_finish(o_blocks, off_blocks, v16_blocks, inside_blocks, gate_ref, og_ref, rec_ref, n_chunks):
    used = n_chunks * HG_CHUNK
    row = lax.broadcasted_iota(jnp.int32, (used, B_DIM), 0)
    col = lax.broadcasted_iota(jnp.int32, (used, B_DIM), 1)
    same_chunk = _div_pow2(row, HG_CHUNK) == _div_pow2(col, HG_CHUNK)
    ones = jnp.ones((B_DIM, B_DIM), BF16)
    out = []
    for o, off, v16, inside in zip(o_blocks, off_blocks, v16_blocks, inside_blocks):
        score = off + jnp.where(same_chunk, inside, 0.0)
        intra = _dot(score.astype(BF16), v16)
        if used < HG_BLOCK:
            intra = jnp.concatenate([intra, jnp.zeros((HG_BLOCK - used, B_DIM), F32)], axis=0)
        out.append(o + intra)
    o = jnp.concatenate(out, axis=0)
    ms = _dot((o * o).astype(BF16), ones) * (1.0 / B_DIM)
    gate = gate_ref[...]
    rec_ref[...] = ((o * lax.rsqrt(ms + NORM_EPS) * og_ref[...]) * _silu(gate)).astype(rec_ref.dtype)


def _hgrn(qb, kb, ib, lf, gb, s0, out_gain, batch, seq, tile, n_chunks, heads_per_step):
    nt = seq // tile
    width = heads_per_step * B_DIM
    tok = pl.BlockSpec((tile, width), lambda b, h, t: (b * nt + t, h))
    state = pl.BlockSpec((1, heads_per_step, B_DIM, B_DIM), lambda b, h, t: (b, h, 0, 0))
    idx = jnp.arange(HG_BLOCK)
    tril = (idx[:, None] >= idx[None, :]).astype(BF16)
    pick = (jnp.repeat(jnp.arange(HG_CHUNK), B_DIM)[:, None] == (idx % HG_CHUNK)[None, :]).astype(BF16)
    return pl.pallas_call(
        functools.partial(_hgrn_kernel, n_chunks=n_chunks),
        out_shape=(jax.ShapeDtypeStruct((batch * seq, B_WIDTH), BF16),
                   jax.ShapeDtypeStruct(s0.shape, F32)),
        grid=(batch, B_HEADS // heads_per_step, nt),
        in_specs=[tok, tok, tok, tok, tok, state, pl.BlockSpec((1, width), lambda b, h, t: (0, h)),
                  _resident(tril.shape), _resident(pick.shape)],
        out_specs=(tok, state),
        scratch_shapes=[pltpu.VMEM((heads_per_step, B_DIM, B_DIM), F32),
                        pltpu.VMEM((tile, width), F32)],
        compiler_params=_params(("arbitrary", "arbitrary", "arbitrary")),
        name="hgrn",
    )(qb, kb, ib, lf, gb, s0, out_gain, tril, pick)


def kernel(x_prompt, x_sample, cache_k, cache_v, state_hgrn, page_table, rel_bias_table, lb_logits,
           ffn1_norm, ffn1_gate, ffn1_up, ffn1_down, mix_norm, w_in, q_norm, k_norm, hgrn_out_norm,
           w_out, ffn2_norm, ffn2_gate, ffn2_up, ffn2_down):
    batch, seq, _ = x_prompt.shape
    db, dec_seq, _ = x_sample.shape
    depth, n_phys = cache_k.shape[:2]
    assert depth == 1 and lb_logits.shape[0] == 2
    past = page_table.shape[1] * PAGE_SIZE
    assert seq % MOBA_BLOCK == 0 and past % MOBA_BLOCK == 0 and dec_seq <= HG_CHUNK
    n_p, n_s = batch * seq, db * dec_seq

    bf = lambda w: _to_bf16(w[0])
    gain = lambda g: g[0].reshape(1, -1)
    ffn1 = (gain(ffn1_norm), bf(ffn1_gate), bf(ffn1_up), bf(ffn1_down))
    ffn2 = (gain(ffn2_norm), bf(ffn2_gate), bf(ffn2_up), bf(ffn2_down))
    wo = bf(w_out)
    w_in16 = bf(w_in)
    per_lane = lambda g: jnp.broadcast_to(jnp.tile(g[0], A_HEADS)[:, None], (A_WIDTH, 128))
    proj_w = (gain(mix_norm), w_in16[:, :3 * GROUP_W].T, w_in16[:, 3 * GROUP_W:],
              per_lane(q_norm), per_lane(k_norm), lb_logits)
    out_gain = hgrn_out_norm[0].reshape(1, -1)
    w_bias, sbt, ob = _bias_tables(rel_bias_table.T, seq, past, dec_seq)

    ys = _ffn(x_sample.reshape(n_s, D_MODEL), *ffn1, n_s)
    q_t, k_t, v_t, qb_s, kb_s, ib_s, lf_s, gb_s = _proj(ys, *proj_w, n_s, n_s)
    q_s, k_s, v_s = (a.T.reshape(db, dec_seq, A_WIDTH) for a in (q_t, k_t, v_t))
    pad_rows = lambda a, r: jnp.pad(a.reshape(db, dec_seq, -1), ((0, 0), (0, r - dec_seq), (0, 0)))
    pages = lambda c: c[0].transpose(0, 2, 3, 1).reshape(n_phys, A_WIDTH, PAGE_SIZE)
    sample_attn = functools.partial(
        _moba_sample, q_s, pad_rows(k_s, 8), pad_rows(v_s, 8), pages(cache_k), pages(cache_v), page_table, sbt, ob,
        bpg=math.gcd(past // MOBA_BLOCK, SAMPLE_BLOCKS_PER_STEP))

    tm = 512
    x_p = x_prompt.reshape(n_p, D_MODEL)
    if n_p // tm == db:
        y, attn_s = sample_attn(ffn=(x_p, *ffn1, tm))
    else:
        y, attn_s = _ffn(x_p, *ffn1, tm), sample_attn()
    q_t, kp_t, vp_t, qb, kb, ib, lf, gb = _proj(y, *proj_w, seq, tm)
    attn = _moba_prompt(q_t, kp_t, vp_t, w_bias.reshape(A_HEADS // 2, 2, -1), batch, seq)
    s0 = jnp.zeros((batch, B_HEADS, B_DIM, B_DIM), F32)
    rec, s_p = _hgrn(qb, kb, ib, lf, gb, s0, out_gain, batch, seq, math.gcd(seq, HGRN_TILE), HG_BLOCK // HG_CHUNK, 1)
    y_p = _ffn(y, *ffn2, tm, mix=(attn, rec, wo))
    heads_p = lambda a: a.reshape(batch, A_HEADS, A_HEAD_DIM, seq).transpose(0, 3, 1, 2)[None]

    blocked = lambda a: pad_rows(a, HG_BLOCK).reshape(db * HG_BLOCK, -1)
    rec, s_s = _hgrn(blocked(qb_s), blocked(kb_s), blocked(ib_s), blocked(lf_s), blocked(gb_s), state_hgrn[0],
                     out_gain, db, HG_BLOCK, HG_BLOCK, 1, B_HEADS)
    rec = rec.reshape(db, HG_BLOCK, B_WIDTH)[:, :dec_seq].reshape(n_s, B_WIDTH)
    y_s = _ffn(ys, *ffn2, n_s, mix=(attn_s.reshape(n_s, A_WIDTH), rec, wo))
    heads_s = lambda a: a.reshape(1, db, dec_seq, A_HEADS, A_HEAD_DIM)

    return (y_p.reshape(batch, seq, D_MODEL), y_s.reshape(db, dec_seq, D_MODEL),
            heads_p(kp_t), heads_p(vp_t), s_p[None],
            heads_s(k_s), heads_s(v_s), s_s[None])
```

```python
import functools
import math

import jax
import jax.numpy as jnp
from jax import lax
from jax.experimental import pallas as pl
from jax.experimental.pallas import tpu as pltpu

F32 = jnp.float32
BF16 = jnp.bfloat16

D_MODEL = 1024
A_HEADS = 8
A_HEAD_DIM = 64
A_WIDTH = A_HEADS * A_HEAD_DIM
B_HEADS = 4
B_DIM = 128
B_WIDTH = B_HEADS * B_DIM
GROUP_W = 512
N_GROUPS = 7
D_FF = 2816
MOBA_BLOCK = 256
MOBA_TOPK = 3
PAGE_SIZE = 128
N_BUCKETS = 32
MAX_EXACT = N_BUCKETS // 2
REL_MAX_DIST = 4096
HG_CHUNK = 16
HG_BLOCK = 128
NORM_EPS = 1e-6
NEG_INF = -1e30
ATTN_SCALE = A_HEAD_DIM ** -0.5
LOG2_E = math.log2(math.e)

FF_CHUNK = 256
SAMPLE_BLOCKS_PER_STEP = 8
MOBA_BLOCKS_PER_TRIP = 8
HGRN_TILE = 2048
MOBA_ONES_ROWS = 16
VMEM_LIMIT = 48 * 1024 * 1024
VMEM_LIMIT_FUSED = 56 * 1024 * 1024


def _resident(shape):
    nd = len(shape)
    return pl.BlockSpec(shape, lambda *_: (0,) * nd, pipeline_mode=pl.Buffered(1))


def _params(semantics):
    return pltpu.CompilerParams(dimension_semantics=semantics, vmem_limit_bytes=VMEM_LIMIT)


def _dot(a, b):
    return jnp.dot(a, b, preferred_element_type=F32)


def _dot_nt(a, b):
    return lax.dot_general(a, b, (((1,), (1,)), ((), ())), preferred_element_type=F32)


def _split3(a):
    a1 = a.astype(BF16)
    r1 = a - a1.astype(F32)
    a2 = r1.astype(BF16)
    a3 = (r1 - a2.astype(F32)).astype(BF16)
    return a1, a2, a3


def _div_pow2(x, d):
    return lax.shift_right_logical(x, int(math.log2(d)))


def _rms_rows(x, gain):
    ms = jnp.mean(x * x, axis=-1, keepdims=True)
    return x * lax.rsqrt(ms + NORM_EPS) * gain


def _silu(x):
    return x * jax.nn.sigmoid(x)


def _ffn_chunks(h, wg_ref, wu_ref, wd_ref, acc, chunks):
    for j in chunks:
        cols = slice(j * FF_CHUNK, (j + 1) * FF_CHUNK)
        g = _dot(h, wg_ref[:, cols])
        u = _dot(h, wu_ref[:, cols])
        acc = acc + _dot((_silu(g) * u).astype(BF16), wd_ref[cols, :])
    return acc


def _ffn_kernel(*refs, with_mix):
    if with_mix:
        x_ref, attn_ref, rec_ref, wo_ref, gain_ref, wg_ref, wu_ref, wd_ref, o_ref = refs
        y = (x_ref[...] + _dot(attn_ref[...], wo_ref[:A_WIDTH, :])
             + _dot(rec_ref[...], wo_ref[A_WIDTH:, :]))
    else:
        x_ref, gain_ref, wg_ref, wu_ref, wd_ref, o_ref = refs
        y = x_ref[...]
    h = _rms_rows(y, gain_ref[...]).astype(BF16)
    acc = _ffn_chunks(h, wg_ref, wu_ref, wd_ref, jnp.zeros(y.shape, F32), range(D_FF // FF_CHUNK))
    o_ref[...] = y + 0.5 * acc


def _ffn_sample_kernel(pt_ref, x_ref, gain_ref, wg_ref, wu_ref, wd_ref, q_ref, kn_ref, vn_ref, sbt_ref, ob_ref,
                       ck_ref, cv_ref, y_ref, attn_ref, *scratch, nbk, dec_seq, bpg):
    stages = _sample_stages(pl.program_id(0), pl.num_programs(0), pt_ref, q_ref, kn_ref, vn_ref, sbt_ref, ob_ref,
                            ck_ref, cv_ref, attn_ref, *scratch, nbk=nbk, dec_seq=dec_seq, bpg=bpg)
    y = x_ref[...]
    h = _rms_rows(y, gain_ref[...]).astype(BF16)
    acc = jnp.zeros(y.shape, F32)
    n_chunks = D_FF // FF_CHUNK
    cuts = [k * n_chunks // len(stages) for k in range(len(stages) + 1)]
    for k, stage in enumerate(stages):
        stage()
        acc = _ffn_chunks(h, wg_ref, wu_ref, wd_ref, acc, range(cuts[k], cuts[k + 1]))
    y_ref[...] = y + 0.5 * acc


def _ffn(x, gain, wg, wu, wd, tm, mix=None):
    n = x.shape[0]
    row = lambda w: pl.BlockSpec((tm, w), lambda i: (i, 0))
    args, specs = [x], [row(D_MODEL)]
    if mix is not None:
        attn, rec, wo = mix
        args += [attn, rec, wo]
        specs += [row(A_WIDTH), row(B_WIDTH), _resident(wo.shape)]
    args += [gain, wg, wu, wd]
    specs += [_resident(gain.shape), _resident(wg.shape), _resident(wu.shape), _resident(wd.shape)]
    return pl.pallas_call(
        functools.partial(_ffn_kernel, with_mix=mix is not None),
        out_shape=jax.ShapeDtypeStruct((n, D_MODEL), F32),
        grid=(n // tm,),
        in_specs=specs,
        out_specs=row(D_MODEL),
        compiler_params=_params(("parallel",)),
        name="ffn_mix" if mix is not None else "ffn",
    )(*args)


def _proj_kernel(y_ref, gain_ref, wat_ref, wb_ref, qg_ref, kg_ref, lbl_ref,
                 q_ref, k_ref, v_ref, qb_ref, kb_ref, ib_ref, lf_ref, gb_ref):
    h = _rms_rows(y_ref[...], gain_ref[...]).astype(BF16)
    tm = h.shape[0]

    def group_t(i):
        return _dot_nt(wat_ref[i * GROUP_W:(i + 1) * GROUP_W, :], h)

    def group(i):
        return _dot(h, wb_ref[:, i * GROUP_W:(i + 1) * GROUP_W])

    def head_norm(z, g):
        out = []
        for hd in range(A_HEADS):
            zh = z[hd * A_HEAD_DIM:(hd + 1) * A_HEAD_DIM, :]
            out.append(zh * lax.rsqrt(jnp.mean(zh * zh, axis=0, keepdims=True) + NORM_EPS))
        return jnp.concatenate(out, axis=0) * jnp.concatenate([g] * pl.cdiv(tm, 128), axis=1)[:, :tm]

    q_ref[...] = head_norm(group_t(0), qg_ref[...]) * ATTN_SCALE
    k_ref[...] = head_norm(group_t(1), kg_ref[...])
    v_ref[...] = group_t(2)
    qb_ref[...] = _silu(group(0))
    lbl = lbl_ref[...]
    e = jnp.exp(lbl - jnp.max(lbl, axis=0, keepdims=True))
    lb = e[0:1, :] / jnp.sum(e, axis=0, keepdims=True)
    forget = lb + (1.0 - lb) * jax.nn.sigmoid(group(1))
    lf_ref[...] = jnp.log(forget)
    kb_ref[...] = 1.0 - forget
    ib_ref[...] = group(2)
    gb_ref[...] = group(3)


def _proj(y, gain, wat, wb, qg, kg, lbl, seq, tm):
    n = y.shape[0]
    nt = seq // tm
    row = lambda w: pl.BlockSpec((tm, w), lambda i: (i, 0))
    col = pl.BlockSpec((GROUP_W, tm), lambda i: (i // nt, i % nt))
    out_t = jax.ShapeDtypeStruct((n // seq * GROUP_W, seq), F32)
    out = jax.ShapeDtypeStruct((n, GROUP_W), F32)
    return pl.pallas_call(
        _proj_kernel,
        out_shape=(out_t,) * 3 + (out,) * 5,
        grid=(n // tm,),
        in_specs=[row(D_MODEL)] + [_resident(a.shape) for a in (gain, wat, wb, qg, kg, lbl)],
        out_specs=(col,) * 3 + (row(GROUP_W),) * 5,
        compiler_params=_params(("parallel",)),
        name="proj",
    )(y, gain, wat, wb, qg, kg, lbl)


def _rel_bucket(n):
    nf = jnp.maximum(n, 1).astype(F32)
    large = MAX_EXACT + (jnp.log(nf / MAX_EXACT) / math.log(REL_MAX_DIST / MAX_EXACT)
                         * (N_BUCKETS - MAX_EXACT)).astype(jnp.int32)
    large = jnp.minimum(large, N_BUCKETS - 1)
    return jnp.where(n < MAX_EXACT, n, large)


def _bias_kernel(tab_ref, w_ref, sbt_ref, ob_ref, *, past, dec_seq):
    tab = tab_ref[...]
    tab_rows = jnp.concatenate([tab] * dec_seq, axis=0)

    def lookup(bucket, table):
        out = jnp.zeros(bucket.shape, F32)
        for b in range(N_BUCKETS):
            out = jnp.where(bucket == b, table[:, b:b + 1], out)
        return out

    c = lax.broadcasted_iota(jnp.int32, w_ref.shape, 1)
    w_ref[...] = lookup(_rel_bucket(jnp.maximum(c - MOBA_BLOCK, 0)), tab)
    t = _div_pow2(lax.broadcasted_iota(jnp.int32, sbt_ref.shape, 0), A_HEADS)
    kpos = lax.broadcasted_iota(jnp.int32, sbt_ref.shape, 1)
    sbt_ref[...] = lookup(_rel_bucket(jnp.maximum(past + t - kpos, 0)), tab_rows)
    t = _div_pow2(lax.broadcasted_iota(jnp.int32, ob_ref.shape, 0), A_HEADS)
    s = lax.broadcasted_iota(jnp.int32, ob_ref.shape, 1)
    ob_ref[...] = lookup(_rel_bucket(jnp.maximum(t - s, 0)), tab_rows)


def _bias_tables(tab_t, seq, past, dec_seq):
    rows = dec_seq * A_HEADS
    return pl.pallas_call(
        functools.partial(_bias_kernel, past=past, dec_seq=dec_seq),
        out_shape=(jax.ShapeDtypeStruct((A_HEADS, seq + MOBA_BLOCK), F32),
                   jax.ShapeDtypeStruct((rows, past), F32),
                   jax.ShapeDtypeStruct((rows, 128), F32)),
        name="rel_bias",
    )(tab_t)


def _top_k_mask(gate, k, axis):
    index = lax.broadcasted_iota(jnp.int32, gate.shape, axis).astype(F32)
    sel = jnp.zeros(gate.shape, F32)
    for _ in range(k):
        mx = jnp.max(gate, axis=axis, keepdims=True)
        first = jnp.min(jnp.where(gate == mx, index, float(gate.shape[axis])), axis=axis, keepdims=True)
        chosen = index == first
        sel = jnp.where(chosen & (mx > -jnp.inf), 1.0, sel)
        gate = jnp.where(chosen, -jnp.inf, gate)
    return sel


def _moba_prompt_kernel(q_ref, k_ref, v_ref, w_ref, o_ref, kb_ref, vt_ref, km_ref, tt_ref, s_ref, *, nb):
    b = pl.program_id(1)
    i = pl.program_id(2)
    blk = MOBA_BLOCK

    key_j = lax.broadcasted_iota(jnp.int32, (blk, blk), 0)
    qry_c = lax.broadcasted_iota(jnp.int32, (blk, blk), 1)

    @pl.when((b == 0) & (i == 0))
    def _build_bias_tiles():
        for hh in range(2):
            for d in range(nb):
                win = jnp.broadcast_to(w_ref[0, hh:hh + 1, d * blk:(d + 2) * blk], (blk, 2 * blk))
                tile = pltpu.roll(win, 0, 1, stride=1, stride_axis=0)[:, blk:] * LOG2_E
                tt_ref[hh, d] = jnp.where(qry_c >= key_j, tile, NEG_INF) if d == 0 else tile

    @pl.when(i == 0)
    def _prepare_sequence():
        lane = lax.broadcasted_iota(jnp.int32, (blk, 128), 1)
        ones = jnp.ones((MOBA_ONES_ROWS, blk), F32)
        for n in range(nb):
            cols = slice(n * blk, (n + 1) * blk)
            kblk = k_ref[:, cols].T
            kb_ref[n] = jnp.concatenate([kblk, jnp.where(lane == n, 1.0, 0.0)], axis=1).astype(BF16)
            km_ref[n:n + 1, :] = jnp.mean(kblk, axis=0, keepdims=True)
            for hh in range(2):
                v_h = v_ref[hh * A_HEAD_DIM:(hh + 1) * A_HEAD_DIM, cols]
                vt_ref[n, hh] = jnp.concatenate([v_h, ones], axis=0).astype(BF16)

    q_t = q_ref[...] * LOG2_E
    feat = lax.broadcasted_iota(jnp.int32, q_t.shape, 0)
    km = km_ref[...]
    km_lane = lax.broadcasted_iota(jnp.int32, km.shape, 1)
    blk_id = lax.broadcasted_iota(jnp.int32, (nb, blk), 0)

    fold = lambda a, op: op(a.reshape(blk // 8, 8, blk), axis=0)

    qp = []
    for hh in range(2):
        lo = hh * A_HEAD_DIM
        q_h = jnp.where((feat >= lo) & (feat < lo + A_HEAD_DIM), q_t, 0.0)
        km_h = jnp.where((km_lane >= lo) & (km_lane < lo + A_HEAD_DIM), km, 0.0)
        k1, k2, _ = _split3(km_h)
        q1, q2, _ = _split3(q_h)
        gate = _dot(k1, q1) + _dot(k1, q2) + _dot(k2, q1)
        gate = jnp.where(blk_id < i, gate, -jnp.inf)
        attends = jnp.where(blk_id == i, 1.0, _top_k_mask(gate, MOBA_TOPK, 0))
        penalty = jnp.concatenate([jnp.where(attends > 0.0, 0.0, NEG_INF),
                                   jnp.zeros((128 - nb, blk), F32)], axis=0)
        qp.append(jnp.concatenate([q1, penalty.astype(BF16)], axis=0))

    per_trip = math.gcd(nb, MOBA_BLOCKS_PER_TRIP)
    full_trips = _div_pow2(i + 1, per_trip)
    left_over = (i + 1) & (per_trip - 1)

    def walk(body, carry):
        carry = lax.fori_loop(0, full_trips, lambda j, c: body(per_trip * j, per_trip, c), carry)
        first, count = per_trip * full_trips, per_trip // 2
        while count:
            take = (left_over & count) != 0
            carry = lax.cond(take, lambda c, first=first, count=count: body(first, count, c), lambda c: c, carry)
            first, count = first + jnp.where(take, count, 0), count // 2
        return carry

    pairs = lambda first, count: [(first + u, hh) for u in range(count) for hh in range(2)]

    def logits(first, count, m8):
        raw = [_dot(kb_ref[n], qp[hh]) for n, hh in pairs(first, count)]
        out = list(m8)
        for (n, hh), r in zip(pairs(first, count), raw):
            s = r + tt_ref[hh, jnp.maximum(i - n, 0)]
            s_ref[hh, n] = s
            out[hh] = jnp.maximum(out[hh], fold(s, jnp.max))
        return tuple(out)

    m8 = walk(logits, (jnp.full((8, blk), -jnp.inf, F32),) * 2)
    m = [jnp.max(a, axis=0, keepdims=True) for a in m8]

    def values(first, count, acc):
        p = [jnp.exp2(s_ref[hh, n] - m[hh]).astype(BF16) for n, hh in pairs(first, count)]
        out = list(acc)
        for (n, hh), ph in zip(pairs(first, count), p):
            out[hh] = out[hh] + _dot(vt_ref[n, hh], ph)
        return tuple(out)

    acc = walk(values, (jnp.zeros((A_HEAD_DIM + MOBA_ONES_ROWS, blk), F32),) * 2)
    out_t = jnp.concatenate([a[:A_HEAD_DIM] / a[A_HEAD_DIM:A_HEAD_DIM + 1] for a in acc], axis=0)
    o_ref[...] = out_t.T.astype(o_ref.dtype)


def _moba_prompt(q_t, k_t, v_t, w_pairs, batch, seq):
    nb = seq // MOBA_BLOCK
    pairs = A_HEADS // 2
    q_tile = pl.BlockSpec((128, MOBA_BLOCK), lambda p, b, i: (b * pairs + p, i))
    whole = pl.BlockSpec((128, seq), lambda p, b, i: (b * pairs + p, 0))
    return pl.pallas_call(
        functools.partial(_moba_prompt_kernel, nb=nb),
        out_shape=jax.ShapeDtypeStruct((batch * seq, A_WIDTH), BF16),
        grid=(pairs, batch, nb),
        in_specs=[q_tile, whole, whole,
                  pl.BlockSpec((1, 2, seq + MOBA_BLOCK), lambda p, b, i: (p, 0, 0))],
        out_specs=pl.BlockSpec((MOBA_BLOCK, 128), lambda p, b, i: (b * nb + i, p)),
        scratch_shapes=[pltpu.VMEM((nb, MOBA_BLOCK, 256), BF16),
                        pltpu.VMEM((nb, 2, A_HEAD_DIM + MOBA_ONES_ROWS, MOBA_BLOCK), BF16),
                        pltpu.VMEM((nb, 128), F32),
                        pltpu.VMEM((2, nb, MOBA_BLOCK, MOBA_BLOCK), F32),
                        pltpu.VMEM((2, nb, MOBA_BLOCK, MOBA_BLOCK), F32)],
        compiler_params=_params(("arbitrary", "arbitrary", "arbitrary")),
        name="moba_prompt",
    )(q_t, k_t, v_t, w_pairs)


def _moba_sample_kernel(*refs, nbk, dec_seq, bpg):
    for stage in _sample_stages(pl.program_id(0), pl.num_programs(0), *refs, nbk=nbk, dec_seq=dec_seq, bpg=bpg):
        stage()


def _sample_stages(b, n_seq, pt_ref, q_ref, kn_ref, vn_ref, sbt_ref, ob_ref, ck_ref, cv_ref, o_ref,
                   kbuf, vbuf, sem, m_sc, l_sc, acc_sc, g_sc, *, nbk, dec_seq, bpg):
    n_groups = nbk // bpg
    ppg = 2 * bpg
    rows = dec_seq * A_HEADS
    head_of_row = lax.broadcasted_iota(jnp.int32, (A_HEADS, A_WIDTH), 0)
    head_of_lane = _div_pow2(lax.broadcasted_iota(jnp.int32, (A_HEADS, A_WIDTH), 1), A_HEAD_DIM)
    head_mask = head_of_row == head_of_lane
    wide = lambda a: jnp.broadcast_to(a, (rows, 128))

    def group_copies(seq, g, slot):
        out = []
        for r in range(ppg):
            page = pt_ref[seq * (n_groups * ppg) + g * ppg + r]
            out.append(pltpu.make_async_copy(ck_ref.at[page], kbuf.at[slot, r], sem.at[slot, 0]))
            out.append(pltpu.make_async_copy(cv_ref.at[page], vbuf.at[slot, r], sem.at[slot, 1]))
        return out

    def queries():
        return jnp.concatenate(
            [jnp.where(head_mask, jnp.broadcast_to(q_ref[0, t:t + 1, :], head_mask.shape), 0.0)
             for t in range(dec_seq)], axis=0).astype(BF16)

    live = {}

    def group_keys(g):
        slot = (b * n_groups + g) & 1
        if g == 0:
            @pl.when(b == 0)
            def _first_group():
                for cp in group_copies(0, 0, 0):
                    cp.start()
        if g + 1 < n_groups:
            for cp in group_copies(b, g + 1, 1 - slot):
                cp.start()
        else:
            @pl.when(b + 1 < n_seq)
            def _next_sequence():
                for cp in group_copies(b + 1, 0, 1 - slot):
                    cp.start()
        for cp in group_copies(b, g, slot):
            cp.wait()
        q16 = queries()
        live["raw"] = jnp.concatenate(
            [_dot(q16, jnp.concatenate([kbuf[slot, 2 * r], kbuf[slot, 2 * r + 1]], axis=1).astype(BF16))
             for r in range(bpg)], axis=1)

    per_block = lambda a: [a[:, r * MOBA_BLOCK:(r + 1) * MOBA_BLOCK] for r in range(bpg)]

    def group_softmax(g):
        width = bpg * MOBA_BLOCK
        s = per_block(live["raw"] + sbt_ref[:, g * width:(g + 1) * width])
        live["m"] = [jnp.max(a, axis=1, keepdims=True) for a in s]
        live["p"] = [jnp.exp(a - mr) for a, mr in zip(s, live["m"])]

    def group_values(g):
        slot = (b * n_groups + g) & 1
        raw, m, p = per_block(live.pop("raw")), live.pop("m"), live.pop("p")
        for r in range(bpg):
            n = g * bpg + r
            vblk = jnp.concatenate([vbuf[slot, 2 * r], vbuf[slot, 2 * r + 1]], axis=1).astype(BF16)
            m_sc[n] = wide(m[r])
            l_sc[n] = wide(jnp.sum(p[r], axis=1, keepdims=True))
            acc_sc[n] = _dot_nt(p[r].astype(BF16), vblk)
            g_sc[n] = wide(jnp.sum(raw[r], axis=1, keepdims=True) * (1.0 / MOBA_BLOCK))

    def merge():
        q16 = queries()
        gates = [g_sc[n] for n in range(nbk)]
        picked = [jnp.zeros((rows, 128), jnp.bool_)] * nbk
        for _ in range(min(MOBA_TOPK, nbk)):
            mx = functools.reduce(jnp.maximum, gates)
            first = functools.reduce(
                jnp.minimum, [jnp.where(g == mx, float(n), float(nbk)) for n, g in enumerate(gates)])
            for n in range(nbk):
                chosen = first == float(n)
                picked[n] = picked[n] | (chosen & (mx > -jnp.inf))
                gates[n] = jnp.where(chosen, -jnp.inf, gates[n])
        zeros = jnp.zeros((128 - kn_ref.shape[1], A_WIDTH), F32)
        k_own = jnp.concatenate([kn_ref[0], zeros], axis=0).astype(BF16)
        v_own = jnp.concatenate([vn_ref[0], zeros], axis=0).astype(BF16)
        s = _dot_nt(q16, k_own) + ob_ref[...]
        t_of_row = _div_pow2(lax.broadcasted_iota(jnp.int32, s.shape, 0), A_HEADS)
        s_idx = lax.broadcasted_iota(jnp.int32, s.shape, 1)
        s = jnp.where(s_idx <= t_of_row, s, NEG_INF)
        m_own = wide(jnp.max(s, axis=1, keepdims=True))
        p = jnp.exp(s - m_own)
        l_own = wide(jnp.sum(p, axis=1, keepdims=True))
        acc_own = _dot(p.astype(BF16), v_own)
        m_all = m_own
        for n in range(nbk):
            m_all = jnp.maximum(m_all, jnp.where(picked[n], m_sc[n], -jnp.inf))
        lanes4 = lambda a: jnp.concatenate([a] * (A_WIDTH // 128), axis=1)
        w = jnp.exp(m_own - m_all)
        l_all = w * l_own
        acc = lanes4(w) * acc_own
        for n in range(nbk):
            w = jnp.where(picked[n], jnp.exp(m_sc[n] - m_all), 0.0)
            l_all = l_all + w * l_sc[n]
            acc = acc + lanes4(w) * acc_sc[n]
        res = acc / lanes4(l_all)
        out = [jnp.sum(jnp.where(head_mask, res[t * A_HEADS:(t + 1) * A_HEADS, :], 0.0),
                       axis=0, keepdims=True) for t in range(dec_seq)]
        o_ref[0] = jnp.concatenate(out, axis=0).astype(o_ref.dtype)

    return [functools.partial(stage, g) for g in range(n_groups)
            for stage in (group_keys, group_softmax, group_values)] + [merge]


def _moba_sample(q, k_new, v_new, cache_k, cache_v, page_table, sbt, ob, bpg, ffn=None):
    db, dec_seq, _ = q.shape
    n_pages = page_table.shape[1]
    nbk = n_pages * PAGE_SIZE // MOBA_BLOCK
    rows = dec_seq * A_HEADS
    assert MOBA_BLOCK == 2 * PAGE_SIZE and nbk % bpg == 0
    per_seq = lambda b, pt: (b, 0, 0)
    whole = lambda a: pl.BlockSpec(a.shape, lambda b, pt: (0,) * a.ndim, pipeline_mode=pl.Buffered(1))
    sample_specs = [pl.BlockSpec((1, dec_seq, A_WIDTH), per_seq),
                    pl.BlockSpec((1, k_new.shape[1], A_WIDTH), per_seq),
                    pl.BlockSpec((1, v_new.shape[1], A_WIDTH), per_seq),
                    whole(sbt), whole(ob),
                    pl.BlockSpec(memory_space=pl.ANY), pl.BlockSpec(memory_space=pl.ANY)]
    sample_args = (q, k_new, v_new, sbt, ob, cache_k, cache_v)
    attn_spec = pl.BlockSpec((1, dec_seq, A_WIDTH), per_seq)
    attn_shape = jax.ShapeDtypeStruct((db, dec_seq, A_WIDTH), BF16)
    scratch = [pltpu.VMEM((2, 2 * bpg, A_WIDTH, PAGE_SIZE), F32),
               pltpu.VMEM((2, 2 * bpg, A_WIDTH, PAGE_SIZE), F32),
               pltpu.SemaphoreType.DMA((2, 2)),
               pltpu.VMEM((nbk, rows, 128), F32),
               pltpu.VMEM((nbk, rows, 128), F32),
               pltpu.VMEM((nbk, rows, A_WIDTH), F32),
               pltpu.VMEM((nbk, rows, 128), F32)]
    static = dict(nbk=nbk, dec_seq=dec_seq, bpg=bpg)
    if ffn is None:
        return pl.pallas_call(
            functools.partial(_moba_sample_kernel, **static),
            out_shape=attn_shape,
            grid_spec=pltpu.PrefetchScalarGridSpec(num_scalar_prefetch=1, grid=(db,), in_specs=sample_specs,
                                                   out_specs=attn_spec, scratch_shapes=scratch),
            compiler_params=_params(("arbitrary",)),
            name="moba_sample",
        )(page_table.reshape(-1), *sample_args)
    x, gain, wg, wu, wd, tm = ffn
    assert x.shape[0] == db * tm
    row = pl.BlockSpec((tm, D_MODEL), lambda b, pt: (b, 0))
    return pl.pallas_call(
        functools.partial(_ffn_sample_kernel, **static),
        out_shape=(jax.ShapeDtypeStruct(x.shape, F32), attn_shape),
        grid_spec=pltpu.PrefetchScalarGridSpec(
            num_scalar_prefetch=1, grid=(db,),
            in_specs=[row] + [whole(a) for a in (gain, wg, wu, wd)] + sample_specs,
            out_specs=(row, attn_spec), scratch_shapes=scratch),
        compiler_params=pltpu.CompilerParams(dimension_semantics=("arbitrary",),
                                             vmem_limit_bytes=VMEM_LIMIT_FUSED),
        name="ffn_sample",
    )(page_table.reshape(-1), x, gain, wg, wu, wd, *sample_args)


def _hgrn_kernel(q_ref, k_ref, v_ref, g_ref, gate_ref, s0_ref, og_ref, tril_ref, pick_ref, rec_ref, sfin_ref,
                 st_ref, cum_ref, *, n_chunks):
    tb = pl.program_id(2)
    heads = range(q_ref.shape[1] // B_DIM)

    @pl.when(tb == 0)
    def _load_state():
        for h in heads:
            st_ref[h] = s0_ref[0, h].T

    head = lambda ref, h: ref.at[:, h * B_DIM:(h + 1) * B_DIM]
    used = n_chunks * HG_CHUNK
    work, terms = [], []
    for h in heads:
        o_blocks, off_blocks, term_blocks, v16_blocks, st_ref[h] = _hgrn_scores(
            head(q_ref, h), head(k_ref, h), head(v_ref, h), head(g_ref, h), tril_ref, st_ref[h], head(cum_ref, h),
            n_chunks)
        work.append((o_blocks, off_blocks, v16_blocks))
        terms += term_blocks
    half = (len(terms) + 1) // 2
    inside = [_dot(jnp.concatenate(part, axis=0), pick_ref[...]) for part in (terms[:half], terms[half:]) if part]
    inside = [a[at * used:(at + 1) * used] for a in inside for at in range(a.shape[0] // used)]
    for h in heads:
        n_blocks = len(work[h][0])
        _hgrn_finish(*work[h], inside[h * n_blocks:(h + 1) * n_blocks], head(gate_ref, h), head(og_ref, h),
                     head(rec_ref, h), n_chunks)

    @pl.when(tb == pl.num_programs(2) - 1)
    def _store_state():
        for h in heads:
            sfin_ref[0, h] = st_ref[h].T


def _hgrn_scores(q_ref, k_ref, v_ref, g_ref, tril_ref, st, cum_ref, n_chunks):
    n_blocks = q_ref.shape[0] // HG_BLOCK
    used = n_chunks * HG_CHUNK
    blocks = [slice(bi * HG_BLOCK, (bi + 1) * HG_BLOCK) for bi in range(n_blocks)]

    tril = tril_ref[...]
    g_all = jnp.concatenate([g_ref[rows, :] for rows in blocks], axis=1) * LOG2_E
    g1, g2, g3 = _split3(g_all)
    cum_all = _dot(tril, g1) + _dot(tril, g2) + _dot(tril, g3)

    o_blocks, off_blocks, term_blocks, v16_blocks = [], [], [], []
    for bi, rows in enumerate(blocks):
        q, k, v = q_ref[rows, :], k_ref[rows, :], v_ref[rows, :]
        cum = cum_all[:, bi * B_DIM:(bi + 1) * B_DIM]
        last = cum[HG_BLOCK - 1:HG_BLOCK, :]
        v16_blocks.append(v.astype(BF16))
        o_blocks.append(_dot_nt((q * jnp.exp2(cum)).astype(BF16), st.astype(BF16)))
        st = st * jnp.exp2(last) + _dot(v.T.astype(BF16), (k * jnp.exp2(last - cum)).astype(BF16))
        parts = [jnp.zeros((HG_CHUNK, B_DIM), F32)]
        for c in range(1, n_chunks):
            lo = c * HG_CHUNK
            r = cum[lo - 1:lo, :]
            k_c = (k[:lo] * jnp.exp2(r - cum[:lo])).astype(BF16)
            k_c = jnp.concatenate([k_c, jnp.zeros((HG_BLOCK - lo, B_DIM), BF16)], axis=0)
            q_c = (q[lo:lo + HG_CHUNK] * jnp.exp2(cum[lo:lo + HG_CHUNK] - r)).astype(BF16)
            parts.append(_dot_nt(q_c, k_c))
        off_blocks.append(jnp.concatenate(parts, axis=0))
        cum_ref[rows, :] = cum

    assert used == HG_BLOCK or n_blocks == 1
    n_rows = n_blocks * used
    chunk_rows = lambda ref, s: jnp.concatenate(
        [jnp.broadcast_to(ref[c * HG_CHUNK + s:c * HG_CHUNK + s + 1, :], (HG_CHUNK, B_DIM))
         for c in range(n_rows // HG_CHUNK)], axis=0)
    row_in_chunk = lax.broadcasted_iota(jnp.int32, (n_rows, B_DIM), 0) & (HG_CHUNK - 1)
    q_rows, cum_rows = q_ref[:n_rows, :], cum_ref[:n_rows, :]
    terms = []
    for s in range(HG_CHUNK):
        decay = jnp.exp2(jnp.where(row_in_chunk >= s, cum_rows - chunk_rows(cum_ref, s), -jnp.inf))
        terms.append((q_rows * decay * chunk_rows(k_ref, s)).astype(BF16))
    terms = jnp.concatenate(terms, axis=1)
    term_blocks = [terms[bi * used:(bi + 1) * used] for bi in range(n_blocks)]
    return o_blocks, off_blocks, term_blocks, v16_blocks, st


def _hgrn_finish(o_blocks, off_blocks, v16_blocks, inside_blocks, gate_ref, og_ref, rec_ref, n_chunks):
    used = n_chunks * HG_CHUNK
    row = lax.broadcasted_iota(jnp.int32, (used, B_DIM), 0)
    col = lax.broadcasted_iota(jnp.int32, (used, B_DIM), 1)
    same_chunk = _div_pow2(row, HG_CHUNK) == _div_pow2(col, HG_CHUNK)
    ones = jnp.ones((B_DIM, B_DIM), BF16)
    out = []
    for o, off, v16, inside in zip(o_blocks, off_blocks, v16_blocks, inside_blocks):
        score = off + jnp.where(same_chunk, inside, 0.0)
        intra = _dot(score.astype(BF16), v16)
        if used < HG_BLOCK:
            intra = jnp.concatenate([intra, jnp.zeros((HG_BLOCK - used, B_DIM), F32)], axis=0)
        out.append(o + intra)
    o = jnp.concatenate(out, axis=0)
    ms = _dot((o * o).astype(BF16), ones) * (1.0 / B_DIM)
    gate = gate_ref[...]
    rec_ref[...] = ((o * lax.rsqrt(ms + NORM_EPS) * og_ref[...]) * _silu(gate)).astype(rec_ref.dtype)


def _hgrn(qb, kb, ib, lf, gb, s0, out_gain, batch, seq, tile, n_chunks, heads_per_step):
    nt = seq // tile
    width = heads_per_step * B_DIM
    tok = pl.BlockSpec((tile, width), lambda b, h, t: (b * nt + t, h))
    state = pl.BlockSpec((1, heads_per_step, B_DIM, B_DIM), lambda b, h, t: (b, h, 0, 0))
    idx = jnp.arange(HG_BLOCK)
    tril = (idx[:, None] >= idx[None, :]).astype(BF16)
    pick = (jnp.repeat(jnp.arange(HG_CHUNK), B_DIM)[:, None] == (idx % HG_CHUNK)[None, :]).astype(BF16)
    return pl.pallas_call(
        functools.partial(_hgrn_kernel, n_chunks=n_chunks),
        out_shape=(jax.ShapeDtypeStruct((batch * seq, B_WIDTH), BF16),
                   jax.ShapeDtypeStruct(s0.shape, F32)),
        grid=(batch, B_HEADS // heads_per_step, nt),
        in_specs=[tok, tok, tok, tok, tok, state, pl.BlockSpec((1, width), lambda b, h, t: (0, h)),
                  _resident(tril.shape), _resident(pick.shape)],
        out_specs=(tok, state),
        scratch_shapes=[pltpu.VMEM((heads_per_step, B_DIM, B_DIM), F32),
                        pltpu.VMEM((tile, width), F32)],
        compiler_params=_params(("arbitrary", "arbitrary", "arbitrary")),
        name="hgrn",
    )(qb, kb, ib, lf, gb, s0, out_gain, tril, pick)


def kernel(x_prompt, x_sample, cache_k, cache_v, state_hgrn, page_table, rel_bias_table, lb_logits,
           ffn1_norm, ffn1_gate, ffn1_up, ffn1_down, mix_norm, w_in, q_norm, k_norm, hgrn_out_norm,
           w_out, ffn2_norm, ffn2_gate, ffn2_up, ffn2_down):
    batch, seq, _ = x_prompt.shape
    db, dec_seq, _ = x_sample.shape
    depth, n_phys = cache_k.shape[:2]
    assert depth == 1 and lb_logits.shape[0] == 2
    past = page_table.shape[1] * PAGE_SIZE
    assert seq % MOBA_BLOCK == 0 and past % MOBA_BLOCK == 0 and dec_seq <= HG_CHUNK
    n_p, n_s = batch * seq, db * dec_seq

    bf = lambda w: w[0].astype(BF16)
    gain = lambda g: g[0].reshape(1, -1)
    ffn1 = (gain(ffn1_norm), bf(ffn1_gate), bf(ffn1_up), bf(ffn1_down))
    ffn2 = (gain(ffn2_norm), bf(ffn2_gate), bf(ffn2_up), bf(ffn2_down))
    wo = bf(w_out)
    w_in16 = bf(w_in)
    per_lane = lambda g: jnp.broadcast_to(jnp.tile(g[0], A_HEADS)[:, None], (A_WIDTH, 128))
    proj_w = (gain(mix_norm), w_in16[:, :3 * GROUP_W].T, w_in16[:, 3 * GROUP_W:],
              per_lane(q_norm), per_lane(k_norm), lb_logits)
    out_gain = hgrn_out_norm[0].reshape(1, -1)
    w_bias, sbt, ob = _bias_tables(rel_bias_table.T, seq, past, dec_seq)

    ys = _ffn(x_sample.reshape(n_s, D_MODEL), *ffn1, n_s)
    q_t, k_t, v_t, qb_s, kb_s, ib_s, lf_s, gb_s = _proj(ys, *proj_w, n_s, n_s)
    q_s, k_s, v_s = (a.T.reshape(db, dec_seq, A_WIDTH) for a in (q_t, k_t, v_t))
    pad_rows = lambda a, r: jnp.pad(a.reshape(db, dec_seq, -1), ((0, 0), (0, r - dec_seq), (0, 0)))
    pages = lambda c: c[0].transpose(0, 2, 3, 1).reshape(n_phys, A_WIDTH, PAGE_SIZE)
    sample_attn = functools.partial(
        _moba_sample, q_s, pad_rows(k_s, 8), pad_rows(v_s, 8), pages(cache_k), pages(cache_v), page_table, sbt, ob,
        bpg=math.gcd(past // MOBA_BLOCK, SAMPLE_BLOCKS_PER_STEP))

    tm = 512
    x_p = x_prompt.reshape(n_p, D_MODEL)
    if n_p // tm == db:
        y, attn_s = sample_attn(ffn=(x_p, *ffn1, tm))
    else:
        y, attn_s = _ffn(x_p, *ffn1, tm), sample_attn()
    q_t, kp_t, vp_t, qb, kb, ib, lf, gb = _proj(y, *proj_w, seq, tm)
    attn = _moba_prompt(q_t, kp_t, vp_t, w_bias.reshape(A_HEADS // 2, 2, -1), batch, seq)
    s0 = jnp.zeros((batch, B_HEADS, B_DIM, B_DIM), F32)
    rec, s_p = _hgrn(qb, kb, ib, lf, gb, s0, out_gain, batch, seq, math.gcd(seq, HGRN_TILE), HG_BLOCK // HG_CHUNK, 1)
    y_p = _ffn(y, *ffn2, tm, mix=(attn, rec, wo))
    heads_p = lambda a: a.reshape(batch, A_HEADS, A_HEAD_DIM, seq).transpose(0, 3, 1, 2)[None]

    blocked = lambda a: pad_rows(a, HG_BLOCK).reshape(db * HG_BLOCK, -1)
    rec, s_s = _hgrn(blocked(qb_s), blocked(kb_s), blocked(ib_s), blocked(lf_s), blocked(gb_s), state_hgrn[0],
                     out_gain, db, HG_BLOCK, HG_BLOCK, 1, B_HEADS)
    rec = rec.reshape(db, HG_BLOCK, B_WIDTH)[:, :dec_seq].reshape(n_s, B_WIDTH)
    y_s = _ffn(ys, *ffn2, n_s, mix=(attn_s.reshape(n_s, A_WIDTH), rec, wo))
    heads_s = lambda a: a.reshape(1, db, dec_seq, A_HEADS, A_HEAD_DIM)

    return (y_p.reshape(batch, seq, D_MODEL), y_s.reshape(db, dec_seq, D_MODEL),
            heads_p(kp_t), heads_p(vp_t), s_p[None],
            heads_s(k_s), heads_s(v_s), s_s[None])
```

```python
import functools
import math

import jax
import jax.numpy as jnp
from jax import lax
from jax.experimental import pallas as pl
from jax.experimental.pallas import tpu as pltpu

F32 = jnp.float32
BF16 = jnp.bfloat16

D_MODEL = 1024
A_HEADS = 8
A_HEAD_DIM = 64
A_WIDTH = A_HEADS * A_HEAD_DIM
B_HEADS = 4
B_DIM = 128
B_WIDTH = B_HEADS * B_DIM
GROUP_W = 512
N_GROUPS = 7
D_FF = 2816
MOBA_BLOCK = 256
MOBA_TOPK = 3
PAGE_SIZE = 128
N_BUCKETS = 32
MAX_EXACT = N_BUCKETS // 2
REL_MAX_DIST = 4096
HG_CHUNK = 16
HG_BLOCK = 128
NORM_EPS = 1e-6
NEG_INF = -1e30
ATTN_SCALE = A_HEAD_DIM ** -0.5
LOG2_E = math.log2(math.e)

FF_CHUNK = 256
SAMPLE_BLOCKS_PER_STEP = 8
MOBA_BLOCKS_PER_TRIP = 8
HGRN_TILE = 2048
MOBA_ONES_ROWS = 16
VMEM_LIMIT = 48 * 1024 * 1024
VMEM_LIMIT_FUSED = 56 * 1024 * 1024


def _resident(shape):
    nd = len(shape)
    return pl.BlockSpec(shape, lambda *_: (0,) * nd, pipeline_mode=pl.Buffered(1))


def _params(semantics):
    return pltpu.CompilerParams(dimension_semantics=semantics, vmem_limit_bytes=VMEM_LIMIT)


def _dot(a, b):
    return jnp.dot(a, b, preferred_element_type=F32)


def _dot_nt(a, b):
    return lax.dot_general(a, b, (((1,), (1,)), ((), ())), preferred_element_type=F32)


def _split3(a):
    a1 = a.astype(BF16)
    r1 = a - a1.astype(F32)
    a2 = r1.astype(BF16)
    a3 = (r1 - a2.astype(F32)).astype(BF16)
    return a1, a2, a3


def _div_pow2(x, d):
    return lax.shift_right_logical(x, int(math.log2(d)))


def _rms_rows(x, gain):
    ms = jnp.mean(x * x, axis=-1, keepdims=True)
    return x * lax.rsqrt(ms + NORM_EPS) * gain


def _silu(x):
    return x * jax.nn.sigmoid(x)


def _ffn_chunks(h, wg_ref, wu_ref, wd_ref, acc, chunks):
    for j in chunks:
        cols = slice(j * FF_CHUNK, (j + 1) * FF_CHUNK)
        g = _dot(h, wg_ref[:, cols])
        u = _dot(h, wu_ref[:, cols])
        acc = acc + _dot((_silu(g) * u).astype(BF16), wd_ref[cols, :])
    return acc


def _ffn_kernel(*refs, with_mix):
    if with_mix:
        x_ref, attn_ref, rec_ref, wo_ref, gain_ref, wg_ref, wu_ref, wd_ref, o_ref = refs
        y = (x_ref[...] + _dot(attn_ref[...], wo_ref[:A_WIDTH, :])
             + _dot(rec_ref[...], wo_ref[A_WIDTH:, :]))
    else:
        x_ref, gain_ref, wg_ref, wu_ref, wd_ref, o_ref = refs
        y = x_ref[...]
    h = _rms_rows(y, gain_ref[...]).astype(BF16)
    acc = _ffn_chunks(h, wg_ref, wu_ref, wd_ref, jnp.zeros(y.shape, F32), range(D_FF // FF_CHUNK))
    o_ref[...] = y + 0.5 * acc


def _ffn_sample_kernel(pt_ref, x_ref, gain_ref, wg_ref, wu_ref, wd_ref, q_ref, kn_ref, vn_ref, sbt_ref, ob_ref,
                       ck_ref, cv_ref, y_ref, attn_ref, *scratch, nbk, dec_seq, bpg):
    stages = _sample_stages(pl.program_id(0), pl.num_programs(0), pt_ref, q_ref, kn_ref, vn_ref, sbt_ref, ob_ref,
                            ck_ref, cv_ref, attn_ref, *scratch, nbk=nbk, dec_seq=dec_seq, bpg=bpg)
    y = x_ref[...]
    h = _rms_rows(y, gain_ref[...]).astype(BF16)
    acc = jnp.zeros(y.shape, F32)
    n_chunks = D_FF // FF_CHUNK
    cuts = [k * n_chunks // len(stages) for k in range(len(stages) + 1)]
    for k, stage in enumerate(stages):
        stage()
        acc = _ffn_chunks(h, wg_ref, wu_ref, wd_ref, acc, range(cuts[k], cuts[k + 1]))
    y_ref[...] = y + 0.5 * acc


def _ffn(x, gain, wg, wu, wd, tm, mix=None):
    n = x.shape[0]
    row = lambda w: pl.BlockSpec((tm, w), lambda i: (i, 0))
    args, specs = [x], [row(D_MODEL)]
    if mix is not None:
        attn, rec, wo = mix
        args += [attn, rec, wo]
        specs += [row(A_WIDTH), row(B_WIDTH), _resident(wo.shape)]
    args += [gain, wg, wu, wd]
    specs += [_resident(gain.shape), _resident(wg.shape), _resident(wu.shape), _resident(wd.shape)]
    return pl.pallas_call(
        functools.partial(_ffn_kernel, with_mix=mix is not None),
        out_shape=jax.ShapeDtypeStruct((n, D_MODEL), F32),
        grid=(n // tm,),
        in_specs=specs,
        out_specs=row(D_MODEL),
        compiler_params=_params(("parallel",)),
        name="ffn_mix" if mix is not None else "ffn",
    )(*args)


def _proj_kernel(y_ref, gain_ref, wat_ref, wb_ref, qg_ref, kg_ref, lbl_ref,
                 q_ref, k_ref, v_ref, qb_ref, kb_ref, ib_ref, lf_ref, gb_ref):
    h = _rms_rows(y_ref[...], gain_ref[...]).astype(BF16)
    tm = h.shape[0]

    def group_t(i):
        return _dot_nt(wat_ref[i * GROUP_W:(i + 1) * GROUP_W, :], h)

    def group(i):
        return _dot(h, wb_ref[:, i * GROUP_W:(i + 1) * GROUP_W])

    def head_norm(z, g):
        out = []
        for hd in range(A_HEADS):
            zh = z[hd * A_HEAD_DIM:(hd + 1) * A_HEAD_DIM, :]
            out.append(zh * lax.rsqrt(jnp.mean(zh * zh, axis=0, keepdims=True) + NORM_EPS))
        return jnp.concatenate(out, axis=0) * jnp.concatenate([g] * pl.cdiv(tm, 128), axis=1)[:, :tm]

    q_ref[...] = head_norm(group_t(0), qg_ref[...]) * ATTN_SCALE
    k_ref[...] = head_norm(group_t(1), kg_ref[...])
    v_ref[...] = group_t(2)
    qb_ref[...] = _silu(group(0))
    lbl = lbl_ref[...]
    e = jnp.exp(lbl - jnp.max(lbl, axis=0, keepdims=True))
    lb = e[0:1, :] / jnp.sum(e, axis=0, keepdims=True)
    forget = lb + (1.0 - lb) * jax.nn.sigmoid(group(1))
    lf_ref[...] = jnp.log(forget)
    kb_ref[...] = 1.0 - forget
    ib_ref[...] = group(2)
    gb_ref[...] = group(3)


def _proj(y, gain, wat, wb, qg, kg, lbl, seq, tm):
    n = y.shape[0]
    nt = seq // tm
    row = lambda w: pl.BlockSpec((tm, w), lambda i: (i, 0))
    col = pl.BlockSpec((GROUP_W, tm), lambda i: (i // nt, i % nt))
    out_t = jax.ShapeDtypeStruct((n // seq * GROUP_W, seq), F32)
    out = jax.ShapeDtypeStruct((n, GROUP_W), F32)
    return pl.pallas_call(
        _proj_kernel,
        out_shape=(out_t,) * 3 + (out,) * 5,
        grid=(n // tm,),
        in_specs=[row(D_MODEL)] + [_resident(a.shape) for a in (gain, wat, wb, qg, kg, lbl)],
        out_specs=(col,) * 3 + (row(GROUP_W),) * 5,
        compiler_params=_params(("parallel",)),
        name="proj",
    )(y, gain, wat, wb, qg, kg, lbl)


def _rel_bucket(n):
    nf = jnp.maximum(n, 1).astype(F32)
    large = MAX_EXACT + (jnp.log(nf / MAX_EXACT) / math.log(REL_MAX_DIST / MAX_EXACT)
                         * (N_BUCKETS - MAX_EXACT)).astype(jnp.int32)
    large = jnp.minimum(large, N_BUCKETS - 1)
    return jnp.where(n < MAX_EXACT, n, large)


def _bias_kernel(tab_ref, w_ref, sbt_ref, ob_ref, *, past, dec_seq):
    tab = tab_ref[...]
    tab_rows = jnp.concatenate([tab] * dec_seq, axis=0)

    def lookup(bucket, table):
        out = jnp.zeros(bucket.shape, F32)
        for b in range(N_BUCKETS):
            out = jnp.where(bucket == b, table[:, b:b + 1], out)
        return out

    c = lax.broadcasted_iota(jnp.int32, w_ref.shape, 1)
    w_ref[...] = lookup(_rel_bucket(jnp.maximum(c - MOBA_BLOCK, 0)), tab)
    t = _div_pow2(lax.broadcasted_iota(jnp.int32, sbt_ref.shape, 0), A_HEADS)
    kpos = lax.broadcasted_iota(jnp.int32, sbt_ref.shape, 1)
    sbt_ref[...] = lookup(_rel_bucket(jnp.maximum(past + t - kpos, 0)), tab_rows)
    t = _div_pow2(lax.broadcasted_iota(jnp.int32, ob_ref.shape, 0), A_HEADS)
    s = lax.broadcasted_iota(jnp.int32, ob_ref.shape, 1)
    ob_ref[...] = lookup(_rel_bucket(jnp.maximum(t - s, 0)), tab_rows)


def _bias_tables(tab_t, seq, past, dec_seq):
    rows = dec_seq * A_HEADS
    return pl.pallas_call(
        functools.partial(_bias_kernel, past=past, dec_seq=dec_seq),
        out_shape=(jax.ShapeDtypeStruct((A_HEADS, seq + MOBA_BLOCK), F32),
                   jax.ShapeDtypeStruct((rows, past), F32),
                   jax.ShapeDtypeStruct((rows, 128), F32)),
        name="rel_bias",
    )(tab_t)


def _top_k_mask(gate, k, axis):
    index = lax.broadcasted_iota(jnp.int32, gate.shape, axis).astype(F32)
    sel = jnp.zeros(gate.shape, F32)
    for _ in range(k):
        mx = jnp.max(gate, axis=axis, keepdims=True)
        first = jnp.min(jnp.where(gate == mx, index, float(gate.shape[axis])), axis=axis, keepdims=True)
        chosen = index == first
        sel = jnp.where(chosen & (mx > -jnp.inf), 1.0, sel)
        gate = jnp.where(chosen, -jnp.inf, gate)
    return sel


def _moba_prompt_kernel(q_ref, k_ref, v_ref, w_ref, o_ref, kb_ref, vt_ref, km_ref, tt_ref, s_ref, *, nb):
    b = pl.program_id(1)
    i = pl.program_id(2)
    blk = MOBA_BLOCK

    key_j = lax.broadcasted_iota(jnp.int32, (blk, blk), 0)
    qry_c = lax.broadcasted_iota(jnp.int32, (blk, blk), 1)

    @pl.when((b == 0) & (i == 0))
    def _build_bias_tiles():
        for hh in range(2):
            for d in range(nb):
                win = jnp.broadcast_to(w_ref[0, hh:hh + 1, d * blk:(d + 2) * blk], (blk, 2 * blk))
                tile = pltpu.roll(win, 0, 1, stride=1, stride_axis=0)[:, blk:] * LOG2_E
                tt_ref[hh, d] = jnp.where(qry_c >= key_j, tile, NEG_INF) if d == 0 else tile

    @pl.when(i == 0)
    def _prepare_sequence():
        lane = lax.broadcasted_iota(jnp.int32, (blk, 128), 1)
        ones = jnp.ones((MOBA_ONES_ROWS, blk), F32)
        for n in range(nb):
            cols = slice(n * blk, (n + 1) * blk)
            kblk = k_ref[:, cols].T
            kb_ref[n] = jnp.concatenate([kblk, jnp.where(lane == n, 1.0, 0.0)], axis=1).astype(BF16)
            km_ref[n:n + 1, :] = jnp.mean(kblk, axis=0, keepdims=True)
            for hh in range(2):
                v_h = v_ref[hh * A_HEAD_DIM:(hh + 1) * A_HEAD_DIM, cols]
                vt_ref[n, hh] = jnp.concatenate([v_h, ones], axis=0).astype(BF16)

    q_t = q_ref[...] * LOG2_E
    feat = lax.broadcasted_iota(jnp.int32, q_t.shape, 0)
    km = km_ref[...]
    km_lane = lax.broadcasted_iota(jnp.int32, km.shape, 1)
    blk_id = lax.broadcasted_iota(jnp.int32, (nb, blk), 0)

    fold = lambda a, op: op(a.reshape(blk // 8, 8, blk), axis=0)

    qp = []
    for hh in range(2):
        lo = hh * A_HEAD_DIM
        q_h = jnp.where((feat >= lo) & (feat < lo + A_HEAD_DIM), q_t, 0.0)
        km_h = jnp.where((km_lane >= lo) & (km_lane < lo + A_HEAD_DIM), km, 0.0)
        k1, k2, _ = _split3(km_h)
        q1, q2, _ = _split3(q_h)
        gate = _dot(k1, q1) + _dot(k1, q2) + _dot(k2, q1)
        gate = jnp.where(blk_id < i, gate, -jnp.inf)
        attends = jnp.where(blk_id == i, 1.0, _top_k_mask(gate, MOBA_TOPK, 0))
        penalty = jnp.concatenate([jnp.where(attends > 0.0, 0.0, NEG_INF),
                                   jnp.zeros((128 - nb, blk), F32)], axis=0)
        qp.append(jnp.concatenate([q1, penalty.astype(BF16)], axis=0))

    per_trip = math.gcd(nb, MOBA_BLOCKS_PER_TRIP)
    full_trips = _div_pow2(i + 1, per_trip)
    left_over = (i + 1) & (per_trip - 1)

    def pieces(left):
        padded = left + 1
        if left > 2 and padded & (padded - 1) == 0:
            return [padded]
        return [bit for bit in (per_trip >> k for k in range(1, per_trip.bit_length())) if left & bit]

    def walk(body, carry):
        carry = lax.fori_loop(0, full_trips, lambda j, c: body(per_trip * j, per_trip, c), carry)

        def finish(left):
            def run(c):
                first = per_trip * full_trips
                for count in pieces(left):
                    c = body(first, count, c)
                    first = first + count
                return c
            return run

        return lax.switch(left_over, [finish(left) for left in range(per_trip)], carry)

    pairs = lambda first, count: [(first + u, hh) for u in range(count) for hh in range(2)]

    def logits(first, count, m8):
        raw = [_dot(kb_ref[n], qp[hh]) for n, hh in pairs(first, count)]
        out = list(m8)
        for (n, hh), r in zip(pairs(first, count), raw):
            s = r + tt_ref[hh, jnp.maximum(i - n, 0)]
            s_ref[hh, n] = s
            out[hh] = jnp.maximum(out[hh], fold(s, jnp.max))
        return tuple(out)

    m8 = walk(logits, (jnp.full((8, blk), -jnp.inf, F32),) * 2)
    m = [jnp.max(a, axis=0, keepdims=True) for a in m8]

    def values(first, count, acc):
        p = [jnp.exp2(s_ref[hh, n] - m[hh]).astype(BF16) for n, hh in pairs(first, count)]
        out = list(acc)
        for (n, hh), ph in zip(pairs(first, count), p):
            out[hh] = out[hh] + _dot(vt_ref[n, hh], ph)
        return tuple(out)

    acc = walk(values, (jnp.zeros((A_HEAD_DIM + MOBA_ONES_ROWS, blk), F32),) * 2)
    out_t = jnp.concatenate([a[:A_HEAD_DIM] / a[A_HEAD_DIM:A_HEAD_DIM + 1] for a in acc], axis=0)
    o_ref[...] = out_t.T.astype(o_ref.dtype)


def _moba_prompt(q_t, k_t, v_t, w_pairs, batch, seq):
    nb = seq // MOBA_BLOCK
    pairs = A_HEADS // 2
    q_tile = pl.BlockSpec((128, MOBA_BLOCK), lambda p, b, i: (b * pairs + p, i))
    whole = pl.BlockSpec((128, seq), lambda p, b, i: (b * pairs + p, 0))
    return pl.pallas_call(
        functools.partial(_moba_prompt_kernel, nb=nb),
        out_shape=jax.ShapeDtypeStruct((batch * seq, A_WIDTH), BF16),
        grid=(pairs, batch, nb),
        in_specs=[q_tile, whole, whole,
                  pl.BlockSpec((1, 2, seq + MOBA_BLOCK), lambda p, b, i: (p, 0, 0))],
        out_specs=pl.BlockSpec((MOBA_BLOCK, 128), lambda p, b, i: (b * nb + i, p)),
        scratch_shapes=[pltpu.VMEM((nb, MOBA_BLOCK, 256), BF16),
                        pltpu.VMEM((nb, 2, A_HEAD_DIM + MOBA_ONES_ROWS, MOBA_BLOCK), BF16),
                        pltpu.VMEM((nb, 128), F32),
                        pltpu.VMEM((2, nb, MOBA_BLOCK, MOBA_BLOCK), F32),
                        pltpu.VMEM((2, nb, MOBA_BLOCK, MOBA_BLOCK), F32)],
        compiler_params=_params(("arbitrary", "arbitrary", "arbitrary")),
        name="moba_prompt",
    )(q_t, k_t, v_t, w_pairs)


def _moba_sample_kernel(*refs, nbk, dec_seq, bpg):
    for stage in _sample_stages(pl.program_id(0), pl.num_programs(0), *refs, nbk=nbk, dec_seq=dec_seq, bpg=bpg):
        stage()


def _sample_stages(b, n_seq, pt_ref, q_ref, kn_ref, vn_ref, sbt_ref, ob_ref, ck_ref, cv_ref, o_ref,
                   kbuf, vbuf, sem, m_sc, l_sc, acc_sc, g_sc, *, nbk, dec_seq, bpg):
    n_groups = nbk // bpg
    ppg = 2 * bpg
    rows = dec_seq * A_HEADS
    head_of_row = lax.broadcasted_iota(jnp.int32, (A_HEADS, A_WIDTH), 0)
    head_of_lane = _div_pow2(lax.broadcasted_iota(jnp.int32, (A_HEADS, A_WIDTH), 1), A_HEAD_DIM)
    head_mask = head_of_row == head_of_lane
    wide = lambda a: jnp.broadcast_to(a, (rows, 128))

    def group_copies(seq, g, slot):
        out = []
        for r in range(ppg):
            page = pt_ref[seq * (n_groups * ppg) + g * ppg + r]
            out.append(pltpu.make_async_copy(ck_ref.at[page], kbuf.at[slot, r], sem.at[slot, 0]))
            out.append(pltpu.make_async_copy(cv_ref.at[page], vbuf.at[slot, r], sem.at[slot, 1]))
        return out

    def queries():
        return jnp.concatenate(
            [jnp.where(head_mask, jnp.broadcast_to(q_ref[0, t:t + 1, :], head_mask.shape), 0.0)
             for t in range(dec_seq)], axis=0).astype(BF16)

    live = {}

    def group_keys(g):
        slot = (b * n_groups + g) & 1
        if g == 0:
            @pl.when(b == 0)
            def _first_group():
                for cp in group_copies(0, 0, 0):
                    cp.start()
        if g + 1 < n_groups:
            for cp in group_copies(b, g + 1, 1 - slot):
                cp.start()
        else:
            @pl.when(b + 1 < n_seq)
            def _next_sequence():
                for cp in group_copies(b + 1, 0, 1 - slot):
                    cp.start()
        for cp in group_copies(b, g, slot):
            cp.wait()
        q16 = queries()
        live["raw"] = jnp.concatenate(
            [_dot(q16, jnp.concatenate([kbuf[slot, 2 * r], kbuf[slot, 2 * r + 1]], axis=1).astype(BF16))
             for r in range(bpg)], axis=1)

    per_block = lambda a: [a[:, r * MOBA_BLOCK:(r + 1) * MOBA_BLOCK] for r in range(bpg)]

    def group_softmax(g):
        width = bpg * MOBA_BLOCK
        s = per_block(live["raw"] + sbt_ref[:, g * width:(g + 1) * width])
        live["m"] = [jnp.max(a, axis=1, keepdims=True) for a in s]
        live["p"] = [jnp.exp(a - mr) for a, mr in zip(s, live["m"])]

    def group_values(g):
        slot = (b * n_groups + g) & 1
        raw, m, p = per_block(live.pop("raw")), live.pop("m"), live.pop("p")
        for r in range(bpg):
            n = g * bpg + r
            vblk = jnp.concatenate([vbuf[slot, 2 * r], vbuf[slot, 2 * r + 1]], axis=1).astype(BF16)
            m_sc[n] = wide(m[r])
            l_sc[n] = wide(jnp.sum(p[r], axis=1, keepdims=True))
            acc_sc[n] = _dot_nt(p[r].astype(BF16), vblk)
            g_sc[n] = wide(jnp.sum(raw[r], axis=1, keepdims=True) * (1.0 / MOBA_BLOCK))

    def merge():
        q16 = queries()
        gates = [g_sc[n] for n in range(nbk)]
        picked = [jnp.zeros((rows, 128), jnp.bool_)] * nbk
        for _ in range(min(MOBA_TOPK, nbk)):
            mx = functools.reduce(jnp.maximum, gates)
            first = functools.reduce(
                jnp.minimum, [jnp.where(g == mx, float(n), float(nbk)) for n, g in enumerate(gates)])
            for n in range(nbk):
                chosen = first == float(n)
                picked[n] = picked[n] | (chosen & (mx > -jnp.inf))
                gates[n] = jnp.where(chosen, -jnp.inf, gates[n])
        zeros = jnp.zeros((128 - kn_ref.shape[1], A_WIDTH), F32)
        k_own = jnp.concatenate([kn_ref[0], zeros], axis=0).astype(BF16)
        v_own = jnp.concatenate([vn_ref[0], zeros], axis=0).astype(BF16)
        s = _dot_nt(q16, k_own) + ob_ref[...]
        t_of_row = _div_pow2(lax.broadcasted_iota(jnp.int32, s.shape, 0), A_HEADS)
        s_idx = lax.broadcasted_iota(jnp.int32, s.shape, 1)
        s = jnp.where(s_idx <= t_of_row, s, NEG_INF)
        m_own = wide(jnp.max(s, axis=1, keepdims=True))
        p = jnp.exp(s - m_own)
        l_own = wide(jnp.sum(p, axis=1, keepdims=True))
        acc_own = _dot(p.astype(BF16), v_own)
        m_all = m_own
        for n in range(nbk):
            m_all = jnp.maximum(m_all, jnp.where(picked[n], m_sc[n], -jnp.inf))
        lanes4 = lambda a: jnp.concatenate([a] * (A_WIDTH // 128), axis=1)
        w = jnp.exp(m_own - m_all)
        l_all = w * l_own
        acc = lanes4(w) * acc_own
        for n in range(nbk):
            w = jnp.where(picked[n], jnp.exp(m_sc[n] - m_all), 0.0)
            l_all = l_all + w * l_sc[n]
            acc = acc + lanes4(w) * acc_sc[n]
        res = acc / lanes4(l_all)
        out = [jnp.sum(jnp.where(head_mask, res[t * A_HEADS:(t + 1) * A_HEADS, :], 0.0),
                       axis=0, keepdims=True) for t in range(dec_seq)]
        o_ref[0] = jnp.concatenate(out, axis=0).astype(o_ref.dtype)

    return [functools.partial(stage, g) for g in range(n_groups)
            for stage in (group_keys, group_softmax, group_values)] + [merge]


def _moba_sample(q, k_new, v_new, cache_k, cache_v, page_table, sbt, ob, bpg, ffn=None):
    db, dec_seq, _ = q.shape
    n_pages = page_table.shape[1]
    nbk = n_pages * PAGE_SIZE // MOBA_BLOCK
    rows = dec_seq * A_HEADS
    assert MOBA_BLOCK == 2 * PAGE_SIZE and nbk % bpg == 0
    per_seq = lambda b, pt: (b, 0, 0)
    whole = lambda a: pl.BlockSpec(a.shape, lambda b, pt: (0,) * a.ndim, pipeline_mode=pl.Buffered(1))
    sample_specs = [pl.BlockSpec((1, dec_seq, A_WIDTH), per_seq),
                    pl.BlockSpec((1, k_new.shape[1], A_WIDTH), per_seq),
                    pl.BlockSpec((1, v_new.shape[1], A_WIDTH), per_seq),
                    whole(sbt), whole(ob),
                    pl.BlockSpec(memory_space=pl.ANY), pl.BlockSpec(memory_space=pl.ANY)]
    sample_args = (q, k_new, v_new, sbt, ob, cache_k, cache_v)
    attn_spec = pl.BlockSpec((1, dec_seq, A_WIDTH), per_seq)
    attn_shape = jax.ShapeDtypeStruct((db, dec_seq, A_WIDTH), BF16)
    scratch = [pltpu.VMEM((2, 2 * bpg, A_WIDTH, PAGE_SIZE), F32),
               pltpu.VMEM((2, 2 * bpg, A_WIDTH, PAGE_SIZE), F32),
               pltpu.SemaphoreType.DMA((2, 2)),
               pltpu.VMEM((nbk, rows, 128), F32),
               pltpu.VMEM((nbk, rows, 128), F32),
               pltpu.VMEM((nbk, rows, A_WIDTH), F32),
               pltpu.VMEM((nbk, rows, 128), F32)]
    static = dict(nbk=nbk, dec_seq=dec_seq, bpg=bpg)
    if ffn is None:
        return pl.pallas_call(
            functools.partial(_moba_sample_kernel, **static),
            out_shape=attn_shape,
            grid_spec=pltpu.PrefetchScalarGridSpec(num_scalar_prefetch=1, grid=(db,), in_specs=sample_specs,
                                                   out_specs=attn_spec, scratch_shapes=scratch),
            compiler_params=_params(("arbitrary",)),
            name="moba_sample",
        )(page_table.reshape(-1), *sample_args)
    x, gain, wg, wu, wd, tm = ffn
    assert x.shape[0] == db * tm
    row = pl.BlockSpec((tm, D_MODEL), lambda b, pt: (b, 0))
    return pl.pallas_call(
        functools.partial(_ffn_sample_kernel, **static),
        out_shape=(jax.ShapeDtypeStruct(x.shape, F32), attn_shape),
        grid_spec=pltpu.PrefetchScalarGridSpec(
            num_scalar_prefetch=1, grid=(db,),
            in_specs=[row] + [whole(a) for a in (gain, wg, wu, wd)] + sample_specs,
            out_specs=(row, attn_spec), scratch_shapes=scratch),
        compiler_params=pltpu.CompilerParams(dimension_semantics=("arbitrary",),
                                             vmem_limit_bytes=VMEM_LIMIT_FUSED),
        name="ffn_sample",
    )(page_table.reshape(-1), x, gain, wg, wu, wd, *sample_args)


def _hgrn_kernel(q_ref, k_ref, v_ref, g_ref, gate_ref, s0_ref, og_ref, tril_ref, pick_ref, rec_ref, sfin_ref,
                 st_ref, cum_ref, *, n_chunks):
    tb = pl.program_id(2)
    heads = range(q_ref.shape[1] // B_DIM)

    @pl.when(tb == 0)
    def _load_state():
        for h in heads:
            st_ref[h] = s0_ref[0, h].T

    head = lambda ref, h: ref.at[:, h * B_DIM:(h + 1) * B_DIM]
    used = n_chunks * HG_CHUNK
    work, terms = [], []
    for h in heads:
        o_blocks, off_blocks, term_blocks, v16_blocks, st_ref[h] = _hgrn_scores(
            head(q_ref, h), head(k_ref, h), head(v_ref, h), head(g_ref, h), tril_ref, st_ref[h], head(cum_ref, h),
            n_chunks)
        work.append((o_blocks, off_blocks, v16_blocks))
        terms += term_blocks
    half = (len(terms) + 1) // 2
    inside = [_dot(jnp.concatenate(part, axis=0), pick_ref[...]) for part in (terms[:half], terms[half:]) if part]
    inside = [a[at * used:(at + 1) * used] for a in inside for at in range(a.shape[0] // used)]
    for h in heads:
        n_blocks = len(work[h][0])
        _hgrn_finish(*work[h], inside[h * n_blocks:(h + 1) * n_blocks], head(gate_ref, h), head(og_ref, h),
                     head(rec_ref, h), n_chunks)

    @pl.when(tb == pl.num_programs(2) - 1)
    def _store_state():
        for h in heads:
            sfin_ref[0, h] = st_ref[h].T


def _hgrn_scores(q_ref, k_ref, v_ref, g_ref, tril_ref, st, cum_ref, n_chunks):
    n_blocks = q_ref.shape[0] // HG_BLOCK
    used = n_chunks * HG_CHUNK
    blocks = [slice(bi * HG_BLOCK, (bi + 1) * HG_BLOCK) for bi in range(n_blocks)]

    tril = tril_ref[...]
    g_all = jnp.concatenate([g_ref[rows, :] for rows in blocks], axis=1) * LOG2_E
    g1, g2, g3 = _split3(g_all)
    cum_all = _dot(tril, g1) + _dot(tril, g2) + _dot(tril, g3)

    o_blocks, off_blocks, term_blocks, v16_blocks = [], [], [], []
    for bi, rows in enumerate(blocks):
        q, k, v = q_ref[rows, :], k_ref[rows, :], v_ref[rows, :]
        cum = cum_all[:, bi * B_DIM:(bi + 1) * B_DIM]
        last = cum[HG_BLOCK - 1:HG_BLOCK, :]
        v16_blocks.append(v.astype(BF16))
        o_blocks.append(_dot_nt((q * jnp.exp2(cum)).astype(BF16), st.astype(BF16)))
        st = st * jnp.exp2(last) + _dot(v.T.astype(BF16), (k * jnp.exp2(last - cum)).astype(BF16))
        parts = [jnp.zeros((HG_CHUNK, B_DIM), F32)]
        for c in range(1, n_chunks):
            lo = c * HG_CHUNK
            r = cum[lo - 1:lo, :]
            k_c = (k[:lo] * jnp.exp2(r - cum[:lo])).astype(BF16)
            k_c = jnp.concatenate([k_c, jnp.zeros((HG_BLOCK - lo, B_DIM), BF16)], axis=0)
            q_c = (q[lo:lo + HG_CHUNK] * jnp.exp2(cum[lo:lo + HG_CHUNK] - r)).astype(BF16)
            parts.append(_dot_nt(q_c, k_c))
        off_blocks.append(jnp.concatenate(parts, axis=0))
        cum_ref[rows, :] = cum

    assert used == HG_BLOCK or n_blocks == 1
    n_rows = n_blocks * used
    chunk_rows = lambda ref, s: jnp.concatenate(
        [jnp.broadcast_to(ref[c * HG_CHUNK + s:c * HG_CHUNK + s + 1, :], (HG_CHUNK, B_DIM))
         for c in range(n_rows // HG_CHUNK)], axis=0)
    row_in_chunk = lax.broadcasted_iota(jnp.int32, (n_rows, B_DIM), 0) & (HG_CHUNK - 1)
    q_rows, cum_rows = q_ref[:n_rows, :], cum_ref[:n_rows, :]
    terms = []
    for s in range(HG_CHUNK):
        decay = jnp.exp2(jnp.where(row_in_chunk >= s, cum_rows - chunk_rows(cum_ref, s), -jnp.inf))
        terms.append((q_rows * decay * chunk_rows(k_ref, s)).astype(BF16))
    terms = jnp.concatenate(terms, axis=1)
    term_blocks = [terms[bi * used:(bi + 1) * used] for bi in range(n_blocks)]
    return o_blocks, off_blocks, term_blocks, v16_blocks, st


def _hgrn_finish(o_blocks, off_blocks, v16_blocks, inside_blocks, gate_ref, og_ref, rec_ref, n_chunks):
    used = n_chunks * HG_CHUNK
    row = lax.broadcasted_iota(jnp.int32, (used, B_DIM), 0)
    col = lax.broadcasted_iota(jnp.int32, (used, B_DIM), 1)
    same_chunk = _div_pow2(row, HG_CHUNK) == _div_pow2(col, HG_CHUNK)
    ones = jnp.ones((B_DIM, B_DIM), BF16)
    out = []
    for o, off, v16, inside in zip(o_blocks, off_blocks, v16_blocks, inside_blocks):
        score = off + jnp.where(same_chunk, inside, 0.0)
        intra = _dot(score.astype(BF16), v16)
        if used < HG_BLOCK:
            intra = jnp.concatenate([intra, jnp.zeros((HG_BLOCK - used, B_DIM), F32)], axis=0)
        out.append(o + intra)
    o = jnp.concatenate(out, axis=0)
    ms = _dot((o * o).astype(BF16), ones) * (1.0 / B_DIM)
    gate = gate_ref[...]
    rec_ref[...] = ((o * lax.rsqrt(ms + NORM_EPS) * og_ref[...]) * _silu(gate)).astype(rec_ref.dtype)


def _hgrn(qb, kb, ib, lf, gb, s0, out_gain, batch, seq, tile, n_chunks, heads_per_step):
    nt = seq // tile
    width = heads_per_step * B_DIM
    tok = pl.BlockSpec((tile, width), lambda b, h, t: (b * nt + t, h))
    state = pl.BlockSpec((1, heads_per_step, B_DIM, B_DIM), lambda b, h, t: (b, h, 0, 0))
    idx = jnp.arange(HG_BLOCK)
    tril = (idx[:, None] >= idx[None, :]).astype(BF16)
    pick = (jnp.repeat(jnp.arange(HG_CHUNK), B_DIM)[:, None] == (idx % HG_CHUNK)[None, :]).astype(BF16)
    return pl.pallas_call(
        functools.partial(_hgrn_kernel, n_chunks=n_chunks),
        out_shape=(jax.ShapeDtypeStruct((batch * seq, B_WIDTH), BF16),
                   jax.ShapeDtypeStruct(s0.shape, F32)),
        grid=(batch, B_HEADS // heads_per_step, nt),
        in_specs=[tok, tok, tok, tok, tok, state, pl.BlockSpec((1, width), lambda b, h, t: (0, h)),
                  _resident(tril.shape), _resident(pick.shape)],
        out_specs=(tok, state),
        scratch_shapes=[pltpu.VMEM((heads_per_step, B_DIM, B_DIM), F32),
                        pltpu.VMEM((tile, width), F32)],
        compiler_params=_params(("arbitrary", "arbitrary", "arbitrary")),
        name="hgrn",
    )(qb, kb, ib, lf, gb, s0, out_gain, tril, pick)


def kernel(x_prompt, x_sample, cache_k, cache_v, state_hgrn, page_table, rel_bias_table, lb_logits,
           ffn1_norm, ffn1_gate, ffn1_up, ffn1_down, mix_norm, w_in, q_norm, k_norm, hgrn_out_norm,
           w_out, ffn2_norm, ffn2_gate, ffn2_up, ffn2_down):
    batch, seq, _ = x_prompt.shape
    db, dec_seq, _ = x_sample.shape
    depth, n_phys = cache_k.shape[:2]
    assert depth == 1 and lb_logits.shape[0] == 2
    past = page_table.shape[1] * PAGE_SIZE
    assert seq % MOBA_BLOCK == 0 and past % MOBA_BLOCK == 0 and dec_seq <= HG_CHUNK
    n_p, n_s = batch * seq, db * dec_seq

    bf = lambda w: w[0].astype(BF16)
    gain = lambda g: g[0].reshape(1, -1)
    ffn1 = (gain(ffn1_norm), bf(ffn1_gate), bf(ffn1_up), bf(ffn1_down))
    ffn2 = (gain(ffn2_norm), bf(ffn2_gate), bf(ffn2_up), bf(ffn2_down))
    wo = bf(w_out)
    w_in16 = bf(w_in)
    per_lane = lambda g: jnp.broadcast_to(jnp.tile(g[0], A_HEADS)[:, None], (A_WIDTH, 128))
    proj_w = (gain(mix_norm), w_in16[:, :3 * GROUP_W].T, w_in16[:, 3 * GROUP_W:],
              per_lane(q_norm), per_lane(k_norm), lb_logits)
    out_gain = hgrn_out_norm[0].reshape(1, -1)
    w_bias, sbt, ob = _bias_tables(rel_bias_table.T, seq, past, dec_seq)

    ys = _ffn(x_sample.reshape(n_s, D_MODEL), *ffn1, n_s)
    q_t, k_t, v_t, qb_s, kb_s, ib_s, lf_s, gb_s = _proj(ys, *proj_w, n_s, n_s)
    q_s, k_s, v_s = (a.T.reshape(db, dec_seq, A_WIDTH) for a in (q_t, k_t, v_t))
    pad_rows = lambda a, r: jnp.pad(a.reshape(db, dec_seq, -1), ((0, 0), (0, r - dec_seq), (0, 0)))
    pages = lambda c: c[0].transpose(0, 2, 3, 1).reshape(n_phys, A_WIDTH, PAGE_SIZE)
    sample_attn = functools.partial(
        _moba_sample, q_s, pad_rows(k_s, 8), pad_rows(v_s, 8), pages(cache_k), pages(cache_v), page_table, sbt, ob,
        bpg=math.gcd(past // MOBA_BLOCK, SAMPLE_BLOCKS_PER_STEP))

    tm = 512
    x_p = x_prompt.reshape(n_p, D_MODEL)
    if n_p // tm == db:
        y, attn_s = sample_attn(ffn=(x_p, *ffn1, tm))
    else:
        y, attn_s = _ffn(x_p, *ffn1, tm), sample_attn()
    q_t, kp_t, vp_t, qb, kb, ib, lf, gb = _proj(y, *proj_w, seq, tm)
    attn = _moba_prompt(q_t, kp_t, vp_t, w_bias.reshape(A_HEADS // 2, 2, -1), batch, seq)
    s0 = jnp.zeros((batch, B_HEADS, B_DIM, B_DIM), F32)
    rec, s_p = _hgrn(qb, kb, ib, lf, gb, s0, out_gain, batch, seq, math.gcd(seq, HGRN_TILE), HG_BLOCK // HG_CHUNK, 1)
    y_p = _ffn(y, *ffn2, tm, mix=(attn, rec, wo))
    heads_p = lambda a: a.reshape(batch, A_HEADS, A_HEAD_DIM, seq).transpose(0, 3, 1, 2)[None]

    blocked = lambda a: pad_rows(a, HG_BLOCK).reshape(db * HG_BLOCK, -1)
    rec, s_s = _hgrn(blocked(qb_s), blocked(kb_s), blocked(ib_s), blocked(lf_s), blocked(gb_s), state_hgrn[0],
                     out_gain, db, HG_BLOCK, HG_BLOCK, 1, B_HEADS)
    rec = rec.reshape(db, HG_BLOCK, B_WIDTH)[:, :dec_seq].reshape(n_s, B_WIDTH)
    y_s = _ffn(ys, *ffn2, n_s, mix=(attn_s.reshape(n_s, A_WIDTH), rec, wo))
    heads_s = lambda a: a.reshape(1, db, dec_seq, A_HEADS, A_HEAD_DIM)

    return (y_p.reshape(batch, seq, D_MODEL), y_s.reshape(db, dec_seq, D_MODEL),
            heads_p(kp_t), heads_p(vp_t), s_p[None],
            heads_s(k_s), heads_s(v_s), s_s[None])
```

```python
import functools
import math

import jax
import jax.numpy as jnp
from jax import lax
from jax.experimental import pallas as pl
from jax.experimental.pallas import tpu as pltpu

F32 = jnp.float32
BF16 = jnp.bfloat16

D_MODEL = 1024
A_HEADS = 8
A_HEAD_DIM = 64
A_WIDTH = A_HEADS * A_HEAD_DIM
B_HEADS = 4
B_DIM = 128
B_WIDTH = B_HEADS * B_DIM
GROUP_W = 512
N_GROUPS = 7
D_FF = 2816
MOBA_BLOCK = 256
MOBA_TOPK = 3
PAGE_SIZE = 128
N_BUCKETS = 32
MAX_EXACT = N_BUCKETS // 2
REL_MAX_DIST = 4096
HG_CHUNK = 16
HG_BLOCK = 128
NORM_EPS = 1e-6
NEG_INF = -1e30
ATTN_SCALE = A_HEAD_DIM ** -0.5
LOG2_E = math.log2(math.e)

FF_CHUNK = 256
SAMPLE_BLOCKS_PER_STEP = 8
MOBA_BLOCKS_PER_TRIP = 8
HGRN_TILE = 2048
MOBA_ONES_ROWS = 16
VMEM_LIMIT = 48 * 1024 * 1024
VMEM_LIMIT_FUSED = 56 * 1024 * 1024


def _resident(shape):
    nd = len(shape)
    return pl.BlockSpec(shape, lambda *_: (0,) * nd, pipeline_mode=pl.Buffered(1))


def _params(semantics):
    return pltpu.CompilerParams(dimension_semantics=semantics, vmem_limit_bytes=VMEM_LIMIT)


def _dot(a, b):
    return jnp.dot(a, b, preferred_element_type=F32)


def _dot_nt(a, b):
    return lax.dot_general(a, b, (((1,), (1,)), ((), ())), preferred_element_type=F32)


def _split3(a):
    a1 = a.astype(BF16)
    r1 = a - a1.astype(F32)
    a2 = r1.astype(BF16)
    a3 = (r1 - a2.astype(F32)).astype(BF16)
    return a1, a2, a3


def _div_pow2(x, d):
    return lax.shift_right_logical(x, int(math.log2(d)))


def _rms_rows(x, gain):
    ms = jnp.mean(x * x, axis=-1, keepdims=True)
    return x * lax.rsqrt(ms + NORM_EPS) * gain


def _silu(x):
    return x * jax.nn.sigmoid(x)


def _ffn_chunks(h, wg_ref, wu_ref, wd_ref, acc, chunks):
    for j in chunks:
        cols = slice(j * FF_CHUNK, (j + 1) * FF_CHUNK)
        g = _dot(h, wg_ref[:, cols])
        u = _dot(h, wu_ref[:, cols])
        acc = acc + _dot((_silu(g) * u).astype(BF16), wd_ref[cols, :])
    return acc


def _ffn_kernel(*refs, with_mix):
    if with_mix:
        x_ref, attn_ref, rec_ref, wo_ref, gain_ref, wg_ref, wu_ref, wd_ref, o_ref = refs
        y = (x_ref[...] + _dot(attn_ref[...], wo_ref[:A_WIDTH, :])
             + _dot(rec_ref[...], wo_ref[A_WIDTH:, :]))
    else:
        x_ref, gain_ref, wg_ref, wu_ref, wd_ref, o_ref = refs
        y = x_ref[...]
    h = _rms_rows(y, gain_ref[...]).astype(BF16)
    acc = _ffn_chunks(h, wg_ref, wu_ref, wd_ref, jnp.zeros(y.shape, F32), range(D_FF // FF_CHUNK))
    o_ref[...] = y + 0.5 * acc


def _ffn_sample_kernel(pt_ref, x_ref, gain_ref, wg_ref, wu_ref, wd_ref, q_ref, kn_ref, vn_ref, sbt_ref, ob_ref,
                       ck_ref, cv_ref, y_ref, attn_ref, *scratch, nbk, dec_seq, bpg):
    stages = _sample_stages(pl.program_id(0), pl.num_programs(0), pt_ref, q_ref, kn_ref, vn_ref, sbt_ref, ob_ref,
                            ck_ref, cv_ref, attn_ref, *scratch, nbk=nbk, dec_seq=dec_seq, bpg=bpg)
    y = x_ref[...]
    h = _rms_rows(y, gain_ref[...]).astype(BF16)
    acc = jnp.zeros(y.shape, F32)
    n_chunks = D_FF // FF_CHUNK
    cuts = [k * n_chunks // len(stages) for k in range(len(stages) + 1)]
    for k, stage in enumerate(stages):
        stage()
        acc = _ffn_chunks(h, wg_ref, wu_ref, wd_ref, acc, range(cuts[k], cuts[k + 1]))
    y_ref[...] = y + 0.5 * acc


def _ffn(x, gain, wg, wu, wd, tm, mix=None):
    n = x.shape[0]
    row = lambda w: pl.BlockSpec((tm, w), lambda i: (i, 0))
    args, specs = [x], [row(D_MODEL)]
    if mix is not None:
        attn, rec, wo = mix
        args += [attn, rec, wo]
        specs += [row(A_WIDTH), row(B_WIDTH), _resident(wo.shape)]
    args += [gain, wg, wu, wd]
    specs += [_resident(gain.shape), _resident(wg.shape), _resident(wu.shape), _resident(wd.shape)]
    return pl.pallas_call(
        functools.partial(_ffn_kernel, with_mix=mix is not None),
        out_shape=jax.ShapeDtypeStruct((n, D_MODEL), F32),
        grid=(n // tm,),
        in_specs=specs,
        out_specs=row(D_MODEL),
        compiler_params=_params(("parallel",)),
        name="ffn_mix" if mix is not None else "ffn",
    )(*args)


def _proj_kernel(y_ref, gain_ref, wat_ref, wb_ref, qg_ref, kg_ref, lbl_ref,
                 q_ref, k_ref, v_ref, qb_ref, kb_ref, ib_ref, lf_ref, gb_ref):
    h = _rms_rows(y_ref[...], gain_ref[...]).astype(BF16)
    tm = h.shape[0]

    def group_t(i):
        return _dot_nt(wat_ref[i * GROUP_W:(i + 1) * GROUP_W, :], h)

    def group(i):
        return _dot(h, wb_ref[:, i * GROUP_W:(i + 1) * GROUP_W])

    def head_norm(z, g):
        out = []
        for hd in range(A_HEADS):
            zh = z[hd * A_HEAD_DIM:(hd + 1) * A_HEAD_DIM, :]
            out.append(zh * lax.rsqrt(jnp.mean(zh * zh, axis=0, keepdims=True) + NORM_EPS))
        return jnp.concatenate(out, axis=0) * jnp.concatenate([g] * pl.cdiv(tm, 128), axis=1)[:, :tm]

    q_ref[...] = head_norm(group_t(0), qg_ref[...]) * ATTN_SCALE
    k_ref[...] = head_norm(group_t(1), kg_ref[...])
    v_ref[...] = group_t(2)
    qb_ref[...] = _silu(group(0))
    lbl = lbl_ref[...]
    e = jnp.exp(lbl - jnp.max(lbl, axis=0, keepdims=True))
    lb = e[0:1, :] / jnp.sum(e, axis=0, keepdims=True)
    forget = lb + (1.0 - lb) * jax.nn.sigmoid(group(1))
    lf_ref[...] = jnp.log(forget)
    kb_ref[...] = 1.0 - forget
    ib_ref[...] = group(2)
    gb_ref[...] = group(3)


def _proj(y, gain, wat, wb, qg, kg, lbl, seq, tm):
    n = y.shape[0]
    nt = seq // tm
    row = lambda w: pl.BlockSpec((tm, w), lambda i: (i, 0))
    col = pl.BlockSpec((GROUP_W, tm), lambda i: (i // nt, i % nt))
    out_t = jax.ShapeDtypeStruct((n // seq * GROUP_W, seq), F32)
    out = jax.ShapeDtypeStruct((n, GROUP_W), F32)
    return pl.pallas_call(
        _proj_kernel,
        out_shape=(out_t,) * 3 + (out,) * 5,
        grid=(n // tm,),
        in_specs=[row(D_MODEL)] + [_resident(a.shape) for a in (gain, wat, wb, qg, kg, lbl)],
        out_specs=(col,) * 3 + (row(GROUP_W),) * 5,
        compiler_params=_params(("parallel",)),
        name="proj",
    )(y, gain, wat, wb, qg, kg, lbl)


def _rel_bucket(n):
    nf = jnp.maximum(n, 1).astype(F32)
    large = MAX_EXACT + (jnp.log(nf / MAX_EXACT) / math.log(REL_MAX_DIST / MAX_EXACT)
                         * (N_BUCKETS - MAX_EXACT)).astype(jnp.int32)
    large = jnp.minimum(large, N_BUCKETS - 1)
    return jnp.where(n < MAX_EXACT, n, large)


def _bias_kernel(tab_ref, w_ref, sbt_ref, ob_ref, *, past, dec_seq):
    tab = tab_ref[...]
    tab_rows = jnp.concatenate([tab] * dec_seq, axis=0)

    def lookup(bucket, table):
        out = jnp.zeros(bucket.shape, F32)
        for b in range(N_BUCKETS):
            out = jnp.where(bucket == b, table[:, b:b + 1], out)
        return out

    c = lax.broadcasted_iota(jnp.int32, w_ref.shape, 1)
    w_ref[...] = lookup(_rel_bucket(jnp.maximum(c - MOBA_BLOCK, 0)), tab)
    t = _div_pow2(lax.broadcasted_iota(jnp.int32, sbt_ref.shape, 0), A_HEADS)
    kpos = lax.broadcasted_iota(jnp.int32, sbt_ref.shape, 1)
    sbt_ref[...] = lookup(_rel_bucket(jnp.maximum(past + t - kpos, 0)), tab_rows)
    t = _div_pow2(lax.broadcasted_iota(jnp.int32, ob_ref.shape, 0), A_HEADS)
    s = lax.broadcasted_iota(jnp.int32, ob_ref.shape, 1)
    ob_ref[...] = lookup(_rel_bucket(jnp.maximum(t - s, 0)), tab_rows)


def _bias_tables(tab_t, seq, past, dec_seq):
    rows = dec_seq * A_HEADS
    return pl.pallas_call(
        functools.partial(_bias_kernel, past=past, dec_seq=dec_seq),
        out_shape=(jax.ShapeDtypeStruct((A_HEADS, seq + MOBA_BLOCK), F32),
                   jax.ShapeDtypeStruct((rows, past), F32),
                   jax.ShapeDtypeStruct((rows, 128), F32)),
        name="rel_bias",
    )(tab_t)


def _top_k_mask(gate, k, axis):
    index = lax.broadcasted_iota(jnp.int32, gate.shape, axis).astype(F32)
    sel = jnp.zeros(gate.shape, F32)
    for _ in range(k):
        mx = jnp.max(gate, axis=axis, keepdims=True)
        first = jnp.min(jnp.where(gate == mx, index, float(gate.shape[axis])), axis=axis, keepdims=True)
        chosen = index == first
        sel = jnp.where(chosen & (mx > -jnp.inf), 1.0, sel)
        gate = jnp.where(chosen, -jnp.inf, gate)
    return sel


def _moba_prompt_kernel(q_ref, k_ref, v_ref, w_ref, o_ref, kb_ref, vt_ref, km_ref, tt_ref, s_ref, *, nb):
    b = pl.program_id(1)
    i = pl.program_id(2)
    blk = MOBA_BLOCK

    key_j = lax.broadcasted_iota(jnp.int32, (blk, blk), 0)
    qry_c = lax.broadcasted_iota(jnp.int32, (blk, blk), 1)

    @pl.when((b == 0) & (i == 0))
    def _build_bias_tiles():
        for hh in range(2):
            for d in range(nb):
                win = jnp.broadcast_to(w_ref[0, hh:hh + 1, d * blk:(d + 2) * blk], (blk, 2 * blk))
                tile = pltpu.roll(win, 0, 1, stride=1, stride_axis=0)[:, blk:] * LOG2_E
                tt_ref[hh, d] = jnp.where(qry_c >= key_j, tile, NEG_INF) if d == 0 else tile

    @pl.when(i == 0)
    def _prepare_sequence():
        lane = lax.broadcasted_iota(jnp.int32, (blk, 128), 1)
        ones = jnp.ones((MOBA_ONES_ROWS, blk), F32)
        for n in range(nb):
            cols = slice(n * blk, (n + 1) * blk)
            kblk = k_ref[:, cols].T
            kb_ref[n] = jnp.concatenate([kblk, jnp.where(lane == n, 1.0, 0.0)], axis=1).astype(BF16)
            km_ref[n:n + 1, :] = jnp.mean(kblk, axis=0, keepdims=True)
            for hh in range(2):
                v_h = v_ref[hh * A_HEAD_DIM:(hh + 1) * A_HEAD_DIM, cols]
                vt_ref[n, hh] = jnp.concatenate([v_h, ones], axis=0).astype(BF16)

    q_t = q_ref[...] * LOG2_E
    feat = lax.broadcasted_iota(jnp.int32, q_t.shape, 0)
    km = km_ref[...]
    km_lane = lax.broadcasted_iota(jnp.int32, km.shape, 1)
    blk_id = lax.broadcasted_iota(jnp.int32, (nb, blk), 0)

    fold = lambda a, op: op(a.reshape(blk // 8, 8, blk), axis=0)

    qp = []
    for hh in range(2):
        lo = hh * A_HEAD_DIM
        q_h = jnp.where((feat >= lo) & (feat < lo + A_HEAD_DIM), q_t, 0.0)
        km_h = jnp.where((km_lane >= lo) & (km_lane < lo + A_HEAD_DIM), km, 0.0)
        k1, k2, _ = _split3(km_h)
        q1, q2, _ = _split3(q_h)
        gate = _dot(k1, q1) + _dot(k1, q2) + _dot(k2, q1)
        gate = jnp.where(blk_id < i, gate, -jnp.inf)
        attends = jnp.where(blk_id == i, 1.0, _top_k_mask(gate, MOBA_TOPK, 0))
        penalty = jnp.concatenate([jnp.where(attends > 0.0, 0.0, NEG_INF),
                                   jnp.zeros((128 - nb, blk), F32)], axis=0)
        qp.append(jnp.concatenate([q1, penalty.astype(BF16)], axis=0))

    per_trip = math.gcd(nb, MOBA_BLOCKS_PER_TRIP)
    full_trips = _div_pow2(i + 1, per_trip)
    left_over = (i + 1) & (per_trip - 1)

    def pieces(left):
        padded = left + 1
        if left > 2 and padded & (padded - 1) == 0:
            return [padded]
        return [bit for bit in (per_trip >> k for k in range(1, per_trip.bit_length())) if left & bit]

    def walk(body, carry):
        carry = lax.fori_loop(0, full_trips, lambda j, c: body(per_trip * j, per_trip, c), carry)

        def finish(left):
            def run(c):
                first = per_trip * full_trips
                for count in pieces(left):
                    c = body(first, count, c)
                    first = first + count
                return c
            return run

        return lax.switch(left_over, [finish(left) for left in range(per_trip)], carry)

    pairs = lambda first, count: [(first + u, hh) for u in range(count) for hh in range(2)]

    def logits(first, count, m8):
        raw = [_dot(kb_ref[n], qp[hh]) for n, hh in pairs(first, count)]
        out = list(m8)
        for (n, hh), r in zip(pairs(first, count), raw):
            s = r + tt_ref[hh, jnp.maximum(i - n, 0)]
            s_ref[hh, n] = s
            out[hh] = jnp.maximum(out[hh], fold(s, jnp.max))
        return tuple(out)

    m8 = walk(logits, (jnp.full((8, blk), -jnp.inf, F32),) * 2)
    m = [jnp.max(a, axis=0, keepdims=True) for a in m8]

    def values(first, count, acc):
        p = [jnp.exp2(s_ref[hh, n] - m[hh]).astype(BF16) for n, hh in pairs(first, count)]
        out = list(acc)
        for (n, hh), ph in zip(pairs(first, count), p):
            out[hh] = out[hh] + _dot(vt_ref[n, hh], ph)
        return tuple(out)

    acc = walk(values, (jnp.zeros((A_HEAD_DIM + MOBA_ONES_ROWS, blk), F32),) * 2)
    out_t = jnp.concatenate([a[:A_HEAD_DIM] / a[A_HEAD_DIM:A_HEAD_DIM + 1] for a in acc], axis=0)
    o_ref[...] = out_t.T.astype(o_ref.dtype)


def _moba_prompt(q_t, k_t, v_t, w_pairs, batch, seq):
    nb = seq // MOBA_BLOCK
    pairs = A_HEADS // 2
    q_tile = pl.BlockSpec((128, MOBA_BLOCK), lambda p, b, i: (b * pairs + p, i))
    whole = pl.BlockSpec((128, seq), lambda p, b, i: (b * pairs + p, 0))
    return pl.pallas_call(
        functools.partial(_moba_prompt_kernel, nb=nb),
        out_shape=jax.ShapeDtypeStruct((batch * seq, A_WIDTH), BF16),
        grid=(pairs, batch, nb),
        in_specs=[q_tile, whole, whole,
                  pl.BlockSpec((1, 2, seq + MOBA_BLOCK), lambda p, b, i: (p, 0, 0))],
        out_specs=pl.BlockSpec((MOBA_BLOCK, 128), lambda p, b, i: (b * nb + i, p)),
        scratch_shapes=[pltpu.VMEM((nb, MOBA_BLOCK, 256), BF16),
                        pltpu.VMEM((nb, 2, A_HEAD_DIM + MOBA_ONES_ROWS, MOBA_BLOCK), BF16),
                        pltpu.VMEM((nb, 128), F32),
                        pltpu.VMEM((2, nb, MOBA_BLOCK, MOBA_BLOCK), F32),
                        pltpu.VMEM((2, nb, MOBA_BLOCK, MOBA_BLOCK), F32)],
        compiler_params=_params(("arbitrary", "arbitrary", "arbitrary")),
        name="moba_prompt",
    )(q_t, k_t, v_t, w_pairs)


def _moba_sample_kernel(*refs, nbk, dec_seq, bpg):
    for stage in _sample_stages(pl.program_id(0), pl.num_programs(0), *refs, nbk=nbk, dec_seq=dec_seq, bpg=bpg):
        stage()


def _sample_stages(b, n_seq, pt_ref, q_ref, kn_ref, vn_ref, sbt_ref, ob_ref, ck_ref, cv_ref, o_ref,
                   kbuf, vbuf, sem, m_sc, l_sc, acc_sc, g_sc, *, nbk, dec_seq, bpg):
    n_groups = nbk // bpg
    ppg = 2 * bpg
    rows = dec_seq * A_HEADS
    head_of_row = lax.broadcasted_iota(jnp.int32, (A_HEADS, A_WIDTH), 0)
    head_of_lane = _div_pow2(lax.broadcasted_iota(jnp.int32, (A_HEADS, A_WIDTH), 1), A_HEAD_DIM)
    head_mask = head_of_row == head_of_lane
    wide = lambda a: jnp.broadcast_to(a, (rows, 128))

    def group_copies(seq, g, slot):
        out = []
        for r in range(ppg):
            page = pt_ref[seq * (n_groups * ppg) + g * ppg + r]
            out.append(pltpu.make_async_copy(ck_ref.at[page], kbuf.at[slot, r], sem.at[slot, 0]))
            out.append(pltpu.make_async_copy(cv_ref.at[page], vbuf.at[slot, r], sem.at[slot, 1]))
        return out

    def queries():
        return jnp.concatenate(
            [jnp.where(head_mask, jnp.broadcast_to(q_ref[0, t:t + 1, :], head_mask.shape), 0.0)
             for t in range(dec_seq)], axis=0).astype(BF16)

    live = {}

    def group_keys(g):
        slot = (b * n_groups + g) & 1
        if g == 0:
            @pl.when(b == 0)
            def _first_group():
                for cp in group_copies(0, 0, 0):
                    cp.start()
        if g + 1 < n_groups:
            for cp in group_copies(b, g + 1, 1 - slot):
                cp.start()
        else:
            @pl.when(b + 1 < n_seq)
            def _next_sequence():
                for cp in group_copies(b + 1, 0, 1 - slot):
                    cp.start()
        for cp in group_copies(b, g, slot):
            cp.wait()
        q16 = queries()
        live["raw"] = jnp.concatenate(
            [_dot(q16, jnp.concatenate([kbuf[slot, 2 * r], kbuf[slot, 2 * r + 1]], axis=1).astype(BF16))
             for r in range(bpg)], axis=1)

    per_block = lambda a: [a[:, r * MOBA_BLOCK:(r + 1) * MOBA_BLOCK] for r in range(bpg)]

    def group_softmax(g):
        width = bpg * MOBA_BLOCK
        s = per_block(live["raw"] + sbt_ref[:, g * width:(g + 1) * width])
        live["m"] = [jnp.max(a, axis=1, keepdims=True) for a in s]
        live["p"] = [jnp.exp(a - mr) for a, mr in zip(s, live["m"])]

    def group_values(g):
        slot = (b * n_groups + g) & 1
        raw, m, p = per_block(live.pop("raw")), live.pop("m"), live.pop("p")
        for r in range(bpg):
            n = g * bpg + r
            vblk = jnp.concatenate([vbuf[slot, 2 * r], vbuf[slot, 2 * r + 1]], axis=1).astype(BF16)
            m_sc[n] = wide(m[r])
            l_sc[n] = wide(jnp.sum(p[r], axis=1, keepdims=True))
            acc_sc[n] = _dot_nt(p[r].astype(BF16), vblk)
            g_sc[n] = wide(jnp.sum(raw[r], axis=1, keepdims=True) * (1.0 / MOBA_BLOCK))

    def merge():
        q16 = queries()
        gates = [g_sc[n] for n in range(nbk)]
        picked = [jnp.zeros((rows, 128), jnp.bool_)] * nbk
        for _ in range(min(MOBA_TOPK, nbk)):
            mx = functools.reduce(jnp.maximum, gates)
            first = functools.reduce(
                jnp.minimum, [jnp.where(g == mx, float(n), float(nbk)) for n, g in enumerate(gates)])
            for n in range(nbk):
                chosen = first == float(n)
                picked[n] = picked[n] | (chosen & (mx > -jnp.inf))
                gates[n] = jnp.where(chosen, -jnp.inf, gates[n])
        zeros = jnp.zeros((128 - kn_ref.shape[1], A_WIDTH), F32)
        k_own = jnp.concatenate([kn_ref[0], zeros], axis=0).astype(BF16)
        v_own = jnp.concatenate([vn_ref[0], zeros], axis=0).astype(BF16)
        s = _dot_nt(q16, k_own) + ob_ref[...]
        t_of_row = _div_pow2(lax.broadcasted_iota(jnp.int32, s.shape, 0), A_HEADS)
        s_idx = lax.broadcasted_iota(jnp.int32, s.shape, 1)
        s = jnp.where(s_idx <= t_of_row, s, NEG_INF)
        m_own = wide(jnp.max(s, axis=1, keepdims=True))
        p = jnp.exp(s - m_own)
        l_own = wide(jnp.sum(p, axis=1, keepdims=True))
        acc_own = _dot(p.astype(BF16), v_own)
        m_all = m_own
        for n in range(nbk):
            m_all = jnp.maximum(m_all, jnp.where(picked[n], m_sc[n], -jnp.inf))
        lanes4 = lambda a: jnp.concatenate([a] * (A_WIDTH // 128), axis=1)
        w = jnp.exp(m_own - m_all)
        l_all = w * l_own
        acc = lanes4(w) * acc_own
        for n in range(nbk):
            w = jnp.where(picked[n], jnp.exp(m_sc[n] - m_all), 0.0)
            l_all = l_all + w * l_sc[n]
            acc = acc + lanes4(w) * acc_sc[n]
        res = acc / lanes4(l_all)
        out = [jnp.sum(jnp.where(head_mask, res[t * A_HEADS:(t + 1) * A_HEADS, :], 0.0),
                       axis=0, keepdims=True) for t in range(dec_seq)]
        o_ref[0] = jnp.concatenate(out, axis=0).astype(o_ref.dtype)

    return [functools.partial(stage, g) for g in range(n_groups)
            for stage in (group_keys, group_softmax, group_values)] + [merge]


def _moba_sample(q, k_new, v_new, cache_k, cache_v, page_table, sbt, ob, bpg, ffn=None):
    db, dec_seq, _ = q.shape
    n_pages = page_table.shape[1]
    nbk = n_pages * PAGE_SIZE // MOBA_BLOCK
    rows = dec_seq * A_HEADS
    assert MOBA_BLOCK == 2 * PAGE_SIZE and nbk % bpg == 0
    per_seq = lambda b, pt: (b, 0, 0)
    whole = lambda a: pl.BlockSpec(a.shape, lambda b, pt: (0,) * a.ndim, pipeline_mode=pl.Buffered(1))
    sample_specs = [pl.BlockSpec((1, dec_seq, A_WIDTH), per_seq),
                    pl.BlockSpec((1, k_new.shape[1], A_WIDTH), per_seq),
                    pl.BlockSpec((1, v_new.shape[1], A_WIDTH), per_seq),
                    whole(sbt), whole(ob),
                    pl.BlockSpec(memory_space=pl.ANY), pl.BlockSpec(memory_space=pl.ANY)]
    sample_args = (q, k_new, v_new, sbt, ob, cache_k, cache_v)
    attn_spec = pl.BlockSpec((1, dec_seq, A_WIDTH), per_seq)
    attn_shape = jax.ShapeDtypeStruct((db, dec_seq, A_WIDTH), BF16)
    scratch = [pltpu.VMEM((2, 2 * bpg, A_WIDTH, PAGE_SIZE), F32),
               pltpu.VMEM((2, 2 * bpg, A_WIDTH, PAGE_SIZE), F32),
               pltpu.SemaphoreType.DMA((2, 2)),
               pltpu.VMEM((nbk, rows, 128), F32),
               pltpu.VMEM((nbk, rows, 128), F32),
               pltpu.VMEM((nbk, rows, A_WIDTH), F32),
               pltpu.VMEM((nbk, rows, 128), F32)]
    static = dict(nbk=nbk, dec_seq=dec_seq, bpg=bpg)
    if ffn is None:
        return pl.pallas_call(
            functools.partial(_moba_sample_kernel, **static),
            out_shape=attn_shape,
            grid_spec=pltpu.PrefetchScalarGridSpec(num_scalar_prefetch=1, grid=(db,), in_specs=sample_specs,
                                                   out_specs=attn_spec, scratch_shapes=scratch),
            compiler_params=_params(("arbitrary",)),
            name="moba_sample",
        )(page_table.reshape(-1), *sample_args)
    x, gain, wg, wu, wd, tm = ffn
    assert x.shape[0] == db * tm
    row = pl.BlockSpec((tm, D_MODEL), lambda b, pt: (b, 0))
    return pl.pallas_call(
        functools.partial(_ffn_sample_kernel, **static),
        out_shape=(jax.ShapeDtypeStruct(x.shape, F32), attn_shape),
        grid_spec=pltpu.PrefetchScalarGridSpec(
            num_scalar_prefetch=1, grid=(db,),
            in_specs=[row] + [whole(a) for a in (gain, wg, wu, wd)] + sample_specs,
            out_specs=(row, attn_spec), scratch_shapes=scratch),
        compiler_params=pltpu.CompilerParams(dimension_semantics=("arbitrary",),
                                             vmem_limit_bytes=VMEM_LIMIT_FUSED),
        name="ffn_sample",
    )(page_table.reshape(-1), x, gain, wg, wu, wd, *sample_args)


def _hgrn_kernel(q_ref, k_ref, v_ref, g_ref, gate_ref, s0_ref, og_ref, tril_ref, pick_ref, rec_ref, sfin_ref,
                 st_ref, cum_ref, *, n_chunks):
    tb = pl.program_id(2)
    heads = range(q_ref.shape[1] // B_DIM)

    @pl.when(tb == 0)
    def _load_state():
        for h in heads:
            st_ref[h] = s0_ref[0, h].T

    head = lambda ref, h: ref.at[:, h * B_DIM:(h + 1) * B_DIM]
    used = n_chunks * HG_CHUNK
    work, terms = [], []
    scored = _in_step([_hgrn_scores(head(q_ref, h), head(k_ref, h), head(v_ref, h), head(g_ref, h), tril_ref,
                                    st_ref[h], head(cum_ref, h), n_chunks) for h in heads])
    for h, (o_blocks, off_blocks, term_blocks, v16_blocks, st) in zip(heads, scored):
        st_ref[h] = st
        work.append((o_blocks, off_blocks, v16_blocks))
        terms += term_blocks
    half = (len(terms) + 1) // 2
    inside = [_dot(jnp.concatenate(part, axis=0), pick_ref[...]) for part in (terms[:half], terms[half:]) if part]
    inside = [a[at * used:(at + 1) * used] for a in inside for at in range(a.shape[0] // used)]
    n_blocks = q_ref.shape[0] // HG_BLOCK
    _in_step([_hgrn_finish(*work[h], inside[h * n_blocks:(h + 1) * n_blocks], head(gate_ref, h), head(og_ref, h),
                           head(rec_ref, h), n_chunks) for h in heads])

    @pl.when(tb == pl.num_programs(2) - 1)
    def _store_state():
        for h in heads:
            sfin_ref[0, h] = st_ref[h].T


def _in_step(generators):
    results, live = [None] * len(generators), list(range(len(generators)))
    while live:
        for c in list(live):
            try:
                next(generators[c])
            except StopIteration as done:
                results[c] = done.value
                live.remove(c)
    return results


def _hgrn_scores(q_ref, k_ref, v_ref, g_ref, tril_ref, st, cum_ref, n_chunks):
    n_blocks = q_ref.shape[0] // HG_BLOCK
    used = n_chunks * HG_CHUNK
    blocks = [slice(bi * HG_BLOCK, (bi + 1) * HG_BLOCK) for bi in range(n_blocks)]

    tril = tril_ref[...]
    g_all = jnp.concatenate([g_ref[rows, :] for rows in blocks], axis=1) * LOG2_E
    g1, g2, g3 = _split3(g_all)
    cum_all = _dot(tril, g1) + _dot(tril, g2) + _dot(tril, g3)
    yield

    o_blocks, off_blocks, term_blocks, v16_blocks = [], [], [], []
    for bi, rows in enumerate(blocks):
        q, k, v = q_ref[rows, :], k_ref[rows, :], v_ref[rows, :]
        cum = cum_all[:, bi * B_DIM:(bi + 1) * B_DIM]
        last = cum[HG_BLOCK - 1:HG_BLOCK, :]
        v16_blocks.append(v.astype(BF16))
        o_blocks.append(_dot_nt((q * jnp.exp2(cum)).astype(BF16), st.astype(BF16)))
        st = st * jnp.exp2(last) + _dot(v.T.astype(BF16), (k * jnp.exp2(last - cum)).astype(BF16))
        parts = [jnp.zeros((HG_CHUNK, B_DIM), F32)]
        for c in range(1, n_chunks):
            lo = c * HG_CHUNK
            r = cum[lo - 1:lo, :]
            k_c = (k[:lo] * jnp.exp2(r - cum[:lo])).astype(BF16)
            k_c = jnp.concatenate([k_c, jnp.zeros((HG_BLOCK - lo, B_DIM), BF16)], axis=0)
            q_c = (q[lo:lo + HG_CHUNK] * jnp.exp2(cum[lo:lo + HG_CHUNK] - r)).astype(BF16)
            parts.append(_dot_nt(q_c, k_c))
        off_blocks.append(jnp.concatenate(parts, axis=0))
        cum_ref[rows, :] = cum
    yield

    assert used == HG_BLOCK or n_blocks == 1
    n_rows = n_blocks * used
    chunk_rows = lambda ref, s: jnp.concatenate(
        [jnp.broadcast_to(ref[c * HG_CHUNK + s:c * HG_CHUNK + s + 1, :], (HG_CHUNK, B_DIM))
         for c in range(n_rows // HG_CHUNK)], axis=0)
    row_in_chunk = lax.broadcasted_iota(jnp.int32, (n_rows, B_DIM), 0) & (HG_CHUNK - 1)
    q_rows, cum_rows = q_ref[:n_rows, :], cum_ref[:n_rows, :]
    terms = []
    for s in range(HG_CHUNK):
        decay = jnp.exp2(jnp.where(row_in_chunk >= s, cum_rows - chunk_rows(cum_ref, s), -jnp.inf))
        terms.append((q_rows * decay * chunk_rows(k_ref, s)).astype(BF16))
    terms = jnp.concatenate(terms, axis=1)
    term_blocks = [terms[bi * used:(bi + 1) * used] for bi in range(n_blocks)]
    return o_blocks, off_blocks, term_blocks, v16_blocks, st


def _hgrn_finish(o_blocks, off_blocks, v16_blocks, inside_blocks, gate_ref, og_ref, rec_ref, n_chunks):
    used = n_chunks * HG_CHUNK
    row = lax.broadcasted_iota(jnp.int32, (used, B_DIM), 0)
    col = lax.broadcasted_iota(jnp.int32, (used, B_DIM), 1)
    same_chunk = _div_pow2(row, HG_CHUNK) == _div_pow2(col, HG_CHUNK)
    ones = jnp.ones((B_DIM, B_DIM), BF16)
    out = []
    for o, off, v16, inside in zip(o_blocks, off_blocks, v16_blocks, inside_blocks):
        score = off + jnp.where(same_chunk, inside, 0.0)
        intra = _dot(score.astype(BF16), v16)
        if used < HG_BLOCK:
            intra = jnp.concatenate([intra, jnp.zeros((HG_BLOCK - used, B_DIM), F32)], axis=0)
        out.append(o + intra)
    yield
    o = jnp.concatenate(out, axis=0)
    ms = _dot((o * o).astype(BF16), ones) * (1.0 / B_DIM)
    gate = gate_ref[...]
    rec_ref[...] = ((o * lax.rsqrt(ms + NORM_EPS) * og_ref[...]) * _silu(gate)).astype(rec_ref.dtype)


def _hgrn(qb, kb, ib, lf, gb, s0, out_gain, batch, seq, tile, n_chunks, heads_per_step):
    nt = seq // tile
    width = heads_per_step * B_DIM
    tok = pl.BlockSpec((tile, width), lambda b, h, t: (b * nt + t, h))
    state = pl.BlockSpec((1, heads_per_step, B_DIM, B_DIM), lambda b, h, t: (b, h, 0, 0))
    idx = jnp.arange(HG_BLOCK)
    tril = (idx[:, None] >= idx[None, :]).astype(BF16)
    pick = (jnp.repeat(jnp.arange(HG_CHUNK), B_DIM)[:, None] == (idx % HG_CHUNK)[None, :]).astype(BF16)
    return pl.pallas_call(
        functools.partial(_hgrn_kernel, n_chunks=n_chunks),
        out_shape=(jax.ShapeDtypeStruct((batch * seq, B_WIDTH), BF16),
                   jax.ShapeDtypeStruct(s0.shape, F32)),
        grid=(batch, B_HEADS // heads_per_step, nt),
        in_specs=[tok, tok, tok, tok, tok, state, pl.BlockSpec((1, width), lambda b, h, t: (0, h)),
                  _resident(tril.shape), _resident(pick.shape)],
        out_specs=(tok, state),
        scratch_shapes=[pltpu.VMEM((heads_per_step, B_DIM, B_DIM), F32),
                        pltpu.VMEM((tile, width), F32)],
        compiler_params=_params(("arbitrary", "arbitrary", "arbitrary")),
        name="hgrn",
    )(qb, kb, ib, lf, gb, s0, out_gain, tril, pick)


def kernel(x_prompt, x_sample, cache_k, cache_v, state_hgrn, page_table, rel_bias_table, lb_logits,
           ffn1_norm, ffn1_gate, ffn1_up, ffn1_down, mix_norm, w_in, q_norm, k_norm, hgrn_out_norm,
           w_out, ffn2_norm, ffn2_gate, ffn2_up, ffn2_down):
    batch, seq, _ = x_prompt.shape
    db, dec_seq, _ = x_sample.shape
    depth, n_phys = cache_k.shape[:2]
    assert depth == 1 and lb_logits.shape[0] == 2
    past = page_table.shape[1] * PAGE_SIZE
    assert seq % MOBA_BLOCK == 0 and past % MOBA_BLOCK == 0 and dec_seq <= HG_CHUNK
    n_p, n_s = batch * seq, db * dec_seq

    bf = lambda w: w[0].astype(BF16)
    gain = lambda g: g[0].reshape(1, -1)
    ffn1 = (gain(ffn1_norm), bf(ffn1_gate), bf(ffn1_up), bf(ffn1_down))
    ffn2 = (gain(ffn2_norm), bf(ffn2_gate), bf(ffn2_up), bf(ffn2_down))
    wo = bf(w_out)
    w_in16 = bf(w_in)
    per_lane = lambda g: jnp.broadcast_to(jnp.tile(g[0], A_HEADS)[:, None], (A_WIDTH, 128))
    proj_w = (gain(mix_norm), w_in16[:, :3 * GROUP_W].T, w_in16[:, 3 * GROUP_W:],
              per_lane(q_norm), per_lane(k_norm), lb_logits)
    out_gain = hgrn_out_norm[0].reshape(1, -1)
    w_bias, sbt, ob = _bias_tables(rel_bias_table.T, seq, past, dec_seq)

    ys = _ffn(x_sample.reshape(n_s, D_MODEL), *ffn1, n_s)
    q_t, k_t, v_t, qb_s, kb_s, ib_s, lf_s, gb_s = _proj(ys, *proj_w, n_s, n_s)
    q_s, k_s, v_s = (a.T.reshape(db, dec_seq, A_WIDTH) for a in (q_t, k_t, v_t))
    pad_rows = lambda a, r: jnp.pad(a.reshape(db, dec_seq, -1), ((0, 0), (0, r - dec_seq), (0, 0)))
    pages = lambda c: c[0].transpose(0, 2, 3, 1).reshape(n_phys, A_WIDTH, PAGE_SIZE)
    sample_attn = functools.partial(
        _moba_sample, q_s, pad_rows(k_s, 8), pad_rows(v_s, 8), pages(cache_k), pages(cache_v), page_table, sbt, ob,
        bpg=math.gcd(past // MOBA_BLOCK, SAMPLE_BLOCKS_PER_STEP))

    tm = 512
    x_p = x_prompt.reshape(n_p, D_MODEL)
    if n_p // tm == db:
        y, attn_s = sample_attn(ffn=(x_p, *ffn1, tm))
    else:
        y, attn_s = _ffn(x_p, *ffn1, tm), sample_attn()
    q_t, kp_t, vp_t, qb, kb, ib, lf, gb = _proj(y, *proj_w, seq, tm)
    attn = _moba_prompt(q_t, kp_t, vp_t, w_bias.reshape(A_HEADS // 2, 2, -1), batch, seq)
    s0 = jnp.zeros((batch, B_HEADS, B_DIM, B_DIM), F32)
    rec, s_p = _hgrn(qb, kb, ib, lf, gb, s0, out_gain, batch, seq, math.gcd(seq, HGRN_TILE), HG_BLOCK // HG_CHUNK, 1)
    y_p = _ffn(y, *ffn2, tm, mix=(attn, rec, wo))
    heads_p = lambda a: a.reshape(batch, A_HEADS, A_HEAD_DIM, seq).transpose(0, 3, 1, 2)[None]

    blocked = lambda a: pad_rows(a, HG_BLOCK).reshape(db * HG_BLOCK, -1)
    rec, s_s = _hgrn(blocked(qb_s), blocked(kb_s), blocked(ib_s), blocked(lf_s), blocked(gb_s), state_hgrn[0],
                     out_gain, db, HG_BLOCK, HG_BLOCK, 1, B_HEADS)
    rec = rec.reshape(db, HG_BLOCK, B_WIDTH)[:, :dec_seq].reshape(n_s, B_WIDTH)
    y_s = _ffn(ys, *ffn2, n_s, mix=(attn_s.reshape(n_s, A_WIDTH), rec, wo))
    heads_s = lambda a: a.reshape(1, db, dec_seq, A_HEADS, A_HEAD_DIM)

    return (y_p.reshape(batch, seq, D_MODEL), y_s.reshape(db, dec_seq, D_MODEL),
            heads_p(kp_t), heads_p(vp_t), s_p[None],
            heads_s(k_s), heads_s(v_s), s_s[None])
```

```python
import functools
import math

import jax
import jax.numpy as jnp
from jax import lax
from jax.experimental import pallas as pl
from jax.experimental.pallas import tpu as pltpu

F32 = jnp.float32
BF16 = jnp.bfloat16

D_MODEL = 1024
A_HEADS = 8
A_HEAD_DIM = 64
A_WIDTH = A_HEADS * A_HEAD_DIM
B_HEADS = 4
B_DIM = 128
B_WIDTH = B_HEADS * B_DIM
GROUP_W = 512
N_GROUPS = 7
D_FF = 2816
MOBA_BLOCK = 256
MOBA_TOPK = 3
PAGE_SIZE = 128
N_BUCKETS = 32
MAX_EXACT = N_BUCKETS // 2
REL_MAX_DIST = 4096
HG_CHUNK = 16
HG_BLOCK = 128
NORM_EPS = 1e-6
NEG_INF = -1e30
ATTN_SCALE = A_HEAD_DIM ** -0.5
LOG2_E = math.log2(math.e)

FF_CHUNK = 256
SAMPLE_BLOCKS_PER_STEP = 8
MOBA_BLOCKS_PER_TRIP = 8
HGRN_TILE = 2048
SAMPLE_SEQS_PER_STEP = 2
MOBA_ONES_ROWS = 16
VMEM_LIMIT = 48 * 1024 * 1024
VMEM_LIMIT_FUSED = 56 * 1024 * 1024


def _resident(shape):
    nd = len(shape)
    return pl.BlockSpec(shape, lambda *_: (0,) * nd, pipeline_mode=pl.Buffered(1))


def _params(semantics):
    return pltpu.CompilerParams(dimension_semantics=semantics, vmem_limit_bytes=VMEM_LIMIT)


def _dot(a, b):
    return jnp.dot(a, b, preferred_element_type=F32)


def _dot_nt(a, b):
    return lax.dot_general(a, b, (((1,), (1,)), ((), ())), preferred_element_type=F32)


def _split3(a):
    a1 = a.astype(BF16)
    r1 = a - a1.astype(F32)
    a2 = r1.astype(BF16)
    a3 = (r1 - a2.astype(F32)).astype(BF16)
    return a1, a2, a3


def _div_pow2(x, d):
    return lax.shift_right_logical(x, int(math.log2(d)))


def _rms_rows(x, gain):
    ms = jnp.mean(x * x, axis=-1, keepdims=True)
    return x * lax.rsqrt(ms + NORM_EPS) * gain


def _silu(x):
    return x * jax.nn.sigmoid(x)


def _ffn_chunks(h, wg_ref, wu_ref, wd_ref, acc, chunks):
    for j in chunks:
        cols = slice(j * FF_CHUNK, (j + 1) * FF_CHUNK)
        g = _dot(h, wg_ref[:, cols])
        u = _dot(h, wu_ref[:, cols])
        acc = acc + _dot((_silu(g) * u).astype(BF16), wd_ref[cols, :])
    return acc


def _ffn_kernel(*refs, with_mix):
    if with_mix:
        x_ref, attn_ref, rec_ref, wo_ref, gain_ref, wg_ref, wu_ref, wd_ref, o_ref = refs
        y = (x_ref[...] + _dot(attn_ref[...], wo_ref[:A_WIDTH, :])
             + _dot(rec_ref[...], wo_ref[A_WIDTH:, :]))
    else:
        x_ref, gain_ref, wg_ref, wu_ref, wd_ref, o_ref = refs
        y = x_ref[...]
    h = _rms_rows(y, gain_ref[...]).astype(BF16)
    acc = _ffn_chunks(h, wg_ref, wu_ref, wd_ref, jnp.zeros(y.shape, F32), range(D_FF // FF_CHUNK))
    o_ref[...] = y + 0.5 * acc


def _ffn_sample_kernel(pt_ref, x_ref, gain_ref, wg_ref, wu_ref, wd_ref, q_ref, kn_ref, vn_ref, sbt_ref, ob_ref,
                       ck_ref, cv_ref, y_ref, attn_ref, *scratch, nbk, dec_seq, bpg):
    stages = _sample_stages(pl.program_id(0), pl.num_programs(0), pt_ref, q_ref, kn_ref, vn_ref, sbt_ref, ob_ref,
                            ck_ref, cv_ref, attn_ref, *scratch, nbk=nbk, dec_seq=dec_seq, bpg=bpg)
    y = x_ref[...]
    h = _rms_rows(y, gain_ref[...]).astype(BF16)
    acc = jnp.zeros(y.shape, F32)
    n_chunks = D_FF // FF_CHUNK
    cuts = [k * n_chunks // len(stages) for k in range(len(stages) + 1)]
    for k, stage in enumerate(stages):
        stage()
        acc = _ffn_chunks(h, wg_ref, wu_ref, wd_ref, acc, range(cuts[k], cuts[k + 1]))
    y_ref[...] = y + 0.5 * acc


def _ffn(x, gain, wg, wu, wd, tm, mix=None):
    n = x.shape[0]
    row = lambda w: pl.BlockSpec((tm, w), lambda i: (i, 0))
    args, specs = [x], [row(D_MODEL)]
    if mix is not None:
        attn, rec, wo = mix
        args += [attn, rec, wo]
        specs += [row(A_WIDTH), row(B_WIDTH), _resident(wo.shape)]
    args += [gain, wg, wu, wd]
    specs += [_resident(gain.shape), _resident(wg.shape), _resident(wu.shape), _resident(wd.shape)]
    return pl.pallas_call(
        functools.partial(_ffn_kernel, with_mix=mix is not None),
        out_shape=jax.ShapeDtypeStruct((n, D_MODEL), F32),
        grid=(n // tm,),
        in_specs=specs,
        out_specs=row(D_MODEL),
        compiler_params=_params(("parallel",)),
        name="ffn_mix" if mix is not None else "ffn",
    )(*args)


def _proj_kernel(y_ref, gain_ref, wat_ref, wb_ref, qg_ref, kg_ref, lbl_ref,
                 q_ref, k_ref, v_ref, qb_ref, kb_ref, ib_ref, lf_ref, gb_ref):
    h = _rms_rows(y_ref[...], gain_ref[...]).astype(BF16)
    tm = h.shape[0]

    def group_t(i):
        return _dot_nt(wat_ref[i * GROUP_W:(i + 1) * GROUP_W, :], h)

    def group(i):
        return _dot(h, wb_ref[:, i * GROUP_W:(i + 1) * GROUP_W])

    def head_norm(z, g):
        out = []
        for hd in range(A_HEADS):
            zh = z[hd * A_HEAD_DIM:(hd + 1) * A_HEAD_DIM, :]
            out.append(zh * lax.rsqrt(jnp.mean(zh * zh, axis=0, keepdims=True) + NORM_EPS))
        return jnp.concatenate(out, axis=0) * jnp.concatenate([g] * pl.cdiv(tm, 128), axis=1)[:, :tm]

    q_ref[...] = head_norm(group_t(0), qg_ref[...]) * ATTN_SCALE
    k_ref[...] = head_norm(group_t(1), kg_ref[...])
    v_ref[...] = group_t(2)
    qb_ref[...] = _silu(group(0))
    lbl = lbl_ref[...]
    e = jnp.exp(lbl - jnp.max(lbl, axis=0, keepdims=True))
    lb = e[0:1, :] / jnp.sum(e, axis=0, keepdims=True)
    forget = lb + (1.0 - lb) * jax.nn.sigmoid(group(1))
    lf_ref[...] = jnp.log(forget)
    kb_ref[...] = 1.0 - forget
    ib_ref[...] = group(2)
    gb_ref[...] = group(3)


def _proj(y, gain, wat, wb, qg, kg, lbl, seq, tm):
    n = y.shape[0]
    nt = seq // tm
    row = lambda w: pl.BlockSpec((tm, w), lambda i: (i, 0))
    col = pl.BlockSpec((GROUP_W, tm), lambda i: (i // nt, i % nt))
    out_t = jax.ShapeDtypeStruct((n // seq * GROUP_W, seq), F32)
    out = jax.ShapeDtypeStruct((n, GROUP_W), F32)
    return pl.pallas_call(
        _proj_kernel,
        out_shape=(out_t,) * 3 + (out,) * 5,
        grid=(n // tm,),
        in_specs=[row(D_MODEL)] + [_resident(a.shape) for a in (gain, wat, wb, qg, kg, lbl)],
        out_specs=(col,) * 3 + (row(GROUP_W),) * 5,
        compiler_params=_params(("parallel",)),
        name="proj",
    )(y, gain, wat, wb, qg, kg, lbl)


def _rel_bucket(n):
    nf = jnp.maximum(n, 1).astype(F32)
    large = MAX_EXACT + (jnp.log(nf / MAX_EXACT) / math.log(REL_MAX_DIST / MAX_EXACT)
                         * (N_BUCKETS - MAX_EXACT)).astype(jnp.int32)
    large = jnp.minimum(large, N_BUCKETS - 1)
    return jnp.where(n < MAX_EXACT, n, large)


def _bias_kernel(tab_ref, w_ref, sbt_ref, ob_ref, *, past, dec_seq):
    tab = tab_ref[...]
    tab_rows = jnp.concatenate([tab] * dec_seq, axis=0)

    def lookup(bucket, table):
        out = jnp.zeros(bucket.shape, F32)
        for b in range(N_BUCKETS):
            out = jnp.where(bucket == b, table[:, b:b + 1], out)
        return out

    c = lax.broadcasted_iota(jnp.int32, w_ref.shape, 1)
    w_ref[...] = lookup(_rel_bucket(jnp.maximum(c - MOBA_BLOCK, 0)), tab)
    t = _div_pow2(lax.broadcasted_iota(jnp.int32, sbt_ref.shape, 0), A_HEADS)
    kpos = lax.broadcasted_iota(jnp.int32, sbt_ref.shape, 1)
    sbt_ref[...] = lookup(_rel_bucket(jnp.maximum(past + t - kpos, 0)), tab_rows)
    t = _div_pow2(lax.broadcasted_iota(jnp.int32, ob_ref.shape, 0), A_HEADS)
    s = lax.broadcasted_iota(jnp.int32, ob_ref.shape, 1)
    ob_ref[...] = lookup(_rel_bucket(jnp.maximum(t - s, 0)), tab_rows)


def _bias_tables(tab_t, seq, past, dec_seq):
    rows = dec_seq * A_HEADS
    return pl.pallas_call(
        functools.partial(_bias_kernel, past=past, dec_seq=dec_seq),
        out_shape=(jax.ShapeDtypeStruct((A_HEADS, seq + MOBA_BLOCK), F32),
                   jax.ShapeDtypeStruct((rows, past), F32),
                   jax.ShapeDtypeStruct((rows, 128), F32)),
        name="rel_bias",
    )(tab_t)


def _top_k_mask(gate, k, axis):
    index = lax.broadcasted_iota(jnp.int32, gate.shape, axis).astype(F32)
    sel = jnp.zeros(gate.shape, F32)
    for _ in range(k):
        mx = jnp.max(gate, axis=axis, keepdims=True)
        first = jnp.min(jnp.where(gate == mx, index, float(gate.shape[axis])), axis=axis, keepdims=True)
        chosen = index == first
        sel = jnp.where(chosen & (mx > -jnp.inf), 1.0, sel)
        gate = jnp.where(chosen, -jnp.inf, gate)
    return sel


def _moba_prompt_kernel(q_ref, k_ref, v_ref, w_ref, o_ref, kb_ref, vt_ref, km_ref, tt_ref, s_ref, *, nb):
    b = pl.program_id(1)
    i = pl.program_id(2)
    blk = MOBA_BLOCK

    key_j = lax.broadcasted_iota(jnp.int32, (blk, blk), 0)
    qry_c = lax.broadcasted_iota(jnp.int32, (blk, blk), 1)

    @pl.when((b == 0) & (i == 0))
    def _build_bias_tiles():
        for hh in range(2):
            for d in range(nb):
                win = jnp.broadcast_to(w_ref[0, hh:hh + 1, d * blk:(d + 2) * blk], (blk, 2 * blk))
                tile = pltpu.roll(win, 0, 1, stride=1, stride_axis=0)[:, blk:] * LOG2_E
                tt_ref[hh, d] = jnp.where(qry_c >= key_j, tile, NEG_INF) if d == 0 else tile

    @pl.when(i == 0)
    def _prepare_sequence():
        lane = lax.broadcasted_iota(jnp.int32, (blk, 128), 1)
        ones = jnp.ones((MOBA_ONES_ROWS, blk), F32)
        for n in range(nb):
            cols = slice(n * blk, (n + 1) * blk)
            kblk = k_ref[:, cols].T
            kb_ref[n] = jnp.concatenate([kblk, jnp.where(lane == n, 1.0, 0.0)], axis=1).astype(BF16)
            km_ref[n:n + 1, :] = jnp.mean(kblk, axis=0, keepdims=True)
            for hh in range(2):
                v_h = v_ref[hh * A_HEAD_DIM:(hh + 1) * A_HEAD_DIM, cols]
                vt_ref[n, hh] = jnp.concatenate([v_h, ones], axis=0).astype(BF16)

    q_t = q_ref[...] * LOG2_E
    feat = lax.broadcasted_iota(jnp.int32, q_t.shape, 0)
    km = km_ref[...]
    km_lane = lax.broadcasted_iota(jnp.int32, km.shape, 1)
    blk_id = lax.broadcasted_iota(jnp.int32, (nb, blk), 0)

    fold = lambda a, op: op(a.reshape(blk // 8, 8, blk), axis=0)

    qp = []
    for hh in range(2):
        lo = hh * A_HEAD_DIM
        q_h = jnp.where((feat >= lo) & (feat < lo + A_HEAD_DIM), q_t, 0.0)
        km_h = jnp.where((km_lane >= lo) & (km_lane < lo + A_HEAD_DIM), km, 0.0)
        k1, k2, _ = _split3(km_h)
        q1, q2, _ = _split3(q_h)
        gate = _dot(k1, q1) + _dot(k1, q2) + _dot(k2, q1)
        gate = jnp.where(blk_id < i, gate, -jnp.inf)
        attends = jnp.where(blk_id == i, 1.0, _top_k_mask(gate, MOBA_TOPK, 0))
        penalty = jnp.concatenate([jnp.where(attends > 0.0, 0.0, NEG_INF),
                                   jnp.zeros((128 - nb, blk), F32)], axis=0)
        qp.append(jnp.concatenate([q1, penalty.astype(BF16)], axis=0))

    per_trip = math.gcd(nb, MOBA_BLOCKS_PER_TRIP)
    full_trips = _div_pow2(i + 1, per_trip)
    left_over = (i + 1) & (per_trip - 1)

    def pieces(left):
        padded = left + 1
        if left > 2 and padded & (padded - 1) == 0:
            return [padded]
        return [bit for bit in (per_trip >> k for k in range(1, per_trip.bit_length())) if left & bit]

    def walk(body, carry):
        carry = lax.fori_loop(0, full_trips, lambda j, c: body(per_trip * j, per_trip, c), carry)

        def finish(left):
            def run(c):
                first = per_trip * full_trips
                for count in pieces(left):
                    c = body(first, count, c)
                    first = first + count
                return c
            return run

        return lax.switch(left_over, [finish(left) for left in range(per_trip)], carry)

    pairs = lambda first, count: [(first + u, hh) for u in range(count) for hh in range(2)]

    def logits(first, count, m8):
        raw = [_dot(kb_ref[n], qp[hh]) for n, hh in pairs(first, count)]
        out = list(m8)
        for (n, hh), r in zip(pairs(first, count), raw):
            s = r + tt_ref[hh, jnp.maximum(i - n, 0)]
            s_ref[hh, n] = s
            out[hh] = jnp.maximum(out[hh], fold(s, jnp.max))
        return tuple(out)

    m8 = walk(logits, (jnp.full((8, blk), -jnp.inf, F32),) * 2)
    m = [jnp.max(a, axis=0, keepdims=True) for a in m8]

    def values(first, count, acc):
        p = [jnp.exp2(s_ref[hh, n] - m[hh]).astype(BF16) for n, hh in pairs(first, count)]
        out = list(acc)
        for (n, hh), ph in zip(pairs(first, count), p):
            out[hh] = out[hh] + _dot(vt_ref[n, hh], ph)
        return tuple(out)

    acc = walk(values, (jnp.zeros((A_HEAD_DIM + MOBA_ONES_ROWS, blk), F32),) * 2)
    out_t = jnp.concatenate([a[:A_HEAD_DIM] / a[A_HEAD_DIM:A_HEAD_DIM + 1] for a in acc], axis=0)
    o_ref[...] = out_t.T.astype(o_ref.dtype)


def _moba_prompt(q_t, k_t, v_t, w_pairs, batch, seq):
    nb = seq // MOBA_BLOCK
    pairs = A_HEADS // 2
    q_tile = pl.BlockSpec((128, MOBA_BLOCK), lambda p, b, i: (b * pairs + p, i))
    whole = pl.BlockSpec((128, seq), lambda p, b, i: (b * pairs + p, 0))
    return pl.pallas_call(
        functools.partial(_moba_prompt_kernel, nb=nb),
        out_shape=jax.ShapeDtypeStruct((batch * seq, A_WIDTH), BF16),
        grid=(pairs, batch, nb),
        in_specs=[q_tile, whole, whole,
                  pl.BlockSpec((1, 2, seq + MOBA_BLOCK), lambda p, b, i: (p, 0, 0))],
        out_specs=pl.BlockSpec((MOBA_BLOCK, 128), lambda p, b, i: (b * nb + i, p)),
        scratch_shapes=[pltpu.VMEM((nb, MOBA_BLOCK, 256), BF16),
                        pltpu.VMEM((nb, 2, A_HEAD_DIM + MOBA_ONES_ROWS, MOBA_BLOCK), BF16),
                        pltpu.VMEM((nb, 128), F32),
                        pltpu.VMEM((2, nb, MOBA_BLOCK, MOBA_BLOCK), F32),
                        pltpu.VMEM((2, nb, MOBA_BLOCK, MOBA_BLOCK), F32)],
        compiler_params=_params(("arbitrary", "arbitrary", "arbitrary")),
        name="moba_prompt",
    )(q_t, k_t, v_t, w_pairs)


def _moba_sample_kernel(*refs, nbk, dec_seq, bpg):
    for stage in _sample_stages(pl.program_id(0), pl.num_programs(0), *refs, nbk=nbk, dec_seq=dec_seq, bpg=bpg):
        stage()


def _sample_stages(b, n_seq, pt_ref, q_ref, kn_ref, vn_ref, sbt_ref, ob_ref, ck_ref, cv_ref, o_ref,
                   kbuf, vbuf, sem, m_sc, l_sc, acc_sc, g_sc, *, nbk, dec_seq, bpg):
    n_groups = nbk // bpg
    ppg = 2 * bpg
    rows = dec_seq * A_HEADS
    head_of_row = lax.broadcasted_iota(jnp.int32, (A_HEADS, A_WIDTH), 0)
    head_of_lane = _div_pow2(lax.broadcasted_iota(jnp.int32, (A_HEADS, A_WIDTH), 1), A_HEAD_DIM)
    head_mask = head_of_row == head_of_lane
    wide = lambda a: jnp.broadcast_to(a, (rows, 128))

    def group_copies(seq, g, slot):
        out = []
        for r in range(ppg):
            page = pt_ref[seq * (n_groups * ppg) + g * ppg + r]
            out.append(pltpu.make_async_copy(ck_ref.at[page], kbuf.at[slot, r], sem.at[slot, 0]))
            out.append(pltpu.make_async_copy(cv_ref.at[page], vbuf.at[slot, r], sem.at[slot, 1]))
        return out

    def queries():
        return jnp.concatenate(
            [jnp.where(head_mask, jnp.broadcast_to(q_ref[0, t:t + 1, :], head_mask.shape), 0.0)
             for t in range(dec_seq)], axis=0).astype(BF16)

    live = {}

    def group_keys(g):
        slot = (b * n_groups + g) & 1
        if g == 0:
            @pl.when(b == 0)
            def _first_group():
                for cp in group_copies(0, 0, 0):
                    cp.start()
        if g + 1 < n_groups:
            for cp in group_copies(b, g + 1, 1 - slot):
                cp.start()
        else:
            @pl.when(b + 1 < n_seq)
            def _next_sequence():
                for cp in group_copies(b + 1, 0, 1 - slot):
                    cp.start()
        for cp in group_copies(b, g, slot):
            cp.wait()
        q16 = queries()
        live["raw"] = jnp.concatenate(
            [_dot(q16, jnp.concatenate([kbuf[slot, 2 * r], kbuf[slot, 2 * r + 1]], axis=1).astype(BF16))
             for r in range(bpg)], axis=1)

    per_block = lambda a: [a[:, r * MOBA_BLOCK:(r + 1) * MOBA_BLOCK] for r in range(bpg)]

    def group_softmax(g):
        width = bpg * MOBA_BLOCK
        s = per_block(live["raw"] + sbt_ref[:, g * width:(g + 1) * width])
        live["m"] = [jnp.max(a, axis=1, keepdims=True) for a in s]
        live["p"] = [jnp.exp(a - mr) for a, mr in zip(s, live["m"])]

    def group_values(g):
        slot = (b * n_groups + g) & 1
        raw, m, p = per_block(live.pop("raw")), live.pop("m"), live.pop("p")
        for r in range(bpg):
            n = g * bpg + r
            vblk = jnp.concatenate([vbuf[slot, 2 * r], vbuf[slot, 2 * r + 1]], axis=1).astype(BF16)
            m_sc[n] = wide(m[r])
            l_sc[n] = wide(jnp.sum(p[r], axis=1, keepdims=True))
            acc_sc[n] = _dot_nt(p[r].astype(BF16), vblk)
            g_sc[n] = wide(jnp.sum(raw[r], axis=1, keepdims=True) * (1.0 / MOBA_BLOCK))

    def merge():
        q16 = queries()
        gates = [g_sc[n] for n in range(nbk)]
        picked = [jnp.zeros((rows, 128), jnp.bool_)] * nbk
        for _ in range(min(MOBA_TOPK, nbk)):
            mx = functools.reduce(jnp.maximum, gates)
            first = functools.reduce(
                jnp.minimum, [jnp.where(g == mx, float(n), float(nbk)) for n, g in enumerate(gates)])
            for n in range(nbk):
                chosen = first == float(n)
                picked[n] = picked[n] | (chosen & (mx > -jnp.inf))
                gates[n] = jnp.where(chosen, -jnp.inf, gates[n])
        zeros = jnp.zeros((128 - kn_ref.shape[1], A_WIDTH), F32)
        k_own = jnp.concatenate([kn_ref[0], zeros], axis=0).astype(BF16)
        v_own = jnp.concatenate([vn_ref[0], zeros], axis=0).astype(BF16)
        s = _dot_nt(q16, k_own) + ob_ref[...]
        t_of_row = _div_pow2(lax.broadcasted_iota(jnp.int32, s.shape, 0), A_HEADS)
        s_idx = lax.broadcasted_iota(jnp.int32, s.shape, 1)
        s = jnp.where(s_idx <= t_of_row, s, NEG_INF)
        m_own = wide(jnp.max(s, axis=1, keepdims=True))
        p = jnp.exp(s - m_own)
        l_own = wide(jnp.sum(p, axis=1, keepdims=True))
        acc_own = _dot(p.astype(BF16), v_own)
        m_all = m_own
        for n in range(nbk):
            m_all = jnp.maximum(m_all, jnp.where(picked[n], m_sc[n], -jnp.inf))
        lanes4 = lambda a: jnp.concatenate([a] * (A_WIDTH // 128), axis=1)
        w = jnp.exp(m_own - m_all)
        l_all = w * l_own
        acc = lanes4(w) * acc_own
        for n in range(nbk):
            w = jnp.where(picked[n], jnp.exp(m_sc[n] - m_all), 0.0)
            l_all = l_all + w * l_sc[n]
            acc = acc + lanes4(w) * acc_sc[n]
        res = acc / lanes4(l_all)
        out = [jnp.sum(jnp.where(head_mask, res[t * A_HEADS:(t + 1) * A_HEADS, :], 0.0),
                       axis=0, keepdims=True) for t in range(dec_seq)]
        o_ref[0] = jnp.concatenate(out, axis=0).astype(o_ref.dtype)

    return [functools.partial(stage, g) for g in range(n_groups)
            for stage in (group_keys, group_softmax, group_values)] + [merge]


def _moba_sample(q, k_new, v_new, cache_k, cache_v, page_table, sbt, ob, bpg, ffn=None):
    db, dec_seq, _ = q.shape
    n_pages = page_table.shape[1]
    nbk = n_pages * PAGE_SIZE // MOBA_BLOCK
    rows = dec_seq * A_HEADS
    assert MOBA_BLOCK == 2 * PAGE_SIZE and nbk % bpg == 0
    per_seq = lambda b, pt: (b, 0, 0)
    whole = lambda a: pl.BlockSpec(a.shape, lambda b, pt: (0,) * a.ndim, pipeline_mode=pl.Buffered(1))
    sample_specs = [pl.BlockSpec((1, dec_seq, A_WIDTH), per_seq),
                    pl.BlockSpec((1, k_new.shape[1], A_WIDTH), per_seq),
                    pl.BlockSpec((1, v_new.shape[1], A_WIDTH), per_seq),
                    whole(sbt), whole(ob),
                    pl.BlockSpec(memory_space=pl.ANY), pl.BlockSpec(memory_space=pl.ANY)]
    sample_args = (q, k_new, v_new, sbt, ob, cache_k, cache_v)
    attn_spec = pl.BlockSpec((1, dec_seq, A_WIDTH), per_seq)
    attn_shape = jax.ShapeDtypeStruct((db, dec_seq, A_WIDTH), BF16)
    scratch = [pltpu.VMEM((2, 2 * bpg, A_WIDTH, PAGE_SIZE), F32),
               pltpu.VMEM((2, 2 * bpg, A_WIDTH, PAGE_SIZE), F32),
               pltpu.SemaphoreType.DMA((2, 2)),
               pltpu.VMEM((nbk, rows, 128), F32),
               pltpu.VMEM((nbk, rows, 128), F32),
               pltpu.VMEM((nbk, rows, A_WIDTH), F32),
               pltpu.VMEM((nbk, rows, 128), F32)]
    static = dict(nbk=nbk, dec_seq=dec_seq, bpg=bpg)
    if ffn is None:
        return pl.pallas_call(
            functools.partial(_moba_sample_kernel, **static),
            out_shape=attn_shape,
            grid_spec=pltpu.PrefetchScalarGridSpec(num_scalar_prefetch=1, grid=(db,), in_specs=sample_specs,
                                                   out_specs=attn_spec, scratch_shapes=scratch),
            compiler_params=_params(("arbitrary",)),
            name="moba_sample",
        )(page_table.reshape(-1), *sample_args)
    x, gain, wg, wu, wd, tm = ffn
    assert x.shape[0] == db * tm
    row = pl.BlockSpec((tm, D_MODEL), lambda b, pt: (b, 0))
    return pl.pallas_call(
        functools.partial(_ffn_sample_kernel, **static),
        out_shape=(jax.ShapeDtypeStruct(x.shape, F32), attn_shape),
        grid_spec=pltpu.PrefetchScalarGridSpec(
            num_scalar_prefetch=1, grid=(db,),
            in_specs=[row] + [whole(a) for a in (gain, wg, wu, wd)] + sample_specs,
            out_specs=(row, attn_spec), scratch_shapes=scratch),
        compiler_params=pltpu.CompilerParams(dimension_semantics=("arbitrary",),
                                             vmem_limit_bytes=VMEM_LIMIT_FUSED),
        name="ffn_sample",
    )(page_table.reshape(-1), x, gain, wg, wu, wd, *sample_args)


def _hgrn_kernel(q_ref, k_ref, v_ref, g_ref, gate_ref, s0_ref, og_ref, tril_ref, pick_ref, rec_ref, sfin_ref,
                 st_ref, cum_ref, *, n_chunks, tile):
    tb = pl.program_id(2)
    n_heads = q_ref.shape[1] // B_DIM
    chains = [(j, h) for j in range(q_ref.shape[0] // tile) for h in range(n_heads)]
    state = lambda j, h: j * n_heads + h

    @pl.when(tb == 0)
    def _load_state():
        for j, h in chains:
            st_ref[state(j, h)] = s0_ref[j, h].T

    lanes = lambda h: slice(h * B_DIM, (h + 1) * B_DIM)
    part = lambda ref, j, h: ref.at[j * tile:(j + 1) * tile, lanes(h)]
    used = n_chunks * HG_CHUNK
    work, terms = [], []
    scored = _in_step([_hgrn_scores(part(q_ref, j, h), part(k_ref, j, h), part(v_ref, j, h), part(g_ref, j, h),
                                    tril_ref, st_ref[state(j, h)], part(cum_ref, j, h), n_chunks)
                       for j, h in chains])
    for (j, h), (o_blocks, off_blocks, term_blocks, v16_blocks, st) in zip(chains, scored):
        st_ref[state(j, h)] = st
        work.append((o_blocks, off_blocks, v16_blocks))
        terms += term_blocks
    half = (len(terms) + 1) // 2
    inside = [_dot(jnp.concatenate(some, axis=0), pick_ref[...]) for some in (terms[:half], terms[half:]) if some]
    inside = [a[at * used:(at + 1) * used] for a in inside for at in range(a.shape[0] // used)]
    n_blocks = tile // HG_BLOCK
    _in_step([_hgrn_finish(*work[c], inside[c * n_blocks:(c + 1) * n_blocks], part(gate_ref, j, h),
                           og_ref.at[:, lanes(h)], part(rec_ref, j, h), n_chunks)
              for c, (j, h) in enumerate(chains)])

    @pl.when(tb == pl.num_programs(2) - 1)
    def _store_state():
        for j, h in chains:
            sfin_ref[j, h] = st_ref[state(j, h)].T


def _in_step(generators):
    results, live = [None] * len(generators), list(range(len(generators)))
    while live:
        for c in list(live):
            try:
                next(generators[c])
            except StopIteration as done:
                results[c] = done.value
                live.remove(c)
    return results


def _hgrn_scores(q_ref, k_ref, v_ref, g_ref, tril_ref, st, cum_ref, n_chunks):
    n_blocks = q_ref.shape[0] // HG_BLOCK
    used = n_chunks * HG_CHUNK
    blocks = [slice(bi * HG_BLOCK, (bi + 1) * HG_BLOCK) for bi in range(n_blocks)]

    tril = tril_ref[...]
    g_all = jnp.concatenate([g_ref[rows, :] for rows in blocks], axis=1) * LOG2_E
    g1, g2, g3 = _split3(g_all)
    cum_all = _dot(tril, g1) + _dot(tril, g2) + _dot(tril, g3)
    yield

    o_blocks, off_blocks, term_blocks, v16_blocks = [], [], [], []
    for bi, rows in enumerate(blocks):
        q, k, v = q_ref[rows, :], k_ref[rows, :], v_ref[rows, :]
        cum = cum_all[:, bi * B_DIM:(bi + 1) * B_DIM]
        last = cum[HG_BLOCK - 1:HG_BLOCK, :]
        v16_blocks.append(v.astype(BF16))
        o_blocks.append(_dot_nt((q * jnp.exp2(cum)).astype(BF16), st.astype(BF16)))
        st = st * jnp.exp2(last) + _dot(v.T.astype(BF16), (k * jnp.exp2(last - cum)).astype(BF16))
        parts = [jnp.zeros((HG_CHUNK, B_DIM), F32)]
        for c in range(1, n_chunks):
            lo = c * HG_CHUNK
            r = cum[lo - 1:lo, :]
            k_c = (k[:lo] * jnp.exp2(r - cum[:lo])).astype(BF16)
            k_c = jnp.concatenate([k_c, jnp.zeros((HG_BLOCK - lo, B_DIM), BF16)], axis=0)
            q_c = (q[lo:lo + HG_CHUNK] * jnp.exp2(cum[lo:lo + HG_CHUNK] - r)).astype(BF16)
            parts.append(_dot_nt(q_c, k_c))
        off_blocks.append(jnp.concatenate(parts, axis=0))
        cum_ref[rows, :] = cum
    yield

    assert used == HG_BLOCK or n_blocks == 1
    n_rows = n_blocks * used
    chunk_rows = lambda ref, s: jnp.concatenate(
        [jnp.broadcast_to(ref[c * HG_CHUNK + s:c * HG_CHUNK + s + 1, :], (HG_CHUNK, B_DIM))
         for c in range(n_rows // HG_CHUNK)], axis=0)
    row_in_chunk = lax.broadcasted_iota(jnp.int32, (n_rows, B_DIM), 0) & (HG_CHUNK - 1)
    q_rows, cum_rows = q_ref[:n_rows, :], cum_ref[:n_rows, :]
    terms = []
    for s in range(HG_CHUNK):
        decay = jnp.exp2(jnp.where(row_in_chunk >= s, cum_rows - chunk_rows(cum_ref, s), -jnp.inf))
        terms.append((q_rows * decay * chunk_rows(k_ref, s)).astype(BF16))
    terms = jnp.concatenate(terms, axis=1)
    term_blocks = [terms[bi * used:(bi + 1) * used] for bi in range(n_blocks)]
    return o_blocks, off_blocks, term_blocks, v16_blocks, st


def _hgrn_finish(o_blocks, off_blocks, v16_blocks, inside_blocks, gate_ref, og_ref, rec_ref, n_chunks):
    used = n_chunks * HG_CHUNK
    row = lax.broadcasted_iota(jnp.int32, (used, B_DIM), 0)
    col = lax.broadcasted_iota(jnp.int32, (used, B_DIM), 1)
    same_chunk = _div_pow2(row, HG_CHUNK) == _div_pow2(col, HG_CHUNK)
    ones = jnp.ones((B_DIM, B_DIM), BF16)
    out = []
    for o, off, v16, inside in zip(o_blocks, off_blocks, v16_blocks, inside_blocks):
        score = off + jnp.where(same_chunk, inside, 0.0)
        intra = _dot(score.astype(BF16), v16)
        if used < HG_BLOCK:
            intra = jnp.concatenate([intra, jnp.zeros((HG_BLOCK - used, B_DIM), F32)], axis=0)
        out.append(o + intra)
    yield
    o = jnp.concatenate(out, axis=0)
    ms = _dot((o * o).astype(BF16), ones) * (1.0 / B_DIM)
    gate = gate_ref[...]
    rec_ref[...] = ((o * lax.rsqrt(ms + NORM_EPS) * og_ref[...]) * _silu(gate)).astype(rec_ref.dtype)


def _hgrn(qb, kb, ib, lf, gb, s0, out_gain, batch, seq, tile, n_chunks, heads_per_step, seqs_per_step=1):
    nt = seq // tile
    assert seqs_per_step == 1 or (nt == 1 and batch % seqs_per_step == 0)
    width = heads_per_step * B_DIM
    tok = pl.BlockSpec((seqs_per_step * tile, width), lambda b, h, t: (b * nt + t, h))
    state = pl.BlockSpec((seqs_per_step, heads_per_step, B_DIM, B_DIM), lambda b, h, t: (b, h, 0, 0))
    idx = jnp.arange(HG_BLOCK)
    tril = (idx[:, None] >= idx[None, :]).astype(BF16)
    pick = (jnp.repeat(jnp.arange(HG_CHUNK), B_DIM)[:, None] == (idx % HG_CHUNK)[None, :]).astype(BF16)
    return pl.pallas_call(
        functools.partial(_hgrn_kernel, n_chunks=n_chunks, tile=tile),
        out_shape=(jax.ShapeDtypeStruct((batch * seq, B_WIDTH), BF16),
                   jax.ShapeDtypeStruct(s0.shape, F32)),
        grid=(batch // seqs_per_step, B_HEADS // heads_per_step, nt),
        in_specs=[tok, tok, tok, tok, tok, state, pl.BlockSpec((1, width), lambda b, h, t: (0, h)),
                  _resident(tril.shape), _resident(pick.shape)],
        out_specs=(tok, state),
        scratch_shapes=[pltpu.VMEM((seqs_per_step * heads_per_step, B_DIM, B_DIM), F32),
                        pltpu.VMEM((seqs_per_step * tile, width), F32)],
        compiler_params=_params(("arbitrary", "arbitrary", "arbitrary")),
        name="hgrn",
    )(qb, kb, ib, lf, gb, s0, out_gain, tril, pick)


def kernel(x_prompt, x_sample, cache_k, cache_v, state_hgrn, page_table, rel_bias_table, lb_logits,
           ffn1_norm, ffn1_gate, ffn1_up, ffn1_down, mix_norm, w_in, q_norm, k_norm, hgrn_out_norm,
           w_out, ffn2_norm, ffn2_gate, ffn2_up, ffn2_down):
    batch, seq, _ = x_prompt.shape
    db, dec_seq, _ = x_sample.shape
    depth, n_phys = cache_k.shape[:2]
    assert depth == 1 and lb_logits.shape[0] == 2
    past = page_table.shape[1] * PAGE_SIZE
    assert seq % MOBA_BLOCK == 0 and past % MOBA_BLOCK == 0 and dec_seq <= HG_CHUNK
    n_p, n_s = batch * seq, db * dec_seq

    bf = lambda w: w[0].astype(BF16)
    gain = lambda g: g[0].reshape(1, -1)
    ffn1 = (gain(ffn1_norm), bf(ffn1_gate), bf(ffn1_up), bf(ffn1_down))
    ffn2 = (gain(ffn2_norm), bf(ffn2_gate), bf(ffn2_up), bf(ffn2_down))
    wo = bf(w_out)
    w_in16 = bf(w_in)
    per_lane = lambda g: jnp.broadcast_to(jnp.tile(g[0], A_HEADS)[:, None], (A_WIDTH, 128))
    proj_w = (gain(mix_norm), w_in16[:, :3 * GROUP_W].T, w_in16[:, 3 * GROUP_W:],
              per_lane(q_norm), per_lane(k_norm), lb_logits)
    out_gain = hgrn_out_norm[0].reshape(1, -1)
    w_bias, sbt, ob = _bias_tables(rel_bias_table.T, seq, past, dec_seq)

    ys = _ffn(x_sample.reshape(n_s, D_MODEL), *ffn1, n_s)
    q_t, k_t, v_t, qb_s, kb_s, ib_s, lf_s, gb_s = _proj(ys, *proj_w, n_s, n_s)
    q_s, k_s, v_s = (a.T.reshape(db, dec_seq, A_WIDTH) for a in (q_t, k_t, v_t))
    pad_rows = lambda a, r: jnp.pad(a.reshape(db, dec_seq, -1), ((0, 0), (0, r - dec_seq), (0, 0)))
    pages = lambda c: c[0].transpose(0, 2, 3, 1).reshape(n_phys, A_WIDTH, PAGE_SIZE)
    sample_attn = functools.partial(
        _moba_sample, q_s, pad_rows(k_s, 8), pad_rows(v_s, 8), pages(cache_k), pages(cache_v), page_table, sbt, ob,
        bpg=math.gcd(past // MOBA_BLOCK, SAMPLE_BLOCKS_PER_STEP))

    tm = 512
    x_p = x_prompt.reshape(n_p, D_MODEL)
    if n_p // tm == db:
        y, attn_s = sample_attn(ffn=(x_p, *ffn1, tm))
    else:
        y, attn_s = _ffn(x_p, *ffn1, tm), sample_attn()
    q_t, kp_t, vp_t, qb, kb, ib, lf, gb = _proj(y, *proj_w, seq, tm)
    attn = _moba_prompt(q_t, kp_t, vp_t, w_bias.reshape(A_HEADS // 2, 2, -1), batch, seq)
    s0 = jnp.zeros((batch, B_HEADS, B_DIM, B_DIM), F32)
    rec, s_p = _hgrn(qb, kb, ib, lf, gb, s0, out_gain, batch, seq, math.gcd(seq, HGRN_TILE), HG_BLOCK // HG_CHUNK, 1)
    y_p = _ffn(y, *ffn2, tm, mix=(attn, rec, wo))
    heads_p = lambda a: a.reshape(batch, A_HEADS, A_HEAD_DIM, seq).transpose(0, 3, 1, 2)[None]

    blocked = lambda a: pad_rows(a, HG_BLOCK).reshape(db * HG_BLOCK, -1)
    rec, s_s = _hgrn(blocked(qb_s), blocked(kb_s), blocked(ib_s), blocked(lf_s), blocked(gb_s), state_hgrn[0],
                     out_gain, db, HG_BLOCK, HG_BLOCK, 1, B_HEADS, seqs_per_step=math.gcd(db, SAMPLE_SEQS_PER_STEP))
    rec = rec.reshape(db, HG_BLOCK, B_WIDTH)[:, :dec_seq].reshape(n_s, B_WIDTH)
    y_s = _ffn(ys, *ffn2, n_s, mix=(attn_s.reshape(n_s, A_WIDTH), rec, wo))
    heads_s = lambda a: a.reshape(1, db, dec_seq, A_HEADS, A_HEAD_DIM)

    return (y_p.reshape(batch, seq, D_MODEL), y_s.reshape(db, dec_seq, D_MODEL),
            heads_p(kp_t), heads_p(vp_t), s_p[None],
            heads_s(k_s), heads_s(v_s), s_s[None])
```
